```python
import math
import jax
import jax.numpy as jnp
from jax import lax
import numpy as np


D_MODEL = 1024
BATCH = 16
SEQ = 4096
DEPTH = 2
DEC_BATCH = 8
DEC_SEQ = 4096
PAST_LEN = 128

A_GROUPS = ((128, 1), (512, 4), (2048, 16))
A_HEADS_PER_GROUP = 8
A_HEADS = A_HEADS_PER_GROUP * len(A_GROUPS)
A_HEAD_DIM = 64
A_WIDTH = A_HEADS * A_HEAD_DIM
A_OUT = A_HEADS_PER_GROUP * A_HEAD_DIM
MLA_HEADS = 8
MLA_NOPE = 64
MLA_ROPE = 32
MLA_QK_DIM = MLA_NOPE + MLA_ROPE
MLA_V = 64
MLA_Q_RANK = 512
MLA_KV_RANK = 256
MLA_OUT = MLA_HEADS * MLA_V
ROPE_THETA = 10000.0
MEM_TOKENS = 256
MEM_HEADS = 4
MEM_HEAD_DIM = 128
MEM_WIDTH = MEM_HEADS * MEM_HEAD_DIM
REL_BUCKETS = 32
REL_MAX_DIST = 1024
N_EXPERTS = 16
EC_CAPACITY_FACTOR = 2
D_FF_EXPERT = 1024
RMS_EPS = 1e-6
Q_BLOCK = 128
NEG_INF = -1e30
COL_QA = 0
COL_KA = COL_QA + A_WIDTH
COL_VA = COL_KA + A_WIDTH
COL_CQ = COL_VA + A_WIDTH
COL_CKV = COL_CQ + MLA_Q_RANK
COL_GA = COL_CKV + MLA_KV_RANK + MLA_ROPE
COL_GB = COL_GA + D_MODEL
IN_COLS = COL_GB + D_MODEL

kernel_name = 'hybrid_dilated_mla_ec_encoder'


def rms_norm(x, g):
    xf = x.astype(jnp.float32)
    y = xf * lax.rsqrt(jnp.mean(xf * xf, axis=-1, keepdims=True) + RMS_EPS)
    return (y * g.astype(jnp.float32)).astype(x.dtype)


def t5_bucket(rel):
    half = REL_BUCKETS // 2
    max_exact = half // 2
    n = jnp.abs(rel)
    base = jnp.where(rel > 0, half, 0)
    nf = jnp.maximum(n, 1).astype(jnp.float32)
    large = max_exact + (jnp.log(nf / max_exact) / math.log(REL_MAX_DIST / max_exact)
                         * (half - max_exact)).astype(jnp.int32)
    large = jnp.minimum(large, half - 1)
    return base + jnp.where(n < max_exact, n, large)


def band_attention(q, k, v, bias_table, dilation, half):
    N, L, H, hd = q.shape
    blk = half
    nb = -(-L // blk)
    Lp = nb * blk
    qb = jnp.pad(q, ((0, 0), (0, Lp - L), (0, 0), (0, 0))).reshape(N, nb, blk, H, hd)
    kv_pad = ((0, 0), (blk, Lp - L + blk), (0, 0), (0, 0))
    kp = jnp.pad(k, kv_pad).reshape(N, nb + 2, blk, H, hd)
    vp = jnp.pad(v, kv_pad).reshape(N, nb + 2, blk, H, hd)
    kw = jnp.concatenate([kp[:, :-2], kp[:, 1:-1], kp[:, 2:]], axis=2)
    vw = jnp.concatenate([vp[:, :-2], vp[:, 1:-1], vp[:, 2:]], axis=2)
    rel = jnp.arange(3 * blk)[None, :] - blk - jnp.arange(blk)[:, None]
    tk = jnp.arange(nb)[:, None] * blk - blk + jnp.arange(3 * blk)[None, :]
    valid = (jnp.abs(rel) <= half)[None] & ((tk >= 0) & (tk < L))[:, None, :]
    bias = jnp.transpose(bias_table[t5_bucket(rel * dilation)], (2, 0, 1)).astype(jnp.float32)
    s = jnp.einsum('nbqhd,nbkhd->nbhqk', qb.astype(jnp.float32), kw.astype(jnp.float32)) * (hd ** -0.5)
    s = jnp.where(valid[None, :, None], s + bias[None, None], NEG_INF)
    lse = jax.nn.logsumexp(s, axis=-1)
    p = jnp.exp(s - lse[..., None])
    o = jnp.einsum('nbhqk,nbkhd->nbqhd', p.astype(v.dtype), vw)
    o = o.reshape(N, Lp, H, hd)[:, :L]
    lse = jnp.transpose(lse, (0, 1, 3, 2)).reshape(N, Lp, H)[:, :L]
    return o, lse


def dilated_group(q, k, v, bias_table, window, dilation):
    B, S, H, hd = q.shape
    L = S // dilation

    def to_strided(t):
        return t.reshape(B, L, dilation, H, hd).transpose(0, 2, 1, 3, 4).reshape(B * dilation, L, H, hd)

    o, lse = band_attention(to_strided(q), to_strided(k), to_strided(v), bias_table,
                            dilation, window // (2 * dilation))
    o = o.reshape(B, dilation, L, H, hd).transpose(0, 2, 1, 3, 4).reshape(B, S, H, hd)
    lse = lse.reshape(B, dilation, L, H).transpose(0, 2, 1, 3).reshape(B, S, H)
    return o, lse


def dilated_mixture(q, k, v, rel_bias):
    outs, lses = [], []
    for g, (window, dilation) in enumerate(A_GROUPS):
        sl = slice(g * A_HEADS_PER_GROUP, (g + 1) * A_HEADS_PER_GROUP)
        o, lse = dilated_group(q[:, :, sl], k[:, :, sl], v[:, :, sl], rel_bias[:, sl], window, dilation)
        outs.append(o)
        lses.append(lse)
    w = jax.nn.softmax(jnp.stack(lses), axis=0)
    return jnp.sum(jnp.stack(outs) * w[..., None].astype(q.dtype), axis=0)


def rope(x, pos):
    half = MLA_ROPE // 2
    freqs = ROPE_THETA ** (-jnp.arange(half, dtype=jnp.float32) / half)
    ang = pos.astype(jnp.float32)[:, None] * freqs[None, :]
    cos = jnp.cos(ang)[None, :, None, :]
    sin = jnp.sin(ang)[None, :, None, :]
    x1 = x[..., :half].astype(jnp.float32)
    x2 = x[..., half:].astype(jnp.float32)
    return jnp.concatenate([x1 * cos - x2 * sin, x1 * sin + x2 * cos], axis=-1).astype(x.dtype)


def blocked_attention(q, k, v):
    B, S, H, dq = q.shape
    qb = q.reshape(B, S // Q_BLOCK, Q_BLOCK, H, dq).transpose(1, 0, 2, 3, 4)
    kf = k.astype(jnp.float32)

    def one_block(q_blk):
        s = jnp.einsum('bqhd,bkhd->bhqk', q_blk.astype(jnp.float32), kf) * (dq ** -0.5)
        p = jax.nn.softmax(s, axis=-1)
        return jnp.einsum('bhqk,bkhd->bqhd', p.astype(v.dtype), v)

    o = lax.map(one_block, qb)
    return o.transpose(1, 0, 2, 3, 4).reshape(B, S, H, v.shape[-1])


def mla(cq_raw, ckv_raw, pos, q_lat_norm, w_uq, kv_lat_norm, w_ukv, q_norm, k_norm):
    B, S, _ = cq_raw.shape
    cq = rms_norm(cq_raw, q_lat_norm)
    q = (cq @ w_uq).reshape(B, S, MLA_HEADS, MLA_QK_DIM)
    ckv = rms_norm(ckv_raw[..., :MLA_KV_RANK], kv_lat_norm)
    k_pe = ckv_raw[..., MLA_KV_RANK:]
    kv = (ckv @ w_ukv).reshape(B, S, MLA_HEADS, MLA_NOPE + MLA_V)
    k_nope = kv[..., :MLA_NOPE]
    v = kv[..., MLA_NOPE:]
    k = jnp.concatenate([k_nope, jnp.broadcast_to(k_pe[:, :, None, :], (B, S, MLA_HEADS, MLA_ROPE))], axis=-1)
    q = rms_norm(q, q_norm)
    k = rms_norm(k, k_norm)
    q = jnp.concatenate([q[..., :MLA_NOPE], rope(q[..., MLA_NOPE:], pos)], axis=-1)
    k = jnp.concatenate([k[..., :MLA_NOPE], rope(k[..., MLA_NOPE:], pos)], axis=-1)
    return blocked_attention(q, k, v).reshape(B, S, MLA_OUT)


def memory_attention(h, mem_h, w_mq, w_mkv, q_norm, k_norm, w_mo):
    B, S, _ = h.shape
    M = mem_h.shape[1]
    q = rms_norm((h @ w_mq).reshape(B, S, MEM_HEADS, MEM_HEAD_DIM), q_norm)
    kv = (mem_h @ w_mkv).reshape(B, M, 2, MEM_HEADS, MEM_HEAD_DIM)
    k = rms_norm(kv[:, :, 0], k_norm)
    v = kv[:, :, 1]
    s = jnp.einsum('bqhd,bkhd->bhqk', q.astype(jnp.float32), k.astype(jnp.float32)) * (MEM_HEAD_DIM ** -0.5)
    p = jax.nn.softmax(s, axis=-1)
    o = jnp.einsum('bhqk,bkhd->bqhd', p.astype(v.dtype), v).reshape(B, S, MEM_WIDTH)
    return o @ w_mo


def ec_moe(h, w_router, w_gate, w_up, w_down):
    B, S, D = h.shape
    n = B * S
    cap = EC_CAPACITY_FACTOR * n // N_EXPERTS
    xt = h.reshape(n, D)
    aff = jax.nn.softmax((xt @ w_router).astype(jnp.float32), axis=-1)
    gates, idx = lax.top_k(aff.T, cap)
    xe = xt[idx]
    hid = jax.nn.silu(jnp.einsum('ecd,edf->ecf', xe, w_gate)) * jnp.einsum('ecd,edf->ecf', xe, w_up)
    ye = jnp.einsum('ecf,efd->ecd', hid, w_down) * gates[..., None].astype(h.dtype)
    out = jnp.zeros_like(xt).at[idx.reshape(-1)].add(ye.reshape(-1, D))
    return out.reshape(B, S, D)


def trunk(x, mem, p):
    B, S, _ = x.shape
    pos = jnp.arange(S)

    def a_heads(t):
        return t.reshape(B, S, A_HEADS, A_HEAD_DIM)

    for l in range(DEPTH):
        h = rms_norm(x, p['norm_mix'][l])
        z = h @ p['w_in'][l]
        qa = rms_norm(a_heads(z[..., COL_QA:COL_KA]), p['a_q_norm'][l])
        ka = rms_norm(a_heads(z[..., COL_KA:COL_VA]), p['a_k_norm'][l])
        va = a_heads(z[..., COL_VA:COL_CQ])
        out_a = dilated_mixture(qa, ka, va, p['rel_bias']).reshape(B, S, A_OUT)
        out_b = mla(z[..., COL_CQ:COL_CKV], z[..., COL_CKV:COL_GA], pos,
                    p['mla_q_lat_norm'][l], p['w_uq'][l], p['mla_kv_lat_norm'][l], p['w_ukv'][l],
                    p['mla_q_norm'][l], p['mla_k_norm'][l])
        gate_a = jax.nn.sigmoid(z[..., COL_GA:COL_GB])
        gate_b = jax.nn.sigmoid(z[..., COL_GB:IN_COLS])
        merged = gate_a * (out_a @ p['w_proj_a'][l]) + gate_b * (out_b @ p['w_proj_b'][l])
        x = x + merged @ p['w_out'][l]
        x = x + memory_attention(rms_norm(x, p['norm_mem_x'][l]), rms_norm(mem, p['norm_mem_kv'][l]),
                                 p['w_mq'][l], p['w_mkv'][l], p['mem_q_norm'][l], p['mem_k_norm'][l],
                                 p['w_mo'][l])
        x = x + ec_moe(rms_norm(x, p['norm_ffn'][l]), p['w_router'][l], p['w_gate'][l],
                       p['w_up'][l], p['w_down'][l])
    return x


def setup_inputs(seed: int = 0) -> dict:
    key = jax.random.key(seed)
    ks = jax.random.split(key, 30)
    L, D, E, F = DEPTH, D_MODEL, N_EXPERTS, D_FF_EXPERT

    def dense(k, shape, fan_in):
        return jax.random.normal(k, shape, jnp.float32) * (fan_in ** -0.5)

    def gain(k, shape):
        return 1.0 + 0.05 * jax.random.normal(k, shape, jnp.float32)

    return {
        'x_prompt': jax.random.normal(ks[0], (BATCH, SEQ, D), jnp.float32),
        'x_sample': jax.random.normal(ks[1], (DEC_BATCH, DEC_SEQ, D), jnp.float32),
        'mem_prompt': jax.random.normal(ks[2], (BATCH, MEM_TOKENS, D), jnp.float32),
        'mem_sample': jax.random.normal(ks[3], (DEC_BATCH, MEM_TOKENS, D), jnp.float32),
        'norm_mix': gain(ks[4], (L, D)),
        'w_in': dense(ks[5], (L, D, IN_COLS), D),
        'a_q_norm': gain(ks[6], (L, A_HEAD_DIM)),
        'a_k_norm': gain(ks[7], (L, A_HEAD_DIM)),
        'rel_bias': 0.5 * jax.random.normal(ks[8], (REL_BUCKETS, A_HEADS), jnp.float32),
        'mla_q_lat_norm': gain(ks[9], (L, MLA_Q_RANK)),
        'w_uq': dense(ks[10], (L, MLA_Q_RANK, MLA_HEADS * MLA_QK_DIM), MLA_Q_RANK),
        'mla_kv_lat_norm': gain(ks[11], (L, MLA_KV_RANK)),
        'w_ukv': dense(ks[12], (L, MLA_KV_RANK, MLA_HEADS * (MLA_NOPE + MLA_V)), MLA_KV_RANK),
        'mla_q_norm': gain(ks[13], (L, MLA_QK_DIM)),
        'mla_k_norm': gain(ks[14], (L, MLA_QK_DIM)),
        'w_proj_a': dense(ks[15], (L, A_OUT, D), A_OUT),
        'w_proj_b': dense(ks[16], (L, MLA_OUT, D), MLA_OUT),
        'w_out': dense(ks[17], (L, D, D), D),
        'norm_mem_x': gain(ks[18], (L, D)),
        'norm_mem_kv': gain(ks[19], (L, D)),
        'w_mq': dense(ks[20], (L, D, MEM_WIDTH), D),
        'w_mkv': dense(ks[21], (L, D, 2 * MEM_WIDTH), D),
        'mem_q_norm': gain(ks[22], (L, MEM_HEAD_DIM)),
        'mem_k_norm': gain(ks[23], (L, MEM_HEAD_DIM)),
        'w_mo': dense(ks[24], (L, MEM_WIDTH, D), MEM_WIDTH),
        'norm_ffn': gain(ks[25], (L, D)),
        'w_router': dense(ks[26], (L, D, E), D),
        'w_gate': dense(ks[27], (L, E, D, F), D),
        'w_up': dense(ks[28], (L, E, D, F), D),
        'w_down': dense(ks[29], (L, E, F, D), F),
    }


def reference(x_prompt, x_sample, mem_prompt, mem_sample, norm_mix, w_in, a_q_norm, a_k_norm,
              rel_bias, mla_q_lat_norm, w_uq, mla_kv_lat_norm, w_ukv, mla_q_norm, mla_k_norm,
              w_proj_a, w_proj_b, w_out, norm_mem_x, norm_mem_kv, w_mq, w_mkv, mem_q_norm,
              mem_k_norm, w_mo, norm_ffn, w_router, w_gate, w_up, w_down):
    params = dict(norm_mix=norm_mix, w_in=w_in, a_q_norm=a_q_norm, a_k_norm=a_k_norm,
                  rel_bias=rel_bias, mla_q_lat_norm=mla_q_lat_norm, w_uq=w_uq,
                  mla_kv_lat_norm=mla_kv_lat_norm, w_ukv=w_ukv, mla_q_norm=mla_q_norm,
                  mla_k_norm=mla_k_norm, w_proj_a=w_proj_a, w_proj_b=w_proj_b, w_out=w_out,
                  norm_mem_x=norm_mem_x, norm_mem_kv=norm_mem_kv, w_mq=w_mq, w_mkv=w_mkv,
                  mem_q_norm=mem_q_norm, mem_k_norm=mem_k_norm, w_mo=w_mo, norm_ffn=norm_ffn,
                  w_router=w_router, w_gate=w_gate, w_up=w_up, w_down=w_down)
    y_prompt = trunk(x_prompt, mem_prompt, params)
    y_sample = trunk(x_sample, mem_sample, params)
    return (y_prompt, y_sample)
```

```python
import functools
import math

import jax
import jax.numpy as jnp
from jax import lax
from jax.experimental import pallas as pl
from jax.experimental.pallas import tpu as pltpu

F32 = jnp.float32
BF16 = jnp.bfloat16

D_MODEL = 1024
DEPTH = 2
A_GROUPS = ((128, 1), (512, 4), (2048, 16))
A_HEADS_PER_GROUP = 8
A_HEAD_DIM = 64
A_GROUP_WIDTH = A_HEADS_PER_GROUP * A_HEAD_DIM
A_WIDTH = A_GROUP_WIDTH * len(A_GROUPS)
A_HALF = 64
MLA_HEADS = 8
MLA_NOPE = 64
MLA_ROPE = 32
MLA_QK_DIM = MLA_NOPE + MLA_ROPE
MLA_V = 64
MLA_Q_RANK = 512
MLA_KV_RANK = 256
MLA_HEAD_PAD = 128
ROPE_THETA = 10000.0
MEM_HEADS = 4
MEM_HEAD_DIM = 128
MEM_WIDTH = MEM_HEADS * MEM_HEAD_DIM
REL_BUCKETS = 32
REL_MAX_DIST = 1024
N_EXPERTS = 16
EC_CAPACITY_FACTOR = 2
RMS_EPS = 1e-6
NEG_INF = -1e30
COL_CQ = 3 * A_WIDTH
COL_CKV = COL_CQ + MLA_Q_RANK
COL_GA = COL_CKV + MLA_KV_RANK + MLA_ROPE
IN_COLS = COL_GA + 2 * D_MODEL

LANE = 128
VMEM_LIMIT = 48 * 1024 * 1024

_NT = (((1,), (1,)), ((), ()))


def _rms(x, g):
    ms = jnp.mean(x * x, axis=-1, keepdims=True)
    return x * lax.rsqrt(ms + RMS_EPS) * g


def _params(sem):
    return pltpu.CompilerParams(dimension_semantics=sem, vmem_limit_bytes=VMEM_LIMIT)


def _proj_kernel(x_ref, g_ref, w_ref, aux_ref, bd_ref, o_ref, h_scr, *, mode, norm_tiles):
    j = pl.program_id(1)

    @pl.when(j == 0)
    def _():
        h_scr[...] = _rms(x_ref[...], g_ref[...]).astype(BF16)

    z = jnp.dot(h_scr[...], w_ref[...], preferred_element_type=F32)
    if mode == "heads":
        @pl.when(j < norm_tiles)
        def _():
            for c in range(z.shape[1] // 256):
                sl = slice(c * 256, (c + 1) * 256)
                zc = z[:, sl]
                ss = jnp.dot((zc * zc).astype(BF16), bd_ref[...], preferred_element_type=F32)
                r = lax.rsqrt(ss * (1.0 / A_HEAD_DIM) + RMS_EPS)
                o_ref[:, sl] = (zc * r * aux_ref[:, sl]).astype(o_ref.dtype)

        @pl.when(j >= norm_tiles)
        def _():
            o_ref[...] = z.astype(o_ref.dtype)
    elif mode == "sigmoid":
        o_ref[...] = jax.nn.sigmoid(z).astype(o_ref.dtype)
    else:
        o_ref[...] = z.astype(o_ref.dtype)


def _norm_proj(x, g, w, aux, bd, *, mode, norm_tiles, tm, tn, out_dtype):
    n, d = x.shape
    ncols = w.shape[1]
    return pl.pallas_call(
        functools.partial(_proj_kernel, mode=mode, norm_tiles=norm_tiles),
        grid=(n // tm, ncols // tn),
        in_specs=[
            pl.BlockSpec((tm, d), lambda i, j: (i, 0)),
            pl.BlockSpec((1, d), lambda i, j: (0, 0)),
            pl.BlockSpec((d, tn), lambda i, j: (0, j)),
            pl.BlockSpec((1, tn), lambda i, j: (0, j)),
            pl.BlockSpec(bd.shape, lambda i, j: (0, 0)),
        ],
        out_specs=pl.BlockSpec((tm, tn), lambda i, j: (i, j)),
        out_shape=jax.ShapeDtypeStruct((n, ncols), out_dtype),
        scratch_shapes=[pltpu.VMEM((tm, d), BF16)],
        compiler_params=_params(("parallel", "arbitrary")),
        name="norm_proj_" + mode,
    )(x, g, w, aux, bd)


def _band_attn_kernel(q_ref, kl_ref, km_ref, kh_ref, vl_ref, vm_ref, vh_ref, bias_ref, hm_ref,
                      o_ref, lse_ref, *, tq, seq_len):
    i = pl.program_id(2)
    tk = tq + 2 * A_HALF
    q = q_ref[...]
    k = jnp.concatenate([kl_ref[...], km_ref[...], kh_ref[...]], axis=0)
    v = jnp.concatenate([vl_ref[...], vm_ref[...], vh_ref[...]], axis=0)
    kpos = i * tq - A_HALF + lax.broadcasted_iota(jnp.int32, (1, tk), 1)
    valid = jnp.logical_and(kpos >= 0, kpos < seq_len)
    first = lax.broadcasted_iota(jnp.int32, (tq, LANE), 1) < A_HEAD_DIM
    o_parts, lse_parts = [], []
    for j in range(A_GROUP_WIDTH // LANE):
        sl = slice(j * LANE, (j + 1) * LANE)
        qp, kp, vp = q[:, sl], k[:, sl], v[:, sl]
        res = []
        for hh in range(2):
            qm = qp * hm_ref[hh:hh + 1, :]
            s = lax.dot_general(qm, kp, _NT, preferred_element_type=F32)
            s = jnp.where(valid, s + bias_ref[2 * j + hh], NEG_INF)
            m = jnp.max(s, axis=-1, keepdims=True)
            p = jnp.exp(s - m)
            l = jnp.sum(p, axis=-1, keepdims=True)
            o = jnp.dot(p.astype(BF16), vp, preferred_element_type=F32) / l
            res.append((o, m + jnp.log(l)))
        o_parts.append(jnp.where(first, res[0][0], res[1][0]))
        lse_parts.append(jnp.where(first, res[0][1], res[1][1]))
    o_ref[...] = jnp.concatenate(o_parts, axis=1).astype(o_ref.dtype)
    lse_ref[...] = jnp.concatenate(lse_parts, axis=1)


def _band_attention(za, bias, hmask, *, batch, seq, group, dilation, tq):
    n = batch * seq
    sl = seq // dilation
    ncb = 3 * len(A_GROUPS)
    zv = za.reshape(batch, sl, dilation * 3 * A_WIDTH)
    hb = tq // A_HALF
    nhb = sl // A_HALF
    w = A_GROUP_WIDTH

    def main(off):
        return pl.BlockSpec((None, tq, w), lambda b, r, i: (b, i, r * ncb + off + group))

    def lo(off):
        return pl.BlockSpec((None, A_HALF, w),
                            lambda b, r, i: (b, jnp.maximum(i * hb - 1, 0), r * ncb + off + group))

    def hi(off):
        return pl.BlockSpec((None, A_HALF, w),
                            lambda b, r, i: (b, jnp.minimum((i + 1) * hb, nhb - 1), r * ncb + off + group))

    out_spec = pl.BlockSpec((None, tq, w), lambda b, r, i: (b, i, r))
    o, lse = pl.pallas_call(
        functools.partial(_band_attn_kernel, tq=tq, seq_len=sl),
        grid=(batch, dilation, sl // tq),
        in_specs=[main(0), lo(3), main(3), hi(3), lo(6), main(6), hi(6),
                  pl.BlockSpec(bias.shape, lambda b, r, i: (0, 0, 0)),
                  pl.BlockSpec(hmask.shape, lambda b, r, i: (0, 0))],
        out_specs=[out_spec, out_spec],
        out_shape=[jax.ShapeDtypeStruct((batch, sl, dilation * w), BF16),
                   jax.ShapeDtypeStruct((batch, sl, dilation * w), F32)],
        compiler_params=_params(("parallel", "parallel", "parallel")),
        name=f"band_attn_g{group}",
    )(zv, zv, zv, zv, zv, zv, zv, bias, hmask)
    return o.reshape(n, w), lse.reshape(n, w)


def _mla_prep_kernel(zc_ref, gql_ref, gkl_ref, wq_ref, wk_ref, wv_ref, gq_ref, gk_ref,
                     c_ref, s1_ref, s2_ref, bd_ref, q_ref, k_ref, v_ref):
    zc = zc_ref[...]
    cq = _rms(zc[:, :MLA_Q_RANK], gql_ref[...]).astype(BF16)
    ckv = _rms(zc[:, MLA_Q_RANK:MLA_Q_RANK + MLA_KV_RANK], gkl_ref[...])
    kin = jnp.concatenate([ckv, zc[:, MLA_Q_RANK + MLA_KV_RANK:]], axis=1).astype(BF16)
    q = jnp.dot(cq, wq_ref[...], preferred_element_type=F32)
    k = jnp.dot(kin, wk_ref[...], preferred_element_type=F32)
    v = jnp.dot(ckv.astype(BF16), wv_ref[...], preferred_element_type=F32)
    v_ref[...] = v.astype(v_ref.dtype)
    cos, s1, s2 = c_ref[...], s1_ref[...], s2_ref[...]
    for src, g_ref, dst in ((q, gq_ref, q_ref), (k, gk_ref, k_ref)):
        for c in range(src.shape[1] // 256):
            sl = slice(c * 256, (c + 1) * 256)
            xc = src[:, sl]
            ss = jnp.dot((xc * xc).astype(BF16), bd_ref[...], preferred_element_type=F32)
            xn = xc * lax.rsqrt(ss * (1.0 / MLA_QK_DIM) + RMS_EPS) * g_ref[:, sl]
            for hh in range(2):
                xh = xn[:, hh * LANE:(hh + 1) * LANE]
                out = (xh * cos + pltpu.roll(xh, LANE - MLA_ROPE // 2, 1) * s1
                       + pltpu.roll(xh, MLA_ROPE // 2, 1) * s2)
                lo = c * 256 + hh * LANE
                dst[:, lo:lo + LANE] = out.astype(dst.dtype)


def _mla_prep(zc, gql, gkl, wq, wk, wv, gq, gk, cos, s1, s2, bd, *, seq, tm):
    n = zc.shape[0]
    hp = MLA_HEADS * MLA_HEAD_PAD
    vw = MLA_HEADS * MLA_V
    nsb = seq // tm
    full = lambda a: pl.BlockSpec(a.shape, lambda i: (0,) * a.ndim)
    tab = pl.BlockSpec((tm, LANE), lambda i: (i % nsb, 0))
    return pl.pallas_call(
        _mla_prep_kernel,
        grid=(n // tm,),
        in_specs=[pl.BlockSpec((tm, zc.shape[1]), lambda i: (i, 0)),
                  full(gql), full(gkl), full(wq), full(wk), full(wv), full(gq), full(gk),
                  tab, tab, tab, full(bd)],
        out_specs=[pl.BlockSpec((tm, hp), lambda i: (i, 0)),
                   pl.BlockSpec((tm, hp), lambda i: (i, 0)),
                   pl.BlockSpec((tm, vw), lambda i: (i, 0))],
        out_shape=[jax.ShapeDtypeStruct((n, hp), BF16),
                   jax.ShapeDtypeStruct((n, hp), BF16),
                   jax.ShapeDtypeStruct((n, vw), BF16)],
        compiler_params=_params(("parallel",)),
        name="mla_prep",
    )(zc, gql, gkl, wq, wk, wv, gq, gk, cos, s1, s2, bd)


def _mla_attn_kernel(q_ref, k_ref, v_ref, o_ref, *, tk):
    tq = q_ref.shape[0]
    nk = k_ref.shape[0] // tk
    outs = []
    for hh in range(2):
        hs = slice(hh * MLA_HEAD_PAD, (hh + 1) * MLA_HEAD_PAD)
        q = q_ref[:, hs]

        def body(c, carry, hs=hs, q=q):
            m, l, acc = carry
            rows = pl.ds(pl.multiple_of(c * tk, tk), tk)
            s = lax.dot_general(q, k_ref[rows, hs], _NT, preferred_element_type=F32)
            m_new = jnp.maximum(m, jnp.max(s, axis=-1, keepdims=True))
            alpha = jnp.exp(m - m_new)
            p = jnp.exp(s - m_new)
            l = alpha * l + jnp.sum(p, axis=-1, keepdims=True)
            acc = alpha * acc + jnp.dot(p.astype(BF16), v_ref[rows, :], preferred_element_type=F32)
            return m_new, l, acc

        m, l, acc = lax.fori_loop(
            0, nk, body,
            (jnp.full((tq, 1), NEG_INF, F32), jnp.zeros((tq, 1), F32), jnp.zeros((tq, LANE), F32)))
        outs.append(acc / l)
    first = lax.broadcasted_iota(jnp.int32, (tq, LANE), 1) < MLA_V
    o_ref[...] = jnp.where(first, outs[0], outs[1]).astype(o_ref.dtype)


def _mla_attention(q, k, v, *, batch, seq, tq, tk):
    n = batch * seq
    nqb = seq // tq
    return pl.pallas_call(
        functools.partial(_mla_attn_kernel, tk=tk),
        grid=(batch, MLA_HEADS // 2, nqb),
        in_specs=[pl.BlockSpec((tq, 2 * MLA_HEAD_PAD), lambda b, h, i: (b * nqb + i, h)),
                  pl.BlockSpec((seq, 2 * MLA_HEAD_PAD), lambda b, h, i: (b, h)),
                  pl.BlockSpec((seq, 2 * MLA_V), lambda b, h, i: (b, h))],
        out_specs=pl.BlockSpec((tq, 2 * MLA_V), lambda b, h, i: (b * nqb + i, h)),
        out_shape=jax.ShapeDtypeStruct((n, MLA_HEADS * MLA_V), BF16),
        compiler_params=_params(("parallel", "parallel", "arbitrary")),
        name="mla_attn",
    )(q, k, v)


def _merge_kernel(o0_ref, l0_ref, o1_ref, l1_ref, o2_ref, l2_ref, ob_ref, g_ref, x_ref,
                  wa_ref, wb_ref, wo_ref, xo_ref):
    l0, l1, l2 = l0_ref[...], l1_ref[...], l2_ref[...]
    m = jnp.maximum(jnp.maximum(l0, l1), l2)
    e0, e1, e2 = jnp.exp(l0 - m), jnp.exp(l1 - m), jnp.exp(l2 - m)
    oa = (e0 * o0_ref[...].astype(F32) + e1 * o1_ref[...].astype(F32)
          + e2 * o2_ref[...].astype(F32)) / (e0 + e1 + e2)
    pa = jnp.dot(oa.astype(BF16), wa_ref[...], preferred_element_type=F32)
    pb = jnp.dot(ob_ref[...], wb_ref[...], preferred_element_type=F32)
    d = pa.shape[1]
    merged = g_ref[:, :d].astype(F32) * pa + g_ref[:, d:].astype(F32) * pb
    xo_ref[...] = x_ref[...] + jnp.dot(merged.astype(BF16), wo_ref[...], preferred_element_type=F32)


def _merge(oas, lses, ob, gates, x, wa, wb, wo, *, tm):
    n, d = x.shape
    row = lambda a: pl.BlockSpec((tm, a.shape[1]), lambda i: (i, 0))
    full = lambda a: pl.BlockSpec(a.shape, lambda i: (0, 0))
    args = [oas[0], lses[0], oas[1], lses[1], oas[2], lses[2], ob, gates, x]
    return pl.pallas_call(
        _merge_kernel,
        grid=(n // tm,),
        in_specs=[row(a) for a in args] + [full(wa), full(wb), full(wo)],
        out_specs=pl.BlockSpec((tm, d), lambda i: (i, 0)),
        out_shape=jax.ShapeDtypeStruct((n, d), F32),
        compiler_params=_params(("parallel",)),
        name="merge",
    )(*args, wa, wb, wo)


def _mem_kv_kernel(mem_ref, g_ref, w_ref, gk_ref, k_ref, v_ref):
    mh = _rms(mem_ref[...], g_ref[...]).astype(BF16)
    kv = jnp.dot(mh, w_ref[...], preferred_element_type=F32)
    for h in range(MEM_HEADS):
        sl = slice(h * MEM_HEAD_DIM, (h + 1) * MEM_HEAD_DIM)
        k_ref[:, sl] = _rms(kv[:, sl], gk_ref[...]).astype(k_ref.dtype)
    v_ref[...] = kv[:, MEM_WIDTH:].astype(v_ref.dtype)


def _mem_kv(mem, g, w, gk, *, tm):
    n = mem.shape[0]
    full = lambda a: pl.BlockSpec(a.shape, lambda i: (0, 0))
    return pl.pallas_call(
        _mem_kv_kernel,
        grid=(n // tm,),
        in_specs=[pl.BlockSpec((tm, mem.shape[1]), lambda i: (i, 0)), full(g), full(w), full(gk)],
        out_specs=[pl.BlockSpec((tm, MEM_WIDTH), lambda i: (i, 0))] * 2,
        out_shape=[jax.ShapeDtypeStruct((n, MEM_WIDTH), BF16)] * 2,
        compiler_params=_params(("parallel",)),
        name="mem_kv",
    )(mem, g, w, gk)


def _mem_attn_kernel(x_ref, k_ref, v_ref, gx_ref, wq_ref, gq_ref, wo_ref, gf_ref, wr_ref,
                     xo_ref, hf_ref, aff_ref):
    x = x_ref[...]
    hx = _rms(x, gx_ref[...]).astype(BF16)
    q = jnp.dot(hx, wq_ref[...], preferred_element_type=F32)
    outs = []
    for h in range(MEM_HEADS):
        sl = slice(h * MEM_HEAD_DIM, (h + 1) * MEM_HEAD_DIM)
        qn = _rms(q[:, sl], gq_ref[...]).astype(BF16)
        s = lax.dot_general(qn, k_ref[:, sl], _NT, preferred_element_type=F32)
        m = jnp.max(s, axis=-1, keepdims=True)
        p = jnp.exp(s - m)
        l = jnp.sum(p, axis=-1, keepdims=True)
        outs.append(jnp.dot(p.astype(BF16), v_ref[:, sl], preferred_element_type=F32) / l)
    o = jnp.concatenate(outs, axis=1).astype(BF16)
    x2 = x + jnp.dot(o, wo_ref[...], preferred_element_type=F32)
    xo_ref[...] = x2
    hf = _rms(x2, gf_ref[...]).astype(BF16)
    hf_ref[...] = hf
    logits = jnp.dot(hf, wr_ref[...], preferred_element_type=F32)
    lane = lax.broadcasted_iota(jnp.int32, logits.shape, 1)
    logits = jnp.where(lane < N_EXPERTS, logits, NEG_INF)
    e = jnp.exp(logits - jnp.max(logits, axis=-1, keepdims=True))
    aff_ref[...] = e / jnp.sum(e, axis=-1, keepdims=True)


def _mem_attention(x, kmem, vmem, gx, wq, gq, wo, gf, wr, *, batch, seq, mem_tokens, tm):
    n, d = x.shape
    nsb = seq // tm
    full = lambda a: pl.BlockSpec(a.shape, lambda b, i: (0, 0))
    row = lambda w: pl.BlockSpec((tm, w), lambda b, i: (b * nsb + i, 0))
    kv = pl.BlockSpec((mem_tokens, MEM_WIDTH), lambda b, i: (b, 0))
    return pl.pallas_call(
        _mem_attn_kernel,
        grid=(batch, nsb),
        in_specs=[row(d), kv, kv, full(gx), full(wq), full(gq), full(wo), full(gf), full(wr)],
        out_specs=[row(d), row(d), row(LANE)],
        out_shape=[jax.ShapeDtypeStruct((n, d), F32),
                   jax.ShapeDtypeStruct((n, d), BF16),
                   jax.ShapeDtypeStruct((n, LANE), F32)],
        compiler_params=_params(("parallel", "parallel")),
        name="mem_attn",
    )(x, kmem, vmem, gx, wq, gq, wo, gf, wr)


def _ffn_kernel(x_ref, wg_ref, wu_ref, wd_ref, g_ref, y_ref):
    x = x_ref[...]
    a = jnp.dot(x, wg_ref[...], preferred_element_type=F32)
    b = jnp.dot(x, wu_ref[...], preferred_element_type=F32)
    hid = (a * jax.nn.sigmoid(a) * b).astype(BF16)
    y = jnp.dot(hid, wd_ref[...], preferred_element_type=F32)
    g = g_ref[...]
    y_ref[...] = y * jnp.concatenate([g] * (y.shape[1] // LANE), axis=1)


def _expert_ffn(xe, wg, wu, wd, gates, *, tm):
    e, cap, d = xe.shape
    f = wg.shape[2]
    return pl.pallas_call(
        _ffn_kernel,
        grid=(e, cap // tm),
        in_specs=[pl.BlockSpec((None, tm, d), lambda e, i: (e, i, 0)),
                  pl.BlockSpec((None, d, f), lambda e, i: (e, 0, 0)),
                  pl.BlockSpec((None, d, f), lambda e, i: (e, 0, 0)),
                  pl.BlockSpec((None, f, d), lambda e, i: (e, 0, 0)),
                  pl.BlockSpec((None, tm, LANE), lambda e, i: (e, i, 0))],
        out_specs=pl.BlockSpec((None, tm, d), lambda e, i: (e, i, 0)),
        out_shape=jax.ShapeDtypeStruct((e, cap, d), F32),
        compiler_params=_params(("parallel", "arbitrary")),
        name="expert_ffn",
    )(xe, wg, wu, wd, gates)


def _t5_bucket(rel):
    half = REL_BUCKETS // 2
    max_exact = half // 2
    n = jnp.abs(rel)
    base = jnp.where(rel > 0, half, 0)
    nf = jnp.maximum(n, 1).astype(F32)
    large = max_exact + (jnp.log(nf / max_exact) / math.log(REL_MAX_DIST / max_exact)
                         * (half - max_exact)).astype(jnp.int32)
    large = jnp.minimum(large, half - 1)
    return base + jnp.where(n < max_exact, n, large)


def _band_bias(rel_bias, group, dilation, tq):
    tk = tq + 2 * A_HALF
    rel = jnp.arange(tk)[None, :] - A_HALF - jnp.arange(tq)[:, None]
    heads = slice(group * A_HEADS_PER_GROUP, (group + 1) * A_HEADS_PER_GROUP)
    b = jnp.transpose(rel_bias[:, heads][_t5_bucket(rel * dilation)], (2, 0, 1)).astype(F32)
    return jnp.where((jnp.abs(rel) <= A_HALF)[None], b, NEG_INF)


def _block_diag_ones(size, block):
    idx = jnp.arange(size) // block
    return (idx[:, None] == idx[None, :]).astype(BF16)


def _rope_tables(seq):
    half = MLA_ROPE // 2
    freqs = ROPE_THETA ** (-jnp.arange(half, dtype=F32) / half)
    ang = jnp.arange(seq).astype(F32)[:, None] * freqs[None, :]
    cos, sin = jnp.cos(ang), jnp.sin(ang)
    ones = jnp.ones((seq, MLA_NOPE), F32)
    z64 = jnp.zeros((seq, MLA_NOPE), F32)
    z16 = jnp.zeros((seq, half), F32)
    z32 = jnp.zeros((seq, LANE - MLA_QK_DIM), F32)
    c = jnp.concatenate([ones, cos, cos, z32], axis=1)
    s1 = jnp.concatenate([z64, -sin, z16, z32], axis=1)
    s2 = jnp.concatenate([z64, z16, sin, z32], axis=1)
    return c, s1, s2


def _pad_heads(w, heads, width, lo, hi):
    w = w.reshape(w.shape[0], heads, width)[:, :, lo:hi]
    w = jnp.pad(w, ((0, 0), (0, 0), (0, MLA_HEAD_PAD - (hi - lo))))
    return w.reshape(w.shape[0], heads * MLA_HEAD_PAD)


def _prep_layer(p, l):
    d = D_MODEL
    row = lambda v: v.reshape(1, -1).astype(F32)
    w_in = p["w_in"][l]
    out = {}
    out["g_mix"] = row(p["norm_mix"][l])
    out["w_a"] = w_in[:, :COL_CQ].astype(BF16)
    out["aux_a"] = row(jnp.concatenate([
        jnp.tile(p["a_q_norm"][l], A_WIDTH // A_HEAD_DIM) * (A_HEAD_DIM ** -0.5),
        jnp.tile(p["a_k_norm"][l], A_WIDTH // A_HEAD_DIM),
        jnp.ones((A_WIDTH,), F32)]))
    out["w_c"] = jnp.pad(w_in[:, COL_CQ:COL_GA], ((0, 0), (0, d - (COL_GA - COL_CQ)))).astype(BF16)
    out["w_g"] = w_in[:, COL_GA:].astype(BF16)
    out["g_ql"] = row(p["mla_q_lat_norm"][l])
    out["g_kl"] = row(p["mla_kv_lat_norm"][l])
    out["w_uq"] = _pad_heads(p["w_uq"][l], MLA_HEADS, MLA_QK_DIM, 0, MLA_QK_DIM).astype(BF16)
    w_ukv = p["w_ukv"][l]
    k_nope = _pad_heads(w_ukv, MLA_HEADS, MLA_NOPE + MLA_V, 0, MLA_NOPE)
    eye = jnp.pad(jnp.eye(MLA_ROPE, dtype=F32), ((0, 0), (MLA_NOPE, MLA_HEAD_PAD - MLA_QK_DIM)))
    k_pe = jnp.tile(eye, (1, MLA_HEADS))
    k_rows = d - MLA_Q_RANK - MLA_KV_RANK - MLA_ROPE
    out["w_uk"] = jnp.concatenate(
        [k_nope, k_pe, jnp.zeros((k_rows, MLA_HEADS * MLA_HEAD_PAD), F32)], axis=0).astype(BF16)
    out["w_uv"] = w_ukv.reshape(MLA_KV_RANK, MLA_HEADS, MLA_NOPE + MLA_V)[:, :, MLA_NOPE:].reshape(
        MLA_KV_RANK, MLA_HEADS * MLA_V).astype(BF16)
    pad_gain = lambda g: jnp.tile(jnp.pad(g, (0, MLA_HEAD_PAD - MLA_QK_DIM)), MLA_HEADS)
    out["g_q"] = row(pad_gain(p["mla_q_norm"][l]) * (MLA_QK_DIM ** -0.5))
    out["g_k"] = row(pad_gain(p["mla_k_norm"][l]))
    out["w_pa"] = p["w_proj_a"][l].astype(BF16)
    out["w_pb"] = p["w_proj_b"][l].astype(BF16)
    out["w_out"] = p["w_out"][l].astype(BF16)
    out["g_mx"] = row(p["norm_mem_x"][l])
    out["g_mkv"] = row(p["norm_mem_kv"][l])
    out["w_mq"] = p["w_mq"][l].astype(BF16)
    out["w_mkv"] = p["w_mkv"][l].astype(BF16)
    out["g_mq"] = row(p["mem_q_norm"][l] * (MEM_HEAD_DIM ** -0.5))
    out["g_mk"] = row(p["mem_k_norm"][l])
    out["w_mo"] = p["w_mo"][l].astype(BF16)
    out["g_ffn"] = row(p["norm_ffn"][l])
    out["w_router"] = jnp.pad(p["w_router"][l], ((0, 0), (0, LANE - N_EXPERTS))).astype(BF16)
    out["w_gate"] = p["w_gate"][l].astype(BF16)
    out["w_up"] = p["w_up"][l].astype(BF16)
    out["w_down"] = p["w_down"][l].astype(BF16)
    return out


BAND_TQ = 128


def _layer(x, mem, lp, shared, *, batch, seq):
    n = batch * seq
    mem_tokens = mem.shape[0] // batch
    tm = min(1024, n)
    bd64, bd128 = shared["bd64"], shared["bd128"]
    za = _norm_proj(x, lp["g_mix"], lp["w_a"], lp["aux_a"], bd64, mode="heads", norm_tiles=2,
                    tm=tm, tn=A_WIDTH, out_dtype=BF16)
    zc = _norm_proj(x, lp["g_mix"], lp["w_c"], lp["aux_a"][:, :D_MODEL], bd64, mode="plain",
                    norm_tiles=0, tm=tm, tn=D_MODEL, out_dtype=F32)
    gates = _norm_proj(x, lp["g_mix"], lp["w_g"], lp["aux_a"][:, :2 * D_MODEL], bd64, mode="sigmoid",
                       norm_tiles=0, tm=tm, tn=D_MODEL, out_dtype=BF16)
    oas, lses = [], []
    for g, (_, dil) in enumerate(A_GROUPS):
        o, lse = _band_attention(za, shared["band_bias"][g], shared["hmask"], batch=batch, seq=seq,
                                 group=g, dilation=dil, tq=BAND_TQ)
        oas.append(o)
        lses.append(lse)
    qm, km, vm = _mla_prep(zc, lp["g_ql"], lp["g_kl"], lp["w_uq"], lp["w_uk"], lp["w_uv"],
                           lp["g_q"], lp["g_k"], shared["rope_c"], shared["rope_s1"],
                           shared["rope_s2"], bd128, seq=seq, tm=min(512, seq))
    ob = _mla_attention(qm, km, vm, batch=batch, seq=seq, tq=min(256, seq), tk=min(512, seq))
    x = _merge(oas, lses, ob, gates, x, lp["w_pa"], lp["w_pb"], lp["w_out"], tm=min(512, n))
    kmem, vmem = _mem_kv(mem, lp["g_mkv"], lp["w_mkv"], lp["g_mk"], tm=min(512, mem.shape[0]))
    x, hf, aff = _mem_attention(x, kmem, vmem, lp["g_mx"], lp["w_mq"], lp["g_mq"], lp["w_mo"],
                                lp["g_ffn"], lp["w_router"], batch=batch, seq=seq,
                                mem_tokens=mem_tokens, tm=min(512, seq))
    cap = EC_CAPACITY_FACTOR * n // N_EXPERTS
    gate_vals, idx = lax.top_k(aff[:, :N_EXPERTS].T, cap)
    xe = hf[idx]
    gb = jnp.broadcast_to(gate_vals[..., None], (N_EXPERTS, cap, LANE))
    ye = _expert_ffn(xe, lp["w_gate"], lp["w_up"], lp["w_down"], gb, tm=min(512, cap))
    return x.at[idx.reshape(-1)].add(ye.reshape(-1, D_MODEL))


def _trunk(x, mem, layers, shared):
    batch, seq, d = x.shape
    xf = x.reshape(batch * seq, d)
    memf = mem.reshape(-1, d)
    for lp in layers:
        xf = _layer(xf, memf, lp, shared, batch=batch, seq=seq)
    return xf.reshape(batch, seq, d)


def kernel(x_prompt, x_sample, mem_prompt, mem_sample, norm_mix, w_in, a_q_norm, a_k_norm, rel_bias,
           mla_q_lat_norm, w_uq, mla_kv_lat_norm, w_ukv, mla_q_norm, mla_k_norm, w_proj_a, w_proj_b,
           w_out, norm_mem_x, norm_mem_kv, w_mq, w_mkv, mem_q_norm, mem_k_norm, w_mo, norm_ffn,
           w_router, w_gate, w_up, w_down):
    p = dict(norm_mix=norm_mix, w_in=w_in, a_q_norm=a_q_norm, a_k_norm=a_k_norm,
             mla_q_lat_norm=mla_q_lat_norm, w_uq=w_uq, mla_kv_lat_norm=mla_kv_lat_norm, w_ukv=w_ukv,
             mla_q_norm=mla_q_norm, mla_k_norm=mla_k_norm, w_proj_a=w_proj_a, w_proj_b=w_proj_b,
             w_out=w_out, norm_mem_x=norm_mem_x, norm_mem_kv=norm_mem_kv, w_mq=w_mq, w_mkv=w_mkv,
             mem_q_norm=mem_q_norm, mem_k_norm=mem_k_norm, w_mo=w_mo, norm_ffn=norm_ffn,
             w_router=w_router, w_gate=w_gate, w_up=w_up, w_down=w_down)
    layers = [_prep_layer(p, l) for l in range(DEPTH)]
    outs = []
    for x, mem in ((x_prompt, mem_prompt), (x_sample, mem_sample)):
        seq = x.shape[1]
        c, s1, s2 = _rope_tables(seq)
        hm = (jnp.arange(LANE)[None, :] // A_HEAD_DIM == jnp.arange(16)[:, None]).astype(BF16)
        shared = dict(
            bd64=_block_diag_ones(256, A_HEAD_DIM), bd128=_block_diag_ones(256, MLA_HEAD_PAD),
            band_bias=[_band_bias(rel_bias, g, dil, BAND_TQ) for g, (_, dil) in enumerate(A_GROUPS)],
            hmask=hm, rope_c=c, rope_s1=s1, rope_s2=s2)
        outs.append(_trunk(x, mem, layers, shared))
    return tuple(outs)
```

```python
import functools
import math

import jax
import jax.numpy as jnp
from jax import lax
from jax.experimental import pallas as pl
from jax.experimental.pallas import tpu as pltpu

F32 = jnp.float32
BF16 = jnp.bfloat16

D_MODEL = 1024
DEPTH = 2
A_GROUPS = ((128, 1), (512, 4), (2048, 16))
A_HEADS_PER_GROUP = 8
A_HEAD_DIM = 64
A_GROUP_WIDTH = A_HEADS_PER_GROUP * A_HEAD_DIM
A_WIDTH = A_GROUP_WIDTH * len(A_GROUPS)
A_HALF = 64
MLA_HEADS = 8
MLA_NOPE = 64
MLA_ROPE = 32
MLA_QK_DIM = MLA_NOPE + MLA_ROPE
MLA_V = 64
MLA_Q_RANK = 512
MLA_KV_RANK = 256
MLA_HEAD_PAD = 128
ROPE_THETA = 10000.0
MEM_HEADS = 4
MEM_HEAD_DIM = 128
MEM_WIDTH = MEM_HEADS * MEM_HEAD_DIM
REL_BUCKETS = 32
REL_MAX_DIST = 1024
N_EXPERTS = 16
EC_CAPACITY_FACTOR = 2
RMS_EPS = 1e-6
NEG_INF = -1e30
COL_CQ = 3 * A_WIDTH
COL_CKV = COL_CQ + MLA_Q_RANK
COL_GA = COL_CKV + MLA_KV_RANK + MLA_ROPE
IN_COLS = COL_GA + 2 * D_MODEL

LANE = 128
VMEM_LIMIT = 48 * 1024 * 1024

_NT = (((1,), (1,)), ((), ()))


def _rms(x, g):
    ms = jnp.mean(x * x, axis=-1, keepdims=True)
    return x * lax.rsqrt(ms + RMS_EPS) * g


def _params(sem):
    return pltpu.CompilerParams(dimension_semantics=sem, vmem_limit_bytes=VMEM_LIMIT)


def _proj_kernel(x_ref, g_ref, w_ref, o_ref, h_scr, *, mode):
    j = pl.program_id(1)

    @pl.when(j == 0)
    def _():
        h_scr[...] = _rms(x_ref[...], g_ref[...]).astype(BF16)

    z = jnp.dot(h_scr[...], w_ref[...], preferred_element_type=F32)
    if mode == "sigmoid":
        z = jax.nn.sigmoid(z)
    o_ref[...] = z.astype(o_ref.dtype)


def _proj_heads_kernel(x_ref, g_ref, w_ref, aux_ref, bd_ref, o0_ref, o1_ref, o2_ref, hf_scr, hp_scr):
    j = pl.program_id(1)
    tm = x_ref.shape[0]
    w = 3 * A_GROUP_WIDTH

    @pl.when(j == 0)
    def _():
        h = _rms(x_ref[...], g_ref[...])
        for c in range(h.shape[1] // LANE):
            hf_scr[c] = h[:, c * LANE:(c + 1) * LANE]
        for g, (_, dil) in enumerate(A_GROUPS):
            if dil == 1:
                hp_scr[g] = h.astype(BF16)
                continue
            rows = tm // dil
            for r in range(dil):
                for c in range(h.shape[1] // LANE):
                    hp_scr[g, r * rows:(r + 1) * rows, c * LANE:(c + 1) * LANE] = (
                        hf_scr[c, pl.ds(r, rows, stride=dil), :].astype(BF16))

    z = jnp.dot(hp_scr[j], w_ref[...], preferred_element_type=F32)
    cols = []
    for c in range(2 * A_GROUP_WIDTH // 256):
        sl = slice(c * 256, (c + 1) * 256)
        zc = z[:, sl]
        ss = jnp.dot((zc * zc).astype(BF16), bd_ref[...], preferred_element_type=F32)
        cols.append((zc * lax.rsqrt(ss * (1.0 / A_HEAD_DIM) + RMS_EPS) * aux_ref[:, sl]).astype(BF16))
    cols.append(z[:, 2 * A_GROUP_WIDTH:].astype(BF16))
    zn = jnp.concatenate(cols, axis=1)
    for g, ((_, dil), o_ref) in enumerate(zip(A_GROUPS, (o0_ref, o1_ref, o2_ref))):
        @pl.when(j == g)
        def _(dil=dil, o_ref=o_ref):
            rows = tm // dil
            for r in range(dil):
                o_ref[:, r * w:(r + 1) * w] = zn[r * rows:(r + 1) * rows, :]


def _norm_proj_heads(x, g, w, aux, bd, *, tm):
    n, d = x.shape
    wg = 3 * A_GROUP_WIDTH
    ng = len(A_GROUPS)
    return pl.pallas_call(
        _proj_heads_kernel,
        grid=(n // tm, ng),
        in_specs=[
            pl.BlockSpec((tm, d), lambda i, j: (i, 0)),
            pl.BlockSpec((1, d), lambda i, j: (0, 0)),
            pl.BlockSpec((d, wg), lambda i, j: (0, j)),
            pl.BlockSpec((1, wg), lambda i, j: (0, j)),
            pl.BlockSpec(bd.shape, lambda i, j: (0, 0)),
        ],
        out_specs=[pl.BlockSpec((tm // dil, dil * wg), lambda i, j: (i, 0)) for _, dil in A_GROUPS],
        out_shape=[jax.ShapeDtypeStruct((n // dil, dil * wg), BF16) for _, dil in A_GROUPS],
        scratch_shapes=[pltpu.VMEM((d // LANE, tm, LANE), F32), pltpu.VMEM((ng, tm, d), BF16)],
        compiler_params=_params(("parallel", "arbitrary")),
        name="norm_proj_heads",
    )(x, g, w, aux, bd)


def _norm_proj(x, g, w, *, mode, tm, tn, out_dtype):
    n, d = x.shape
    ncols = w.shape[1]
    return pl.pallas_call(
        functools.partial(_proj_kernel, mode=mode),
        grid=(n // tm, ncols // tn),
        in_specs=[
            pl.BlockSpec((tm, d), lambda i, j: (i, 0)),
            pl.BlockSpec((1, d), lambda i, j: (0, 0)),
            pl.BlockSpec((d, tn), lambda i, j: (0, j)),
        ],
        out_specs=pl.BlockSpec((tm, tn), lambda i, j: (i, j)),
        out_shape=jax.ShapeDtypeStruct((n, ncols), out_dtype),
        scratch_shapes=[pltpu.VMEM((tm, d), BF16)],
        compiler_params=_params(("parallel", "arbitrary")),
        name="norm_proj_" + mode,
    )(x, g, w)


def _band_attn_kernel(q_ref, kl_ref, km_ref, kh_ref, vl_ref, vm_ref, vh_ref, bias_ref, hm_ref,
                      o_ref, lse_ref, *, tq, seq_len):
    i = pl.program_id(2)
    tk = tq + 2 * A_HALF
    q = q_ref[...]
    k = jnp.concatenate([kl_ref[...], km_ref[...], kh_ref[...]], axis=0)
    v = jnp.concatenate([vl_ref[...], vm_ref[...], vh_ref[...]], axis=0)
    kpos = i * tq - A_HALF + lax.broadcasted_iota(jnp.int32, (1, tk), 1)
    valid = jnp.logical_and(kpos >= 0, kpos < seq_len)
    first = lax.broadcasted_iota(jnp.int32, (tq, LANE), 1) < A_HEAD_DIM
    o_parts, lse_parts = [], []
    for j in range(A_GROUP_WIDTH // LANE):
        sl = slice(j * LANE, (j + 1) * LANE)
        qp, kp, vp = q[:, sl], k[:, sl], v[:, sl]
        res = []
        for hh in range(2):
            qm = qp * hm_ref[hh:hh + 1, :]
            s = lax.dot_general(qm, kp, _NT, preferred_element_type=F32)
            s = jnp.where(valid, s + bias_ref[2 * j + hh], NEG_INF)
            m = jnp.max(s, axis=-1, keepdims=True)
            p = jnp.exp(s - m)
            l = jnp.sum(p, axis=-1, keepdims=True)
            o = jnp.dot(p.astype(BF16), vp, preferred_element_type=F32) / l
            res.append((o, m + jnp.log(l)))
        o_parts.append(jnp.where(first, res[0][0], res[1][0]))
        lse_parts.append(jnp.where(first, res[0][1], res[1][1]))
    o_ref[...] = jnp.concatenate(o_parts, axis=1).astype(o_ref.dtype)
    lse_ref[...] = jnp.concatenate(lse_parts, axis=1)


def _band_attention(zg, bias, hmask, *, batch, seq, group, dilation, tq):
    sl = seq // dilation
    zv = zg.reshape(batch, sl, dilation * 3 * A_GROUP_WIDTH)
    hb = tq // A_HALF
    nhb = sl // A_HALF
    w = A_GROUP_WIDTH

    def main(off):
        return pl.BlockSpec((None, tq, w), lambda b, r, i: (b, i, r * 3 + off))

    def lo(off):
        return pl.BlockSpec((None, A_HALF, w),
                            lambda b, r, i: (b, jnp.maximum(i * hb - 1, 0), r * 3 + off))

    def hi(off):
        return pl.BlockSpec((None, A_HALF, w),
                            lambda b, r, i: (b, jnp.minimum((i + 1) * hb, nhb - 1), r * 3 + off))

    out_spec = pl.BlockSpec((None, tq, w), lambda b, r, i: (b, i, r))
    o, lse = pl.pallas_call(
        functools.partial(_band_attn_kernel, tq=tq, seq_len=sl),
        grid=(batch, dilation, sl // tq),
        in_specs=[main(0), lo(1), main(1), hi(1), lo(2), main(2), hi(2),
                  pl.BlockSpec(bias.shape, lambda b, r, i: (0, 0, 0)),
                  pl.BlockSpec(hmask.shape, lambda b, r, i: (0, 0))],
        out_specs=[out_spec, out_spec],
        out_shape=[jax.ShapeDtypeStruct((batch, sl, dilation * w), BF16),
                   jax.ShapeDtypeStruct((batch, sl, dilation * w), F32)],
        compiler_params=_params(("parallel", "parallel", "parallel")),
        name=f"band_attn_g{group}",
    )(zv, zv, zv, zv, zv, zv, zv, bias, hmask)
    return o.reshape(batch * sl, dilation * w), lse.reshape(batch * sl, dilation * w)


def _mla_prep_kernel(zc_ref, gql_ref, gkl_ref, wq_ref, wk_ref, wv_ref, gq_ref, gk_ref,
                     c_ref, s1_ref, s2_ref, bd_ref, q_ref, k_ref, v_ref):
    zc = zc_ref[...]
    cq = _rms(zc[:, :MLA_Q_RANK], gql_ref[...]).astype(BF16)
    ckv = _rms(zc[:, MLA_Q_RANK:MLA_Q_RANK + MLA_KV_RANK], gkl_ref[...])
    kin = jnp.concatenate([ckv, zc[:, MLA_Q_RANK + MLA_KV_RANK:]], axis=1).astype(BF16)
    q = jnp.dot(cq, wq_ref[...], preferred_element_type=F32)
    k = jnp.dot(kin, wk_ref[...], preferred_element_type=F32)
    v = jnp.dot(ckv.astype(BF16), wv_ref[...], preferred_element_type=F32)
    first = lax.broadcasted_iota(jnp.int32, (v.shape[0], LANE), 1) < MLA_V
    for j in range(MLA_HEADS // 2):
        vp = v[:, j * LANE:(j + 1) * LANE]
        v_ref[:, 2 * j * LANE:(2 * j + 1) * LANE] = jnp.where(first, vp, 1.0).astype(v_ref.dtype)
        v_ref[:, (2 * j + 1) * LANE:(2 * j + 2) * LANE] = jnp.where(first, 1.0, vp).astype(v_ref.dtype)
    cos, s1, s2 = c_ref[...], s1_ref[...], s2_ref[...]
    for src, g_ref, dst in ((q, gq_ref, q_ref), (k, gk_ref, k_ref)):
        for c in range(src.shape[1] // 256):
            sl = slice(c * 256, (c + 1) * 256)
            xc = src[:, sl]
            ss = jnp.dot((xc * xc).astype(BF16), bd_ref[...], preferred_element_type=F32)
            xn = xc * lax.rsqrt(ss * (1.0 / MLA_QK_DIM) + RMS_EPS) * g_ref[:, sl]
            for hh in range(2):
                xh = xn[:, hh * LANE:(hh + 1) * LANE]
                out = (xh * cos + pltpu.roll(xh, LANE - MLA_ROPE // 2, 1) * s1
                       + pltpu.roll(xh, MLA_ROPE // 2, 1) * s2)
                lo = c * 256 + hh * LANE
                dst[:, lo:lo + LANE] = out.astype(dst.dtype)


def _mla_prep(zc, gql, gkl, wq, wk, wv, gq, gk, cos, s1, s2, bd, *, seq, tm):
    n = zc.shape[0]
    hp = MLA_HEADS * MLA_HEAD_PAD
    vw = MLA_HEADS * MLA_HEAD_PAD
    nsb = seq // tm
    full = lambda a: pl.BlockSpec(a.shape, lambda i: (0,) * a.ndim)
    tab = pl.BlockSpec((tm, LANE), lambda i: (i % nsb, 0))
    return pl.pallas_call(
        _mla_prep_kernel,
        grid=(n // tm,),
        in_specs=[pl.BlockSpec((tm, zc.shape[1]), lambda i: (i, 0)),
                  full(gql), full(gkl), full(wq), full(wk), full(wv), full(gq), full(gk),
                  tab, tab, tab, full(bd)],
        out_specs=[pl.BlockSpec((tm, hp), lambda i: (i, 0)),
                   pl.BlockSpec((tm, hp), lambda i: (i, 0)),
                   pl.BlockSpec((tm, vw), lambda i: (i, 0))],
        out_shape=[jax.ShapeDtypeStruct((n, hp), BF16),
                   jax.ShapeDtypeStruct((n, hp), BF16),
                   jax.ShapeDtypeStruct((n, vw), BF16)],
        compiler_params=_params(("parallel",)),
        name="mla_prep",
    )(zc, gql, gkl, wq, wk, wv, gq, gk, cos, s1, s2, bd)


def _mla_attn_kernel(q_ref, k_ref, v_ref, o_ref, sa_scr, sb_scr, p_scr, m_scr, acc_scr, *, tk, rb):
    tq = q_ref.shape[0]
    nk = k_ref.shape[0] // tk
    hs = [slice(hh * MLA_HEAD_PAD, (hh + 1) * MLA_HEAD_PAD) for hh in range(2)]

    def rows_of(c):
        return pl.ds(pl.multiple_of(c * tk, tk), tk)

    def scores(c, dst):
        for hh in range(2):
            dst[hh] = lax.dot_general(q_ref[:, hs[hh]], k_ref[rows_of(c), hs[hh]], _NT,
                                      preferred_element_type=F32)

    def update(c, src):
        for hh in range(2):
            alphas = []
            for r in range(tq // rb):
                rs = slice(r * rb, (r + 1) * rb)
                s = src[hh, rs, :]
                m_old = m_scr[hh, rs, :]
                m_new = jnp.maximum(m_old, jnp.max(s, axis=-1, keepdims=True))
                p_scr[hh, rs, :] = jnp.exp2(s - m_new).astype(BF16)
                m_scr[hh, rs, :] = m_new
                alphas.append(jnp.exp2(m_old - m_new))
            alpha = jnp.concatenate(alphas, axis=0)
            acc_scr[hh] = alpha * acc_scr[hh] + jnp.dot(p_scr[hh], v_ref[rows_of(c), hs[hh]],
                                                        preferred_element_type=F32)

    m_scr[...] = jnp.full(m_scr.shape, NEG_INF, F32)
    acc_scr[...] = jnp.zeros(acc_scr.shape, F32)
    scores(0, sa_scr)

    def body(i, carry):
        scores(2 * i + 1, sb_scr)
        update(2 * i, sa_scr)
        scores(2 * i + 2, sa_scr)
        update(2 * i + 1, sb_scr)
        return carry

    lax.fori_loop(0, nk // 2 - 1, body, 0)
    scores(nk - 1, sb_scr)
    update(nk - 2, sa_scr)
    update(nk - 1, sb_scr)
    acc0, acc1 = acc_scr[0], acc_scr[1]
    o0 = acc0 / pltpu.roll(acc0, MLA_V, 1)
    o1 = acc1 / pltpu.roll(acc1, MLA_V, 1)
    first = lax.broadcasted_iota(jnp.int32, (tq, LANE), 1) < MLA_V
    o_ref[...] = jnp.where(first, o0, o1).astype(o_ref.dtype)


def _mla_attention(q, k, v, *, batch, seq, tq, tk):
    n = batch * seq
    nqb = seq // tq
    assert (seq // tk) % 2 == 0
    return pl.pallas_call(
        functools.partial(_mla_attn_kernel, tk=tk, rb=MLA_ROW_BLOCK),
        grid=(batch, MLA_HEADS // 2, nqb),
        in_specs=[pl.BlockSpec((tq, 2 * MLA_HEAD_PAD), lambda b, h, i: (b * nqb + i, h)),
                  pl.BlockSpec((seq, 2 * MLA_HEAD_PAD), lambda b, h, i: (b, h)),
                  pl.BlockSpec((seq, 2 * MLA_HEAD_PAD), lambda b, h, i: (b, h))],
        out_specs=pl.BlockSpec((tq, 2 * MLA_V), lambda b, h, i: (b * nqb + i, h)),
        out_shape=jax.ShapeDtypeStruct((n, MLA_HEADS * MLA_V), BF16),
        scratch_shapes=[pltpu.VMEM((2, tq, tk), F32), pltpu.VMEM((2, tq, tk), F32),
                        pltpu.VMEM((2, tq, tk), BF16), pltpu.VMEM((2, tq, 1), F32),
                        pltpu.VMEM((2, tq, LANE), F32)],
        compiler_params=_params(("parallel", "parallel", "arbitrary")),
        name="mla_attn",
    )(q, k, v)


def _merge_kernel(o0_ref, l0_ref, o1_ref, l1_ref, o2_ref, l2_ref, ob_ref, g_ref, x_ref,
                  wa_ref, wb_ref, wo_ref, xo_ref, nat_scr):
    tm = x_ref.shape[0]
    w = A_GROUP_WIDTH

    def natural(ref, slot, dil):
        if dil == 1:
            return ref[...].astype(F32)
        rows = tm // dil
        nc = w // LANE
        for r in range(dil):
            for c in range(nc):
                lo = r * w + c * LANE
                nat_scr[slot * nc + c, pl.ds(r, rows, stride=dil), :] = ref[:, lo:lo + LANE].astype(F32)
        return jnp.concatenate([nat_scr[slot * nc + c] for c in range(nc)], axis=1)

    dils = [dil for _, dil in A_GROUPS]
    l0, l1, l2 = (natural(ref, 2 * g, dils[g]) for g, ref in enumerate((l0_ref, l1_ref, l2_ref)))
    o0, o1, o2 = (natural(ref, 2 * g + 1, dils[g]) for g, ref in enumerate((o0_ref, o1_ref, o2_ref)))
    m = jnp.maximum(jnp.maximum(l0, l1), l2)
    e0, e1, e2 = jnp.exp(l0 - m), jnp.exp(l1 - m), jnp.exp(l2 - m)
    oa = (e0 * o0 + e1 * o1 + e2 * o2) / (e0 + e1 + e2)
    pa = jnp.dot(oa.astype(BF16), wa_ref[...], preferred_element_type=F32)
    pb = jnp.dot(ob_ref[...], wb_ref[...], preferred_element_type=F32)
    d = pa.shape[1]
    merged = g_ref[:, :d].astype(F32) * pa + g_ref[:, d:].astype(F32) * pb
    xo_ref[...] = x_ref[...] + jnp.dot(merged.astype(BF16), wo_ref[...], preferred_element_type=F32)


def _merge(oas, lses, ob, gates, x, wa, wb, wo, *, tm):
    n, d = x.shape
    row = lambda a: pl.BlockSpec((tm * a.shape[0] // n, a.shape[1]), lambda i: (i, 0))
    full = lambda a: pl.BlockSpec(a.shape, lambda i: (0, 0))
    args = [oas[0], lses[0], oas[1], lses[1], oas[2], lses[2], ob, gates, x]
    return pl.pallas_call(
        _merge_kernel,
        grid=(n // tm,),
        in_specs=[row(a) for a in args] + [full(wa), full(wb), full(wo)],
        out_specs=pl.BlockSpec((tm, d), lambda i: (i, 0)),
        out_shape=jax.ShapeDtypeStruct((n, d), F32),
        scratch_shapes=[pltpu.VMEM((2 * len(A_GROUPS) * A_GROUP_WIDTH // LANE, tm, LANE), F32)],
        compiler_params=_params(("parallel",)),
        name="merge",
    )(*args, wa, wb, wo)


def _mem_kv_kernel(mem_ref, g_ref, w_ref, gk_ref, k_ref, v_ref):
    mh = _rms(mem_ref[...], g_ref[...]).astype(BF16)
    kv = jnp.dot(mh, w_ref[...], preferred_element_type=F32)
    for h in range(MEM_HEADS):
        sl = slice(h * MEM_HEAD_DIM, (h + 1) * MEM_HEAD_DIM)
        k_ref[:, sl] = _rms(kv[:, sl], gk_ref[...]).astype(k_ref.dtype)
    v_ref[...] = kv[:, MEM_WIDTH:].astype(v_ref.dtype)


def _mem_kv(mem, g, w, gk, *, tm):
    n = mem.shape[0]
    full = lambda a: pl.BlockSpec(a.shape, lambda i: (0, 0))
    return pl.pallas_call(
        _mem_kv_kernel,
        grid=(n // tm,),
        in_specs=[pl.BlockSpec((tm, mem.shape[1]), lambda i: (i, 0)), full(g), full(w), full(gk)],
        out_specs=[pl.BlockSpec((tm, MEM_WIDTH), lambda i: (i, 0))] * 2,
        out_shape=[jax.ShapeDtypeStruct((n, MEM_WIDTH), BF16)] * 2,
        compiler_params=_params(("parallel",)),
        name="mem_kv",
    )(mem, g, w, gk)


def _mem_attn_kernel(x_ref, k_ref, v_ref, gx_ref, wq_ref, gq_ref, wo_ref, gf_ref, wr_ref,
                     xo_ref, hf_ref, aff_ref):
    x = x_ref[...]
    hx = _rms(x, gx_ref[...]).astype(BF16)
    q = jnp.dot(hx, wq_ref[...], preferred_element_type=F32)
    outs = []
    for h in range(MEM_HEADS):
        sl = slice(h * MEM_HEAD_DIM, (h + 1) * MEM_HEAD_DIM)
        qn = _rms(q[:, sl], gq_ref[...]).astype(BF16)
        s = lax.dot_general(qn, k_ref[:, sl], _NT, preferred_element_type=F32)
        m = jnp.max(s, axis=-1, keepdims=True)
        p = jnp.exp(s - m)
        l = jnp.sum(p, axis=-1, keepdims=True)
        outs.append(jnp.dot(p.astype(BF16), v_ref[:, sl], preferred_element_type=F32) / l)
    o = jnp.concatenate(outs, axis=1).astype(BF16)
    x2 = x + jnp.dot(o, wo_ref[...], preferred_element_type=F32)
    xo_ref[...] = x2
    hf = _rms(x2, gf_ref[...]).astype(BF16)
    hf_ref[...] = hf
    logits = jnp.dot(hf, wr_ref[...], preferred_element_type=F32)
    lane = lax.broadcasted_iota(jnp.int32, logits.shape, 1)
    logits = jnp.where(lane < N_EXPERTS, logits, NEG_INF)
    e = jnp.exp(logits - jnp.max(logits, axis=-1, keepdims=True))
    aff_ref[...] = e / jnp.sum(e, axis=-1, keepdims=True)


def _mem_attention(x, kmem, vmem, gx, wq, gq, wo, gf, wr, *, batch, seq, mem_tokens, tm):
    n, d = x.shape
    nsb = seq // tm
    full = lambda a: pl.BlockSpec(a.shape, lambda b, i: (0, 0))
    row = lambda w: pl.BlockSpec((tm, w), lambda b, i: (b * nsb + i, 0))
    kv = pl.BlockSpec((mem_tokens, MEM_WIDTH), lambda b, i: (b, 0))
    return pl.pallas_call(
        _mem_attn_kernel,
        grid=(batch, nsb),
        in_specs=[row(d), kv, kv, full(gx), full(wq), full(gq), full(wo), full(gf), full(wr)],
        out_specs=[row(d), row(d), row(LANE)],
        out_shape=[jax.ShapeDtypeStruct((n, d), F32),
                   jax.ShapeDtypeStruct((n, d), BF16),
                   jax.ShapeDtypeStruct((n, LANE), F32)],
        compiler_params=_params(("parallel", "parallel")),
        name="mem_attn",
    )(x, kmem, vmem, gx, wq, gq, wo, gf, wr)


def _ffn_kernel(x_ref, wg_ref, wu_ref, wd_ref, g_ref, y_ref):
    x = x_ref[...]
    a = jnp.dot(x, wg_ref[...], preferred_element_type=F32)
    b = jnp.dot(x, wu_ref[...], preferred_element_type=F32)
    hid = (a * jax.nn.sigmoid(a) * b).astype(BF16)
    y = jnp.dot(hid, wd_ref[...], preferred_element_type=F32)
    g = g_ref[...]
    y_ref[...] = y * jnp.concatenate([g] * (y.shape[1] // LANE), axis=1)


def _expert_ffn(xe, wg, wu, wd, gates, *, tm):
    e, cap, d = xe.shape
    f = wg.shape[2]
    return pl.pallas_call(
        _ffn_kernel,
        grid=(e, cap // tm),
        in_specs=[pl.BlockSpec((None, tm, d), lambda e, i: (e, i, 0)),
                  pl.BlockSpec((None, d, f), lambda e, i: (e, 0, 0)),
                  pl.BlockSpec((None, d, f), lambda e, i: (e, 0, 0)),
                  pl.BlockSpec((None, f, d), lambda e, i: (e, 0, 0)),
                  pl.BlockSpec((None, tm, LANE), lambda e, i: (e, i, 0))],
        out_specs=pl.BlockSpec((None, tm, d), lambda e, i: (e, i, 0)),
        out_shape=jax.ShapeDtypeStruct((e, cap, d), F32),
        compiler_params=_params(("parallel", "arbitrary")),
        name="expert_ffn",
    )(xe, wg, wu, wd, gates)


def _t5_bucket(rel):
    half = REL_BUCKETS // 2
    max_exact = half // 2
    n = jnp.abs(rel)
    base = jnp.where(rel > 0, half, 0)
    nf = jnp.maximum(n, 1).astype(F32)
    large = max_exact + (jnp.log(nf / max_exact) / math.log(REL_MAX_DIST / max_exact)
                         * (half - max_exact)).astype(jnp.int32)
    large = jnp.minimum(large, half - 1)
    return base + jnp.where(n < max_exact, n, large)


def _band_bias(rel_bias, group, dilation, tq):
    tk = tq + 2 * A_HALF
    rel = jnp.arange(tk)[None, :] - A_HALF - jnp.arange(tq)[:, None]
    heads = slice(group * A_HEADS_PER_GROUP, (group + 1) * A_HEADS_PER_GROUP)
    b = jnp.transpose(rel_bias[:, heads][_t5_bucket(rel * dilation)], (2, 0, 1)).astype(F32)
    return jnp.where((jnp.abs(rel) <= A_HALF)[None], b, NEG_INF)


def _block_diag_ones(size, block):
    idx = jnp.arange(size) // block
    return (idx[:, None] == idx[None, :]).astype(BF16)


def _rope_tables(seq):
    half = MLA_ROPE // 2
    freqs = ROPE_THETA ** (-jnp.arange(half, dtype=F32) / half)
    ang = jnp.arange(seq).astype(F32)[:, None] * freqs[None, :]
    cos, sin = jnp.cos(ang), jnp.sin(ang)
    ones = jnp.ones((seq, MLA_NOPE), F32)
    z64 = jnp.zeros((seq, MLA_NOPE), F32)
    z16 = jnp.zeros((seq, half), F32)
    z32 = jnp.zeros((seq, LANE - MLA_QK_DIM), F32)
    c = jnp.concatenate([ones, cos, cos, z32], axis=1)
    s1 = jnp.concatenate([z64, -sin, z16, z32], axis=1)
    s2 = jnp.concatenate([z64, z16, sin, z32], axis=1)
    return c, s1, s2


def _pad_heads(w, heads, width, lo, hi):
    w = w.reshape(w.shape[0], heads, width)[:, :, lo:hi]
    w = jnp.pad(w, ((0, 0), (0, 0), (0, MLA_HEAD_PAD - (hi - lo))))
    return w.reshape(w.shape[0], heads * MLA_HEAD_PAD)


def _prep_layer(p, l):
    d = D_MODEL
    row = lambda v: v.reshape(1, -1).astype(F32)
    w_in = p["w_in"][l]
    out = {}
    out["g_mix"] = row(p["norm_mix"][l])
    ng = len(A_GROUPS)
    out["w_a"] = w_in[:, :COL_CQ].reshape(d, 3, ng, A_GROUP_WIDTH).transpose(0, 2, 1, 3).reshape(
        d, COL_CQ).astype(BF16)
    out["aux_a"] = row(jnp.tile(jnp.concatenate([
        jnp.tile(p["a_q_norm"][l], A_HEADS_PER_GROUP) * (A_HEAD_DIM ** -0.5),
        jnp.tile(p["a_k_norm"][l], A_HEADS_PER_GROUP),
        jnp.ones((A_GROUP_WIDTH,), F32)]), ng))
    out["w_c"] = jnp.pad(w_in[:, COL_CQ:COL_GA], ((0, 0), (0, d - (COL_GA - COL_CQ)))).astype(BF16)
    out["w_g"] = w_in[:, COL_GA:].astype(BF16)
    out["g_ql"] = row(p["mla_q_lat_norm"][l])
    out["g_kl"] = row(p["mla_kv_lat_norm"][l])
    out["w_uq"] = _pad_heads(p["w_uq"][l], MLA_HEADS, MLA_QK_DIM, 0, MLA_QK_DIM).astype(BF16)
    w_ukv = p["w_ukv"][l]
    k_nope = _pad_heads(w_ukv, MLA_HEADS, MLA_NOPE + MLA_V, 0, MLA_NOPE)
    eye = jnp.pad(jnp.eye(MLA_ROPE, dtype=F32), ((0, 0), (MLA_NOPE, MLA_HEAD_PAD - MLA_QK_DIM)))
    k_pe = jnp.tile(eye, (1, MLA_HEADS))
    k_rows = d - MLA_Q_RANK - MLA_KV_RANK - MLA_ROPE
    out["w_uk"] = jnp.concatenate(
        [k_nope, k_pe, jnp.zeros((k_rows, MLA_HEADS * MLA_HEAD_PAD), F32)], axis=0).astype(BF16)
    out["w_uv"] = w_ukv.reshape(MLA_KV_RANK, MLA_HEADS, MLA_NOPE + MLA_V)[:, :, MLA_NOPE:].reshape(
        MLA_KV_RANK, MLA_HEADS * MLA_V).astype(BF16)
    pad_gain = lambda g: jnp.tile(jnp.pad(g, (0, MLA_HEAD_PAD - MLA_QK_DIM)), MLA_HEADS)
    out["g_q"] = row(pad_gain(p["mla_q_norm"][l]) * (MLA_QK_DIM ** -0.5 * math.log2(math.e)))
    out["g_k"] = row(pad_gain(p["mla_k_norm"][l]))
    out["w_pa"] = p["w_proj_a"][l].astype(BF16)
    out["w_pb"] = p["w_proj_b"][l].astype(BF16)
    out["w_out"] = p["w_out"][l].astype(BF16)
    out["g_mx"] = row(p["norm_mem_x"][l])
    out["g_mkv"] = row(p["norm_mem_kv"][l])
    out["w_mq"] = p["w_mq"][l].astype(BF16)
    out["w_mkv"] = p["w_mkv"][l].astype(BF16)
    out["g_mq"] = row(p["mem_q_norm"][l] * (MEM_HEAD_DIM ** -0.5))
    out["g_mk"] = row(p["mem_k_norm"][l])
    out["w_mo"] = p["w_mo"][l].astype(BF16)
    out["g_ffn"] = row(p["norm_ffn"][l])
    out["w_router"] = jnp.pad(p["w_router"][l], ((0, 0), (0, LANE - N_EXPERTS))).astype(BF16)
    out["w_gate"] = p["w_gate"][l].astype(BF16)
    out["w_up"] = p["w_up"][l].astype(BF16)
    out["w_down"] = p["w_down"][l].astype(BF16)
    return out


BAND_TQ = 128
MLA_ROW_BLOCK = 32


def _layer(x, mem, lp, shared, *, batch, seq):
    n = batch * seq
    mem_tokens = mem.shape[0] // batch
    tm = min(1024, n)
    bd64, bd128 = shared["bd64"], shared["bd128"]
    zgs = _norm_proj_heads(x, lp["g_mix"], lp["w_a"], lp["aux_a"], bd64, tm=min(512, n))
    zc = _norm_proj(x, lp["g_mix"], lp["w_c"], mode="plain", tm=tm, tn=D_MODEL, out_dtype=F32)
    gates = _norm_proj(x, lp["g_mix"], lp["w_g"], mode="sigmoid", tm=tm, tn=D_MODEL, out_dtype=BF16)
    oas, lses = [], []
    for g, (_, dil) in enumerate(A_GROUPS):
        o, lse = _band_attention(zgs[g], shared["band_bias"][g], shared["hmask"], batch=batch, seq=seq,
                                 group=g, dilation=dil, tq=BAND_TQ)
        oas.append(o)
        lses.append(lse)
    qm, km, vm = _mla_prep(zc, lp["g_ql"], lp["g_kl"], lp["w_uq"], lp["w_uk"], lp["w_uv"],
                           lp["g_q"], lp["g_k"], shared["rope_c"], shared["rope_s1"],
                           shared["rope_s2"], bd128, seq=seq, tm=min(512, seq))
    ob = _mla_attention(qm, km, vm, batch=batch, seq=seq, tq=min(512, seq), tk=min(512, seq))
    x = _merge(oas, lses, ob, gates, x, lp["w_pa"], lp["w_pb"], lp["w_out"], tm=min(512, n))
    kmem, vmem = _mem_kv(mem, lp["g_mkv"], lp["w_mkv"], lp["g_mk"], tm=min(512, mem.shape[0]))
    x, hf, aff = _mem_attention(x, kmem, vmem, lp["g_mx"], lp["w_mq"], lp["g_mq"], lp["w_mo"],
                                lp["g_ffn"], lp["w_router"], batch=batch, seq=seq,
                                mem_tokens=mem_tokens, tm=min(512, seq))
    cap = EC_CAPACITY_FACTOR * n // N_EXPERTS
    gate_vals, idx = lax.top_k(aff[:, :N_EXPERTS].T, cap)
    xe = hf[idx]
    gb = jnp.broadcast_to(gate_vals[..., None], (N_EXPERTS, cap, LANE))
    ye = _expert_ffn(xe, lp["w_gate"], lp["w_up"], lp["w_down"], gb, tm=min(512, cap))
    return x.at[idx.reshape(-1)].add(ye.reshape(-1, D_MODEL))


def _trunk(x, mem, layers, shared):
    batch, seq, d = x.shape
    xf = x.reshape(batch * seq, d)
    memf = mem.reshape(-1, d)
    for lp in layers:
        xf = _layer(xf, memf, lp, shared, batch=batch, seq=seq)
    return xf.reshape(batch, seq, d)


def kernel(x_prompt, x_sample, mem_prompt, mem_sample, norm_mix, w_in, a_q_norm, a_k_norm, rel_bias,
           mla_q_lat_norm, w_uq, mla_kv_lat_norm, w_ukv, mla_q_norm, mla_k_norm, w_proj_a, w_proj_b,
           w_out, norm_mem_x, norm_mem_kv, w_mq, w_mkv, mem_q_norm, mem_k_norm, w_mo, norm_ffn,
           w_router, w_gate, w_up, w_down):
    p = dict(norm_mix=norm_mix, w_in=w_in, a_q_norm=a_q_norm, a_k_norm=a_k_norm,
             mla_q_lat_norm=mla_q_lat_norm, w_uq=w_uq, mla_kv_lat_norm=mla_kv_lat_norm, w_ukv=w_ukv,
             mla_q_norm=mla_q_norm, mla_k_norm=mla_k_norm, w_proj_a=w_proj_a, w_proj_b=w_proj_b,
             w_out=w_out, norm_mem_x=norm_mem_x, norm_mem_kv=norm_mem_kv, w_mq=w_mq, w_mkv=w_mkv,
             mem_q_norm=mem_q_norm, mem_k_norm=mem_k_norm, w_mo=w_mo, norm_ffn=norm_ffn,
             w_router=w_router, w_gate=w_gate, w_up=w_up, w_down=w_down)
    layers = [_prep_layer(p, l) for l in range(DEPTH)]
    outs = []
    for x, mem in ((x_prompt, mem_prompt), (x_sample, mem_sample)):
        seq = x.shape[1]
        c, s1, s2 = _rope_tables(seq)
        hm = (jnp.arange(LANE)[None, :] // A_HEAD_DIM == jnp.arange(16)[:, None]).astype(BF16)
        shared = dict(
            bd64=_block_diag_ones(256, A_HEAD_DIM), bd128=_block_diag_ones(256, MLA_HEAD_PAD),
            band_bias=[_band_bias(rel_bias, g, dil, BAND_TQ) for g, (_, dil) in enumerate(A_GROUPS)],
            hmask=hm, rope_c=c, rope_s1=s1, rope_s2=s2)
        outs.append(_trunk(x, mem, layers, shared))
    return tuple(outs)
```

```python
import functools
import math

import jax
import jax.numpy as jnp
from jax import lax
from jax.experimental import pallas as pl
from jax.experimental.pallas import tpu as pltpu

F32 = jnp.float32
BF16 = jnp.bfloat16

D_MODEL = 1024
DEPTH = 2
A_GROUPS = ((128, 1), (512, 4), (2048, 16))
A_HEADS_PER_GROUP = 8
A_HEAD_DIM = 64
A_GROUP_WIDTH = A_HEADS_PER_GROUP * A_HEAD_DIM
A_WIDTH = A_GROUP_WIDTH * len(A_GROUPS)
A_HALF = 64
MLA_HEADS = 8
MLA_NOPE = 64
MLA_ROPE = 32
MLA_QK_DIM = MLA_NOPE + MLA_ROPE
MLA_V = 64
MLA_Q_RANK = 512
MLA_KV_RANK = 256
MLA_HEAD_PAD = 128
ROPE_THETA = 10000.0
MEM_HEADS = 4
MEM_HEAD_DIM = 128
MEM_WIDTH = MEM_HEADS * MEM_HEAD_DIM
REL_BUCKETS = 32
REL_MAX_DIST = 1024
N_EXPERTS = 16
EC_CAPACITY_FACTOR = 2
RMS_EPS = 1e-6
NEG_INF = -1e30
COL_CQ = 3 * A_WIDTH
COL_CKV = COL_CQ + MLA_Q_RANK
COL_GA = COL_CKV + MLA_KV_RANK + MLA_ROPE
IN_COLS = COL_GA + 2 * D_MODEL

LANE = 128
VMEM_LIMIT = 48 * 1024 * 1024

_NT = (((1,), (1,)), ((), ()))


def _rms(x, g):
    ms = jnp.mean(x * x, axis=-1, keepdims=True)
    return x * lax.rsqrt(ms + RMS_EPS) * g


def _params(sem):
    return pltpu.CompilerParams(dimension_semantics=sem, vmem_limit_bytes=VMEM_LIMIT)


def _proj_kernel(x_ref, g_ref, w_ref, o_ref, h_scr, *, mode):
    j = pl.program_id(1)

    @pl.when(j == 0)
    def _():
        h_scr[...] = _rms(x_ref[...], g_ref[...]).astype(BF16)

    z = jnp.dot(h_scr[...], w_ref[...], preferred_element_type=F32)
    if mode == "sigmoid":
        z = jax.nn.sigmoid(z)
    o_ref[...] = z.astype(o_ref.dtype)


def _proj_heads_kernel(x_ref, g_ref, w_ref, aux_ref, bd_ref, o0_ref, o1_ref, o2_ref, hf_scr, hp_scr):
    j = pl.program_id(1)
    tm = x_ref.shape[0]
    w = 3 * A_GROUP_WIDTH

    @pl.when(j == 0)
    def _():
        h = _rms(x_ref[...], g_ref[...])
        for c in range(h.shape[1] // LANE):
            hf_scr[c] = h[:, c * LANE:(c + 1) * LANE]
        for g, (_, dil) in enumerate(A_GROUPS):
            if dil == 1:
                hp_scr[g] = h.astype(BF16)
                continue
            rows = tm // dil
            for r in range(dil):
                for c in range(h.shape[1] // LANE):
                    hp_scr[g, r * rows:(r + 1) * rows, c * LANE:(c + 1) * LANE] = (
                        hf_scr[c, pl.ds(r, rows, stride=dil), :].astype(BF16))

    z = jnp.dot(hp_scr[j], w_ref[...], preferred_element_type=F32)
    cols = []
    for c in range(2 * A_GROUP_WIDTH // 256):
        sl = slice(c * 256, (c + 1) * 256)
        zc = z[:, sl]
        ss = jnp.dot((zc * zc).astype(BF16), bd_ref[...], preferred_element_type=F32)
        cols.append((zc * lax.rsqrt(ss * (1.0 / A_HEAD_DIM) + RMS_EPS) * aux_ref[:, sl]).astype(BF16))
    cols.append(z[:, 2 * A_GROUP_WIDTH:].astype(BF16))
    zn = jnp.concatenate(cols, axis=1)
    for g, ((_, dil), o_ref) in enumerate(zip(A_GROUPS, (o0_ref, o1_ref, o2_ref))):
        @pl.when(j == g)
        def _(dil=dil, o_ref=o_ref):
            rows = tm // dil
            for r in range(dil):
                o_ref[:, r * w:(r + 1) * w] = zn[r * rows:(r + 1) * rows, :]


def _norm_proj_heads(x, g, w, aux, bd, *, tm):
    n, d = x.shape
    wg = 3 * A_GROUP_WIDTH
    ng = len(A_GROUPS)
    return pl.pallas_call(
        _proj_heads_kernel,
        grid=(n // tm, ng),
        in_specs=[
            pl.BlockSpec((tm, d), lambda i, j: (i, 0)),
            pl.BlockSpec((1, d), lambda i, j: (0, 0)),
            pl.BlockSpec((d, wg), lambda i, j: (0, j)),
            pl.BlockSpec((1, wg), lambda i, j: (0, j)),
            pl.BlockSpec(bd.shape, lambda i, j: (0, 0)),
        ],
        out_specs=[pl.BlockSpec((tm // dil, dil * wg), lambda i, j: (i, 0)) for _, dil in A_GROUPS],
        out_shape=[jax.ShapeDtypeStruct((n // dil, dil * wg), BF16) for _, dil in A_GROUPS],
        scratch_shapes=[pltpu.VMEM((d // LANE, tm, LANE), F32), pltpu.VMEM((ng, tm, d), BF16)],
        compiler_params=_params(("parallel", "arbitrary")),
        name="norm_proj_heads",
    )(x, g, w, aux, bd)


def _norm_proj(x, g, w, *, mode, tm, tn, out_dtype):
    n, d = x.shape
    ncols = w.shape[1]
    return pl.pallas_call(
        functools.partial(_proj_kernel, mode=mode),
        grid=(n // tm, ncols // tn),
        in_specs=[
            pl.BlockSpec((tm, d), lambda i, j: (i, 0)),
            pl.BlockSpec((1, d), lambda i, j: (0, 0)),
            pl.BlockSpec((d, tn), lambda i, j: (0, j)),
        ],
        out_specs=pl.BlockSpec((tm, tn), lambda i, j: (i, j)),
        out_shape=jax.ShapeDtypeStruct((n, ncols), out_dtype),
        scratch_shapes=[pltpu.VMEM((tm, d), BF16)],
        compiler_params=_params(("parallel", "arbitrary")),
        name="norm_proj_" + mode,
    )(x, g, w)


def _band_attn_kernel(q_ref, kl_ref, km_ref, kh_ref, vl_ref, vm_ref, vh_ref, bias_ref, hm_ref,
                      o_ref, lse_ref, *, tq, seq_len):
    i = pl.program_id(2)
    tk = tq + 2 * A_HALF
    q = q_ref[...]
    k = jnp.concatenate([kl_ref[...], km_ref[...], kh_ref[...]], axis=0)
    v = jnp.concatenate([vl_ref[...], vm_ref[...], vh_ref[...]], axis=0)
    kpos = i * tq - A_HALF + lax.broadcasted_iota(jnp.int32, (1, tk), 1)
    valid = jnp.logical_and(kpos >= 0, kpos < seq_len)
    first = lax.broadcasted_iota(jnp.int32, (tq, LANE), 1) < A_HEAD_DIM
    o_parts, lse_parts = [], []
    for j in range(A_GROUP_WIDTH // LANE):
        sl = slice(j * LANE, (j + 1) * LANE)
        qp, kp, vp = q[:, sl], k[:, sl], v[:, sl]
        res = []
        for hh in range(2):
            qm = qp * hm_ref[hh:hh + 1, :]
            s = lax.dot_general(qm, kp, _NT, preferred_element_type=F32)
            s = jnp.where(valid, s + bias_ref[2 * j + hh], NEG_INF)
            m = jnp.max(s, axis=-1, keepdims=True)
            p = jnp.exp(s - m)
            l = jnp.sum(p, axis=-1, keepdims=True)
            o = jnp.dot(p.astype(BF16), vp, preferred_element_type=F32) / l
            res.append((o, m + jnp.log(l)))
        o_parts.append(jnp.where(first, res[0][0], res[1][0]))
        lse_parts.append(jnp.where(first, res[0][1], res[1][1]))
    o_ref[...] = jnp.concatenate(o_parts, axis=1).astype(o_ref.dtype)
    lse_ref[...] = jnp.concatenate(lse_parts, axis=1)


def _band_attention(zg, bias, hmask, *, batch, seq, group, dilation, tq):
    sl = seq // dilation
    zv = zg.reshape(batch, sl, dilation * 3 * A_GROUP_WIDTH)
    hb = tq // A_HALF
    nhb = sl // A_HALF
    w = A_GROUP_WIDTH

    def main(off):
        return pl.BlockSpec((None, tq, w), lambda b, r, i: (b, i, r * 3 + off))

    def lo(off):
        return pl.BlockSpec((None, A_HALF, w),
                            lambda b, r, i: (b, jnp.maximum(i * hb - 1, 0), r * 3 + off))

    def hi(off):
        return pl.BlockSpec((None, A_HALF, w),
                            lambda b, r, i: (b, jnp.minimum((i + 1) * hb, nhb - 1), r * 3 + off))

    out_spec = pl.BlockSpec((None, tq, w), lambda b, r, i: (b, i, r))
    o, lse = pl.pallas_call(
        functools.partial(_band_attn_kernel, tq=tq, seq_len=sl),
        grid=(batch, dilation, sl // tq),
        in_specs=[main(0), lo(1), main(1), hi(1), lo(2), main(2), hi(2),
                  pl.BlockSpec(bias.shape, lambda b, r, i: (0, 0, 0)),
                  pl.BlockSpec(hmask.shape, lambda b, r, i: (0, 0))],
        out_specs=[out_spec, out_spec],
        out_shape=[jax.ShapeDtypeStruct((batch, sl, dilation * w), BF16),
                   jax.ShapeDtypeStruct((batch, sl, dilation * w), F32)],
        compiler_params=_params(("parallel", "parallel", "parallel")),
        name=f"band_attn_g{group}",
    )(zv, zv, zv, zv, zv, zv, zv, bias, hmask)
    return o.reshape(batch * sl, dilation * w), lse.reshape(batch * sl, dilation * w)


def _mla_prep_kernel(zc_ref, gql_ref, gkl_ref, wq_ref, wk_ref, wv_ref, gq_ref, gk_ref,
                     c_ref, s1_ref, s2_ref, bd_ref, q_ref, k_ref, v_ref):
    zc = zc_ref[...]
    cq = _rms(zc[:, :MLA_Q_RANK], gql_ref[...]).astype(BF16)
    ckv = _rms(zc[:, MLA_Q_RANK:MLA_Q_RANK + MLA_KV_RANK], gkl_ref[...])
    kin = jnp.concatenate([ckv, zc[:, MLA_Q_RANK + MLA_KV_RANK:]], axis=1).astype(BF16)
    q = jnp.dot(cq, wq_ref[...], preferred_element_type=F32)
    k = jnp.dot(kin, wk_ref[...], preferred_element_type=F32)
    v = jnp.dot(ckv.astype(BF16), wv_ref[...], preferred_element_type=F32)
    first = lax.broadcasted_iota(jnp.int32, (v.shape[0], LANE), 1) < MLA_V
    for j in range(MLA_HEADS // 2):
        vp = v[:, j * LANE:(j + 1) * LANE]
        v_ref[:, 2 * j * LANE:(2 * j + 1) * LANE] = jnp.where(first, vp, 1.0).astype(v_ref.dtype)
        v_ref[:, (2 * j + 1) * LANE:(2 * j + 2) * LANE] = jnp.where(first, 1.0, vp).astype(v_ref.dtype)
    cos, s1, s2 = c_ref[...], s1_ref[...], s2_ref[...]
    for src, g_ref, dst in ((q, gq_ref, q_ref), (k, gk_ref, k_ref)):
        for c in range(src.shape[1] // 256):
            sl = slice(c * 256, (c + 1) * 256)
            xc = src[:, sl]
            ss = jnp.dot((xc * xc).astype(BF16), bd_ref[...], preferred_element_type=F32)
            xn = xc * lax.rsqrt(ss * (1.0 / MLA_QK_DIM) + RMS_EPS) * g_ref[:, sl]
            for hh in range(2):
                xh = xn[:, hh * LANE:(hh + 1) * LANE]
                out = (xh * cos + pltpu.roll(xh, LANE - MLA_ROPE // 2, 1) * s1
                       + pltpu.roll(xh, MLA_ROPE // 2, 1) * s2)
                lo = c * 256 + hh * LANE
                dst[:, lo:lo + LANE] = out.astype(dst.dtype)


def _mla_prep(zc, gql, gkl, wq, wk, wv, gq, gk, cos, s1, s2, bd, *, seq, tm):
    n = zc.shape[0]
    hp = MLA_HEADS * MLA_HEAD_PAD
    vw = MLA_HEADS * MLA_HEAD_PAD
    nsb = seq // tm
    full = lambda a: pl.BlockSpec(a.shape, lambda i: (0,) * a.ndim)
    tab = pl.BlockSpec((tm, LANE), lambda i: (i % nsb, 0))
    return pl.pallas_call(
        _mla_prep_kernel,
        grid=(n // tm,),
        in_specs=[pl.BlockSpec((tm, zc.shape[1]), lambda i: (i, 0)),
                  full(gql), full(gkl), full(wq), full(wk), full(wv), full(gq), full(gk),
                  tab, tab, tab, full(bd)],
        out_specs=[pl.BlockSpec((tm, hp), lambda i: (i, 0)),
                   pl.BlockSpec((tm, hp), lambda i: (i, 0)),
                   pl.BlockSpec((tm, vw), lambda i: (i, 0))],
        out_shape=[jax.ShapeDtypeStruct((n, hp), BF16),
                   jax.ShapeDtypeStruct((n, hp), BF16),
                   jax.ShapeDtypeStruct((n, vw), BF16)],
        compiler_params=_params(("parallel",)),
        name="mla_prep",
    )(zc, gql, gkl, wq, wk, wv, gq, gk, cos, s1, s2, bd)


def _mla_attn_kernel(q_ref, k_ref, v_ref, o_ref, sa_scr, sb_scr, p_scr, m_scr, acc_scr, *, tk, rb):
    tq = q_ref.shape[0]
    nk = k_ref.shape[0] // tk
    hs = [slice(hh * MLA_HEAD_PAD, (hh + 1) * MLA_HEAD_PAD) for hh in range(2)]

    def rows_of(c):
        return pl.ds(pl.multiple_of(c * tk, tk), tk)

    def scores(c, dst):
        for hh in range(2):
            dst[hh] = lax.dot_general(q_ref[:, hs[hh]], k_ref[rows_of(c), hs[hh]], _NT,
                                      preferred_element_type=F32)

    def update(c, src):
        for hh in range(2):
            alphas = []
            for r in range(tq // rb):
                rs = slice(r * rb, (r + 1) * rb)
                s = src[hh, rs, :]
                m_old = m_scr[hh, rs, :]
                m_new = jnp.maximum(m_old, jnp.max(s, axis=-1, keepdims=True))
                p_scr[hh, rs, :] = jnp.exp2(s - m_new).astype(BF16)
                m_scr[hh, rs, :] = m_new
                alphas.append(jnp.exp2(m_old - m_new))
            alpha = jnp.concatenate(alphas, axis=0)
            acc_scr[hh] = alpha * acc_scr[hh] + jnp.dot(p_scr[hh], v_ref[rows_of(c), hs[hh]],
                                                        preferred_element_type=F32)

    m_scr[...] = jnp.full(m_scr.shape, NEG_INF, F32)
    acc_scr[...] = jnp.zeros(acc_scr.shape, F32)
    scores(0, sa_scr)

    def body(i, carry):
        scores(2 * i + 1, sb_scr)
        update(2 * i, sa_scr)
        scores(2 * i + 2, sa_scr)
        update(2 * i + 1, sb_scr)
        return carry

    lax.fori_loop(0, nk // 2 - 1, body, 0)
    scores(nk - 1, sb_scr)
    update(nk - 2, sa_scr)
    update(nk - 1, sb_scr)
    acc0, acc1 = acc_scr[0], acc_scr[1]
    o0 = acc0 / pltpu.roll(acc0, MLA_V, 1)
    o1 = acc1 / pltpu.roll(acc1, MLA_V, 1)
    first = lax.broadcasted_iota(jnp.int32, (tq, LANE), 1) < MLA_V
    o_ref[...] = jnp.where(first, o0, o1).astype(o_ref.dtype)


def _mla_attention(q, k, v, *, batch, seq, tq, tk):
    n = batch * seq
    nqb = seq // tq
    assert (seq // tk) % 2 == 0
    return pl.pallas_call(
        functools.partial(_mla_attn_kernel, tk=tk, rb=MLA_ROW_BLOCK),
        grid=(batch, MLA_HEADS // 2, nqb),
        in_specs=[pl.BlockSpec((tq, 2 * MLA_HEAD_PAD), lambda b, h, i: (b * nqb + i, h)),
                  pl.BlockSpec((seq, 2 * MLA_HEAD_PAD), lambda b, h, i: (b, h)),
                  pl.BlockSpec((seq, 2 * MLA_HEAD_PAD), lambda b, h, i: (b, h))],
        out_specs=pl.BlockSpec((tq, 2 * MLA_V), lambda b, h, i: (b * nqb + i, h)),
        out_shape=jax.ShapeDtypeStruct((n, MLA_HEADS * MLA_V), BF16),
        scratch_shapes=[pltpu.VMEM((2, tq, tk), F32), pltpu.VMEM((2, tq, tk), F32),
                        pltpu.VMEM((2, tq, tk), BF16), pltpu.VMEM((2, tq, 1), F32),
                        pltpu.VMEM((2, tq, LANE), F32)],
        compiler_params=_params(("parallel", "parallel", "arbitrary")),
        name="mla_attn",
    )(q, k, v)


def _merge_kernel(o0_ref, l0_ref, o1_ref, l1_ref, o2_ref, l2_ref, ob_ref, g_ref, x_ref,
                  wa_ref, wb_ref, wo_ref, xo_ref, nat_scr):
    tm = x_ref.shape[0]
    w = A_GROUP_WIDTH

    def natural(ref, slot, dil):
        if dil == 1:
            return ref[...].astype(F32)
        rows = tm // dil
        nc = w // LANE
        for r in range(dil):
            for c in range(nc):
                lo = r * w + c * LANE
                nat_scr[slot * nc + c, pl.ds(r, rows, stride=dil), :] = ref[:, lo:lo + LANE].astype(F32)
        return jnp.concatenate([nat_scr[slot * nc + c] for c in range(nc)], axis=1)

    dils = [dil for _, dil in A_GROUPS]
    l0, l1, l2 = (natural(ref, 2 * g, dils[g]) for g, ref in enumerate((l0_ref, l1_ref, l2_ref)))
    o0, o1, o2 = (natural(ref, 2 * g + 1, dils[g]) for g, ref in enumerate((o0_ref, o1_ref, o2_ref)))
    m = jnp.maximum(jnp.maximum(l0, l1), l2)
    e0, e1, e2 = jnp.exp(l0 - m), jnp.exp(l1 - m), jnp.exp(l2 - m)
    oa = (e0 * o0 + e1 * o1 + e2 * o2) / (e0 + e1 + e2)
    pa = jnp.dot(oa.astype(BF16), wa_ref[...], preferred_element_type=F32)
    pb = jnp.dot(ob_ref[...], wb_ref[...], preferred_element_type=F32)
    d = pa.shape[1]
    merged = g_ref[:, :d].astype(F32) * pa + g_ref[:, d:].astype(F32) * pb
    xo_ref[...] = x_ref[...] + jnp.dot(merged.astype(BF16), wo_ref[...], preferred_element_type=F32)


def _merge(oas, lses, ob, gates, x, wa, wb, wo, *, tm):
    n, d = x.shape
    row = lambda a: pl.BlockSpec((tm * a.shape[0] // n, a.shape[1]), lambda i: (i, 0))
    full = lambda a: pl.BlockSpec(a.shape, lambda i: (0, 0))
    args = [oas[0], lses[0], oas[1], lses[1], oas[2], lses[2], ob, gates, x]
    return pl.pallas_call(
        _merge_kernel,
        grid=(n // tm,),
        in_specs=[row(a) for a in args] + [full(wa), full(wb), full(wo)],
        out_specs=pl.BlockSpec((tm, d), lambda i: (i, 0)),
        out_shape=jax.ShapeDtypeStruct((n, d), F32),
        scratch_shapes=[pltpu.VMEM((2 * len(A_GROUPS) * A_GROUP_WIDTH // LANE, tm, LANE), F32)],
        compiler_params=_params(("parallel",)),
        name="merge",
    )(*args, wa, wb, wo)


def _mem_kv_kernel(mem_ref, g_ref, w_ref, gk_ref, k_ref, v_ref):
    mh = _rms(mem_ref[...], g_ref[...]).astype(BF16)
    kv = jnp.dot(mh, w_ref[...], preferred_element_type=F32)
    for h in range(MEM_HEADS):
        sl = slice(h * MEM_HEAD_DIM, (h + 1) * MEM_HEAD_DIM)
        k_ref[:, sl] = _rms(kv[:, sl], gk_ref[...]).astype(k_ref.dtype)
    v_ref[...] = kv[:, MEM_WIDTH:].astype(v_ref.dtype)


def _mem_kv(mem, g, w, gk, *, tm):
    n = mem.shape[0]
    full = lambda a: pl.BlockSpec(a.shape, lambda i: (0, 0))
    return pl.pallas_call(
        _mem_kv_kernel,
        grid=(n // tm,),
        in_specs=[pl.BlockSpec((tm, mem.shape[1]), lambda i: (i, 0)), full(g), full(w), full(gk)],
        out_specs=[pl.BlockSpec((tm, MEM_WIDTH), lambda i: (i, 0))] * 2,
        out_shape=[jax.ShapeDtypeStruct((n, MEM_WIDTH), BF16)] * 2,
        compiler_params=_params(("parallel",)),
        name="mem_kv",
    )(mem, g, w, gk)


def _mem_attn_kernel(x_ref, k_ref, v_ref, gx_ref, wq_ref, gq_ref, wo_ref, gf_ref, wr_ref,
                     xo_ref, hf_ref, aff_ref):
    x = x_ref[...]
    hx = _rms(x, gx_ref[...]).astype(BF16)
    q = jnp.dot(hx, wq_ref[...], preferred_element_type=F32)
    outs = []
    for h in range(MEM_HEADS):
        sl = slice(h * MEM_HEAD_DIM, (h + 1) * MEM_HEAD_DIM)
        qn = _rms(q[:, sl], gq_ref[...]).astype(BF16)
        s = lax.dot_general(qn, k_ref[:, sl], _NT, preferred_element_type=F32)
        m = jnp.max(s, axis=-1, keepdims=True)
        p = jnp.exp(s - m)
        l = jnp.sum(p, axis=-1, keepdims=True)
        outs.append(jnp.dot(p.astype(BF16), v_ref[:, sl], preferred_element_type=F32) / l)
    o = jnp.concatenate(outs, axis=1).astype(BF16)
    x2 = x + jnp.dot(o, wo_ref[...], preferred_element_type=F32)
    xo_ref[...] = x2
    hf = _rms(x2, gf_ref[...]).astype(BF16)
    hf_ref[...] = hf
    logits = jnp.dot(hf, wr_ref[...], preferred_element_type=F32)
    lane = lax.broadcasted_iota(jnp.int32, logits.shape, 1)
    logits = jnp.where(lane < N_EXPERTS, logits, NEG_INF)
    e = jnp.exp(logits - jnp.max(logits, axis=-1, keepdims=True))
    aff_ref[...] = e / jnp.sum(e, axis=-1, keepdims=True)


def _mem_attention(x, kmem, vmem, gx, wq, gq, wo, gf, wr, *, batch, seq, mem_tokens, tm):
    n, d = x.shape
    nsb = seq // tm
    full = lambda a: pl.BlockSpec(a.shape, lambda b, i: (0, 0))
    row = lambda w: pl.BlockSpec((tm, w), lambda b, i: (b * nsb + i, 0))
    kv = pl.BlockSpec((mem_tokens, MEM_WIDTH), lambda b, i: (b, 0))
    return pl.pallas_call(
        _mem_attn_kernel,
        grid=(batch, nsb),
        in_specs=[row(d), kv, kv, full(gx), full(wq), full(gq), full(wo), full(gf), full(wr)],
        out_specs=[row(d), row(d), row(LANE)],
        out_shape=[jax.ShapeDtypeStruct((n, d), F32),
                   jax.ShapeDtypeStruct((n, d), BF16),
                   jax.ShapeDtypeStruct((n, LANE), F32)],
        compiler_params=_params(("parallel", "parallel")),
        name="mem_attn",
    )(x, kmem, vmem, gx, wq, gq, wo, gf, wr)


def _ffn_kernel(x_ref, wg_ref, wu_ref, wd_ref, g_ref, y_ref):
    x = x_ref[...]
    a = jnp.dot(x, wg_ref[...], preferred_element_type=F32)
    b = jnp.dot(x, wu_ref[...], preferred_element_type=F32)
    hid = (a * jax.nn.sigmoid(a) * b).astype(BF16)
    y = jnp.dot(hid, wd_ref[...], preferred_element_type=F32)
    g = g_ref[...]
    y_ref[...] = (y * jnp.concatenate([g] * (y.shape[1] // LANE), axis=1)).astype(y_ref.dtype)


def _expert_ffn(xe, wg, wu, wd, gates, *, tm):
    e, cap, d = xe.shape
    f = wg.shape[2]
    return pl.pallas_call(
        _ffn_kernel,
        grid=(e, cap // tm),
        in_specs=[pl.BlockSpec((None, tm, d), lambda e, i: (e, i, 0)),
                  pl.BlockSpec((None, d, f), lambda e, i: (e, 0, 0)),
                  pl.BlockSpec((None, d, f), lambda e, i: (e, 0, 0)),
                  pl.BlockSpec((None, f, d), lambda e, i: (e, 0, 0)),
                  pl.BlockSpec((None, tm, LANE), lambda e, i: (e, i, 0))],
        out_specs=pl.BlockSpec((None, tm, d), lambda e, i: (e, i, 0)),
        out_shape=jax.ShapeDtypeStruct((e, cap, d), BF16),
        compiler_params=_params(("parallel", "arbitrary")),
        name="expert_ffn",
    )(xe, wg, wu, wd, gates)


SLOT_WINDOW = 128
ROUTE_TOKENS = 512
COMBINE_ROWS = 256
DISPATCH_STAGES = 4


def _select_kernel(aff_ref, sel_ref, *, cap):
    bits = pltpu.bitcast(aff_ref[...], jnp.int32)
    ne, n = bits.shape
    count = lambda mask: jnp.sum(jnp.where(mask, 1, 0), axis=1, keepdims=True)

    def value_bit(i, t):
        cand = t | jnp.left_shift(1, 30 - i)
        return jnp.where(count(bits >= cand) >= cap, cand, t)

    thr = lax.fori_loop(0, 31, value_bit, jnp.zeros((ne, 1), jnp.int32))
    gt = bits > thr
    eq = bits == thr
    need = cap - count(gt)
    idx = lax.broadcasted_iota(jnp.int32, (ne, n), 1)
    nbits = max(1, (n - 1).bit_length())

    def index_bit(i, j):
        cand = j | jnp.left_shift(1, nbits - 1 - i)
        return jnp.where(count(jnp.logical_and(eq, idx < cand)) < need, cand, j)

    last = lax.fori_loop(0, nbits, index_bit, jnp.zeros((ne, 1), jnp.int32))
    sel = jnp.logical_or(gt, jnp.logical_and(eq, idx <= last))
    sel_ref[...] = jnp.where(sel, 1.0, 0.0).astype(sel_ref.dtype)


def _select(aff_t, *, cap):
    return pl.pallas_call(
        functools.partial(_select_kernel, cap=cap),
        out_shape=jax.ShapeDtypeStruct(aff_t.shape, BF16),
        compiler_params=pltpu.CompilerParams(vmem_limit_bytes=VMEM_LIMIT),
        name="route_select",
    )(aff_t)


def _slots_kernel(sel_ref, upper_ref, lower_ref, pos_ref, base_ref):
    s = sel_ref[...]
    incl = jnp.dot(s, upper_ref[...], preferred_element_type=F32)
    tot = jnp.broadcast_to(incl[:, LANE - 1:LANE], incl.shape).astype(BF16)
    base = jnp.dot(lower_ref[...], tot, preferred_element_type=F32)
    pos_ref[...] = jnp.where(s > 0, base + incl - 1.0, -1.0).astype(jnp.int32)
    base_ref[...] = base.astype(jnp.int32)


def _slots(sel3):
    ne, nt, _ = sel3.shape
    upper = (jnp.arange(LANE)[:, None] <= jnp.arange(LANE)[None, :]).astype(BF16)
    lower = (jnp.arange(nt)[None, :] < jnp.arange(nt)[:, None]).astype(BF16)
    blk = pl.BlockSpec((None, nt, LANE), lambda e: (e, 0, 0))
    return pl.pallas_call(
        _slots_kernel,
        grid=(ne,),
        in_specs=[blk, pl.BlockSpec(upper.shape, lambda e: (0, 0)), pl.BlockSpec(lower.shape, lambda e: (0, 0))],
        out_specs=[blk, blk],
        out_shape=[jax.ShapeDtypeStruct(sel3.shape, jnp.int32)] * 2,
        compiler_params=_params(("parallel",)),
        name="route_slots",
    )(sel3, upper, lower)


def _dispatch_kernel(offs_ref, h_ref, pos_ref, aff_ref, xe_ref, gs_ref,
                     acc_scr, gacc_scr, stage_scr, gstage_scr, cnt_scr, sem, gsem, *, nb):
    b = pl.program_id(0)
    t = h_ref.shape[0]
    w = SLOT_WINDOW
    ns = stage_scr.shape[0]

    @pl.when(b == 0)
    def _():
        acc_scr[...] = jnp.zeros(acc_scr.shape, F32)
        gacc_scr[...] = jnp.zeros(gacc_scr.shape, F32)
        cnt_scr[0] = 0

    hb = h_ref[...]
    a = aff_ref[...]
    a_hi = a.astype(BF16)
    r1 = a - a_hi.astype(F32)
    a_mid = r1.astype(BF16)
    a_lo = (r1 - a_mid.astype(F32)).astype(BF16)
    parts = jnp.concatenate([a_hi, a_mid, a_lo], axis=1)
    slot_iota = lax.broadcasted_iota(jnp.int32, (w, t), 0)

    def xe_copy(slot, e, win):
        return pltpu.make_async_copy(
            stage_scr.at[slot], xe_ref.at[e, pl.ds(pl.multiple_of(win * w, w), w), :], sem.at[slot])

    def gs_copy(slot, e, win):
        return pltpu.make_async_copy(
            gstage_scr.at[slot], gs_ref.at[e, pl.ds(pl.multiple_of(win * w, w), w), :], gsem.at[slot])

    for e in range(N_EXPERTS):
        o0 = offs_ref[e * (nb + 1) + b]
        o1 = offs_ref[e * (nb + 1) + b + 1]
        prow = pos_ref[e:e + 1, :]

        def visit(win, carry, e=e, o1=o1, prow=prow):
            place = jnp.where(slot_iota == prow - win * w, 1.0, 0.0).astype(BF16)
            acc_scr[e] += jnp.dot(place, hb, preferred_element_type=F32)
            gacc_scr[e] += jnp.dot(place, parts, preferred_element_type=F32)

            @pl.when(o1 >= (win + 1) * w)
            def _():
                c = cnt_scr[0]
                slot = c % ns

                @pl.when(c >= ns)
                def _():
                    xe_copy(slot, e, win).wait()
                    gs_copy(slot, e, win).wait()

                stage_scr[slot] = acc_scr[e].astype(BF16)
                g3 = gacc_scr[e]
                g = (g3[:, e:e + 1] + g3[:, LANE + e:LANE + e + 1]) + g3[:, 2 * LANE + e:2 * LANE + e + 1]
                gstage_scr[slot] = jnp.broadcast_to(g, (w, LANE))
                acc_scr[e] = jnp.zeros(acc_scr.shape[1:], F32)
                gacc_scr[e] = jnp.zeros(gacc_scr.shape[1:], F32)
                xe_copy(slot, e, win).start()
                gs_copy(slot, e, win).start()
                cnt_scr[0] = c + 1

            return carry

        lax.fori_loop(o0 // w, (o1 + w - 1) // w, visit, 0)

    @pl.when(b == nb - 1)
    def _():
        c = cnt_scr[0]
        for s in range(ns):
            @pl.when(c > s)
            def _(s=s):
                xe_copy(s, 0, 0).wait()
                gs_copy(s, 0, 0).wait()


def _dispatch(offs, h, pos, aff, *, cap):
    n, d = h.shape
    t = min(ROUTE_TOKENS, n)
    nb = n // t
    ne = N_EXPERTS
    w = SLOT_WINDOW
    grid_spec = pltpu.PrefetchScalarGridSpec(
        num_scalar_prefetch=1,
        grid=(nb,),
        in_specs=[pl.BlockSpec((t, d), lambda b, offs: (b, 0)),
                  pl.BlockSpec((ne, t), lambda b, offs: (0, b)),
                  pl.BlockSpec((t, LANE), lambda b, offs: (b, 0))],
        out_specs=[pl.BlockSpec(memory_space=pl.ANY), pl.BlockSpec(memory_space=pl.ANY)],
        scratch_shapes=[pltpu.VMEM((ne, w, d), F32), pltpu.VMEM((ne, w, 3 * LANE), F32),
                        pltpu.VMEM((DISPATCH_STAGES, w, d), BF16), pltpu.VMEM((DISPATCH_STAGES, w, LANE), F32),
                        pltpu.SMEM((1,), jnp.int32),
                        pltpu.SemaphoreType.DMA((DISPATCH_STAGES,)), pltpu.SemaphoreType.DMA((DISPATCH_STAGES,))])
    return pl.pallas_call(
        functools.partial(_dispatch_kernel, nb=nb),
        grid_spec=grid_spec,
        out_shape=[jax.ShapeDtypeStruct((ne, cap, d), BF16), jax.ShapeDtypeStruct((ne, cap, LANE), F32)],
        compiler_params=_params(("arbitrary",)),
        name="route_dispatch",
    )(offs, h, pos, aff)


def _combine_kernel(offs_ref, x_ref, post_ref, ye_ref, o_ref, buf_scr, sem, *, nb, cap):
    b = pl.program_id(0)
    t = x_ref.shape[0]
    w = SLOT_WINDOW
    rows = buf_scr.shape[1]

    def first_row(e):
        return jnp.minimum((offs_ref[e * (nb + 1) + b] // w) * w, cap - rows)

    def fetch(e, start):
        return pltpu.make_async_copy(
            ye_ref.at[e, pl.ds(pl.multiple_of(start, w), rows), :], buf_scr.at[e], sem.at[e])

    for e in range(N_EXPERTS):
        fetch(e, first_row(e)).start()
    o_ref[...] = x_ref[...]
    lane_iota = lax.broadcasted_iota(jnp.int32, (t, rows), 1)
    for e in range(N_EXPERTS):
        start = first_row(e)
        o1 = offs_ref[e * (nb + 1) + b + 1]
        pcol = post_ref[:, e:e + 1]
        fetch(e, start).wait()
        back = jnp.where(lane_iota == pcol - start, 1.0, 0.0).astype(BF16)
        o_ref[...] += jnp.dot(back, buf_scr[e], preferred_element_type=F32)

        def more(j, carry, e=e, start=start, pcol=pcol):
            lo = start + j * rows
            ws = jnp.minimum(lo, cap - rows)
            cp = fetch(e, ws)
            cp.start()
            cp.wait()
            hit = jnp.logical_and(lane_iota == pcol - ws, pcol >= lo)
            o_ref[...] += jnp.dot(jnp.where(hit, 1.0, 0.0).astype(BF16), buf_scr[e],
                                  preferred_element_type=F32)
            return carry

        lax.fori_loop(1, (jnp.maximum(o1 - start, 1) + rows - 1) // rows, more, 0)


def _combine(offs, x, pos_t, ye):
    n, d = x.shape
    ne, cap, _ = ye.shape
    t = min(ROUTE_TOKENS, n)
    nb = n // t
    rows = min(COMBINE_ROWS, cap)
    grid_spec = pltpu.PrefetchScalarGridSpec(
        num_scalar_prefetch=1,
        grid=(nb,),
        in_specs=[pl.BlockSpec((t, d), lambda b, offs: (b, 0)),
                  pl.BlockSpec((t, LANE), lambda b, offs: (b, 0)),
                  pl.BlockSpec(memory_space=pl.ANY)],
        out_specs=pl.BlockSpec((t, d), lambda b, offs: (b, 0)),
        scratch_shapes=[pltpu.VMEM((ne, rows, d), BF16), pltpu.SemaphoreType.DMA((ne,))])
    return pl.pallas_call(
        functools.partial(_combine_kernel, nb=nb, cap=cap),
        grid_spec=grid_spec,
        out_shape=jax.ShapeDtypeStruct((n, d), F32),
        compiler_params=_params(("arbitrary",)),
        name="route_combine",
    )(offs, x, pos_t, ye)


def _moe(x, hf, aff, lp):
    n, d = x.shape
    ne = N_EXPERTS
    cap = EC_CAPACITY_FACTOR * n // ne
    t = min(ROUTE_TOKENS, n)
    nb = n // t
    sel = _select(aff[:, :ne].T, cap=cap)
    pos3, base3 = _slots(sel.reshape(ne, n // LANE, LANE))
    pos = pos3.reshape(ne, n)
    offs = jnp.concatenate([base3[:, ::t // LANE, 0], jnp.full((ne, 1), cap, jnp.int32)], axis=1)
    offs = offs.reshape(-1)
    pos_t = jnp.pad(pos.T, ((0, 0), (0, LANE - ne)), constant_values=-1)
    xe, gs = _dispatch(offs, hf, pos, aff, cap=cap)
    ye = _expert_ffn(xe, lp["w_gate"], lp["w_up"], lp["w_down"], gs, tm=min(512, cap))
    return _combine(offs, x, pos_t, ye)


def _t5_bucket(rel):
    half = REL_BUCKETS // 2
    max_exact = half // 2
    n = jnp.abs(rel)
    base = jnp.where(rel > 0, half, 0)
    nf = jnp.maximum(n, 1).astype(F32)
    large = max_exact + (jnp.log(nf / max_exact) / math.log(REL_MAX_DIST / max_exact)
                         * (half - max_exact)).astype(jnp.int32)
    large = jnp.minimum(large, half - 1)
    return base + jnp.where(n < max_exact, n, large)


def _band_bias(rel_bias, group, dilation, tq):
    tk = tq + 2 * A_HALF
    rel = jnp.arange(tk)[None, :] - A_HALF - jnp.arange(tq)[:, None]
    heads = slice(group * A_HEADS_PER_GROUP, (group + 1) * A_HEADS_PER_GROUP)
    b = jnp.transpose(rel_bias[:, heads][_t5_bucket(rel * dilation)], (2, 0, 1)).astype(F32)
    return jnp.where((jnp.abs(rel) <= A_HALF)[None], b, NEG_INF)


def _block_diag_ones(size, block):
    idx = jnp.arange(size) // block
    return (idx[:, None] == idx[None, :]).astype(BF16)


def _rope_tables(seq):
    half = MLA_ROPE // 2
    freqs = ROPE_THETA ** (-jnp.arange(half, dtype=F32) / half)
    ang = jnp.arange(seq).astype(F32)[:, None] * freqs[None, :]
    cos, sin = jnp.cos(ang), jnp.sin(ang)
    ones = jnp.ones((seq, MLA_NOPE), F32)
    z64 = jnp.zeros((seq, MLA_NOPE), F32)
    z16 = jnp.zeros((seq, half), F32)
    z32 = jnp.zeros((seq, LANE - MLA_QK_DIM), F32)
    c = jnp.concatenate([ones, cos, cos, z32], axis=1)
    s1 = jnp.concatenate([z64, -sin, z16, z32], axis=1)
    s2 = jnp.concatenate([z64, z16, sin, z32], axis=1)
    return c, s1, s2


def _pad_heads(w, heads, width, lo, hi):
    w = w.reshape(w.shape[0], heads, width)[:, :, lo:hi]
    w = jnp.pad(w, ((0, 0), (0, 0), (0, MLA_HEAD_PAD - (hi - lo))))
    return w.reshape(w.shape[0], heads * MLA_HEAD_PAD)


def _prep_layer(p, l):
    d = D_MODEL
    row = lambda v: v.reshape(1, -1).astype(F32)
    w_in = p["w_in"][l]
    out = {}
    out["g_mix"] = row(p["norm_mix"][l])
    ng = len(A_GROUPS)
    out["w_a"] = w_in[:, :COL_CQ].reshape(d, 3, ng, A_GROUP_WIDTH).transpose(0, 2, 1, 3).reshape(
        d, COL_CQ).astype(BF16)
    out["aux_a"] = row(jnp.tile(jnp.concatenate([
        jnp.tile(p["a_q_norm"][l], A_HEADS_PER_GROUP) * (A_HEAD_DIM ** -0.5),
        jnp.tile(p["a_k_norm"][l], A_HEADS_PER_GROUP),
        jnp.ones((A_GROUP_WIDTH,), F32)]), ng))
    out["w_c"] = jnp.pad(w_in[:, COL_CQ:COL_GA], ((0, 0), (0, d - (COL_GA - COL_CQ)))).astype(BF16)
    out["w_g"] = w_in[:, COL_GA:].astype(BF16)
    out["g_ql"] = row(p["mla_q_lat_norm"][l])
    out["g_kl"] = row(p["mla_kv_lat_norm"][l])
    out["w_uq"] = _pad_heads(p["w_uq"][l], MLA_HEADS, MLA_QK_DIM, 0, MLA_QK_DIM).astype(BF16)
    w_ukv = p["w_ukv"][l]
    k_nope = _pad_heads(w_ukv, MLA_HEADS, MLA_NOPE + MLA_V, 0, MLA_NOPE)
    eye = jnp.pad(jnp.eye(MLA_ROPE, dtype=F32), ((0, 0), (MLA_NOPE, MLA_HEAD_PAD - MLA_QK_DIM)))
    k_pe = jnp.tile(eye, (1, MLA_HEADS))
    k_rows = d - MLA_Q_RANK - MLA_KV_RANK - MLA_ROPE
    out["w_uk"] = jnp.concatenate(
        [k_nope, k_pe, jnp.zeros((k_rows, MLA_HEADS * MLA_HEAD_PAD), F32)], axis=0).astype(BF16)
    out["w_uv"] = w_ukv.reshape(MLA_KV_RANK, MLA_HEADS, MLA_NOPE + MLA_V)[:, :, MLA_NOPE:].reshape(
        MLA_KV_RANK, MLA_HEADS * MLA_V).astype(BF16)
    pad_gain = lambda g: jnp.tile(jnp.pad(g, (0, MLA_HEAD_PAD - MLA_QK_DIM)), MLA_HEADS)
    out["g_q"] = row(pad_gain(p["mla_q_norm"][l]) * (MLA_QK_DIM ** -0.5 * math.log2(math.e)))
    out["g_k"] = row(pad_gain(p["mla_k_norm"][l]))
    out["w_pa"] = p["w_proj_a"][l].astype(BF16)
    out["w_pb"] = p["w_proj_b"][l].astype(BF16)
    out["w_out"] = p["w_out"][l].astype(BF16)
    out["g_mx"] = row(p["norm_mem_x"][l])
    out["g_mkv"] = row(p["norm_mem_kv"][l])
    out["w_mq"] = p["w_mq"][l].astype(BF16)
    out["w_mkv"] = p["w_mkv"][l].astype(BF16)
    out["g_mq"] = row(p["mem_q_norm"][l] * (MEM_HEAD_DIM ** -0.5))
    out["g_mk"] = row(p["mem_k_norm"][l])
    out["w_mo"] = p["w_mo"][l].astype(BF16)
    out["g_ffn"] = row(p["norm_ffn"][l])
    out["w_router"] = jnp.pad(p["w_router"][l], ((0, 0), (0, LANE - N_EXPERTS))).astype(BF16)
    out["w_gate"] = p["w_gate"][l].astype(BF16)
    out["w_up"] = p["w_up"][l].astype(BF16)
    out["w_down"] = p["w_down"][l].astype(BF16)
    return out


BAND_TQ = 128
MLA_ROW_BLOCK = 32


def _layer(x, mem, lp, shared, *, batch, seq):
    n = batch * seq
    mem_tokens = mem.shape[0] // batch
    tm = min(1024, n)
    bd64, bd128 = shared["bd64"], shared["bd128"]
    zgs = _norm_proj_heads(x, lp["g_mix"], lp["w_a"], lp["aux_a"], bd64, tm=min(512, n))
    zc = _norm_proj(x, lp["g_mix"], lp["w_c"], mode="plain", tm=tm, tn=D_MODEL, out_dtype=F32)
    gates = _norm_proj(x, lp["g_mix"], lp["w_g"], mode="sigmoid", tm=tm, tn=D_MODEL, out_dtype=BF16)
    oas, lses = [], []
    for g, (_, dil) in enumerate(A_GROUPS):
        o, lse = _band_attention(zgs[g], shared["band_bias"][g], shared["hmask"], batch=batch, seq=seq,
                                 group=g, dilation=dil, tq=BAND_TQ)
        oas.append(o)
        lses.append(lse)
    qm, km, vm = _mla_prep(zc, lp["g_ql"], lp["g_kl"], lp["w_uq"], lp["w_uk"], lp["w_uv"],
                           lp["g_q"], lp["g_k"], shared["rope_c"], shared["rope_s1"],
                           shared["rope_s2"], bd128, seq=seq, tm=min(512, seq))
    ob = _mla_attention(qm, km, vm, batch=batch, seq=seq, tq=min(512, seq), tk=min(512, seq))
    x = _merge(oas, lses, ob, gates, x, lp["w_pa"], lp["w_pb"], lp["w_out"], tm=min(512, n))
    kmem, vmem = _mem_kv(mem, lp["g_mkv"], lp["w_mkv"], lp["g_mk"], tm=min(512, mem.shape[0]))
    x, hf, aff = _mem_attention(x, kmem, vmem, lp["g_mx"], lp["w_mq"], lp["g_mq"], lp["w_mo"],
                                lp["g_ffn"], lp["w_router"], batch=batch, seq=seq,
                                mem_tokens=mem_tokens, tm=min(512, seq))
    return _moe(x, hf, aff, lp)


def _trunk(x, mem, layers, shared):
    batch, seq, d = x.shape
    xf = x.reshape(batch * seq, d)
    memf = mem.reshape(-1, d)
    for lp in layers:
        xf = _layer(xf, memf, lp, shared, batch=batch, seq=seq)
    return xf.reshape(batch, seq, d)


def kernel(x_prompt, x_sample, mem_prompt, mem_sample, norm_mix, w_in, a_q_norm, a_k_norm, rel_bias,
           mla_q_lat_norm, w_uq, mla_kv_lat_norm, w_ukv, mla_q_norm, mla_k_norm, w_proj_a, w_proj_b,
           w_out, norm_mem_x, norm_mem_kv, w_mq, w_mkv, mem_q_norm, mem_k_norm, w_mo, norm_ffn,
           w_router, w_gate, w_up, w_down):
    p = dict(norm_mix=norm_mix, w_in=w_in, a_q_norm=a_q_norm, a_k_norm=a_k_norm,
             mla_q_lat_norm=mla_q_lat_norm, w_uq=w_uq, mla_kv_lat_norm=mla_kv_lat_norm, w_ukv=w_ukv,
             mla_q_norm=mla_q_norm, mla_k_norm=mla_k_norm, w_proj_a=w_proj_a, w_proj_b=w_proj_b,
             w_out=w_out, norm_mem_x=norm_mem_x, norm_mem_kv=norm_mem_kv, w_mq=w_mq, w_mkv=w_mkv,
             mem_q_norm=mem_q_norm, mem_k_norm=mem_k_norm, w_mo=w_mo, norm_ffn=norm_ffn,
             w_router=w_router, w_gate=w_gate, w_up=w_up, w_down=w_down)
    layers = [_prep_layer(p, l) for l in range(DEPTH)]
    outs = []
    for x, mem in ((x_prompt, mem_prompt), (x_sample, mem_sample)):
        seq = x.shape[1]
        c, s1, s2 = _rope_tables(seq)
        hm = (jnp.arange(LANE)[None, :] // A_HEAD_DIM == jnp.arange(16)[:, None]).astype(BF16)
        shared = dict(
            bd64=_block_diag_ones(256, A_HEAD_DIM), bd128=_block_diag_ones(256, MLA_HEAD_PAD),
            band_bias=[_band_bias(rel_bias, g, dil, BAND_TQ) for g, (_, dil) in enumerate(A_GROUPS)],
            hmask=hm, rope_c=c, rope_s1=s1, rope_s2=s2)
        outs.append(_trunk(x, mem, layers, shared))
    return tuple(outs)
```

```python
import functools
import math

import jax
import jax.numpy as jnp
from jax import lax
from jax.experimental import pallas as pl
from jax.experimental.pallas import tpu as pltpu

F32 = jnp.float32
BF16 = jnp.bfloat16

D_MODEL = 1024
DEPTH = 2
A_GROUPS = ((128, 1), (512, 4), (2048, 16))
A_HEADS_PER_GROUP = 8
A_HEAD_DIM = 64
A_GROUP_WIDTH = A_HEADS_PER_GROUP * A_HEAD_DIM
A_WIDTH = A_GROUP_WIDTH * len(A_GROUPS)
A_HALF = 64
MLA_HEADS = 8
MLA_NOPE = 64
MLA_ROPE = 32
MLA_QK_DIM = MLA_NOPE + MLA_ROPE
MLA_V = 64
MLA_Q_RANK = 512
MLA_KV_RANK = 256
MLA_HEAD_PAD = 128
ROPE_THETA = 10000.0
MEM_HEADS = 4
MEM_HEAD_DIM = 128
MEM_WIDTH = MEM_HEADS * MEM_HEAD_DIM
REL_BUCKETS = 32
REL_MAX_DIST = 1024
N_EXPERTS = 16
EC_CAPACITY_FACTOR = 2
RMS_EPS = 1e-6
NEG_INF = -1e30
COL_CQ = 3 * A_WIDTH
COL_CKV = COL_CQ + MLA_Q_RANK
COL_GA = COL_CKV + MLA_KV_RANK + MLA_ROPE
IN_COLS = COL_GA + 2 * D_MODEL

LANE = 128
VMEM_LIMIT = 48 * 1024 * 1024

_NT = (((1,), (1,)), ((), ()))


def _rms(x, g):
    ms = jnp.mean(x * x, axis=-1, keepdims=True)
    return x * lax.rsqrt(ms + RMS_EPS) * g


def _params(sem):
    return pltpu.CompilerParams(dimension_semantics=sem, vmem_limit_bytes=VMEM_LIMIT)


def _proj_kernel(x_ref, g_ref, w_ref, o_ref, h_scr, *, mode):
    j = pl.program_id(1)

    @pl.when(j == 0)
    def _():
        h_scr[...] = _rms(x_ref[...], g_ref[...]).astype(BF16)

    z = jnp.dot(h_scr[...], w_ref[...], preferred_element_type=F32)
    if mode == "sigmoid":
        z = jax.nn.sigmoid(z)
    o_ref[...] = z.astype(o_ref.dtype)


def _proj_heads_kernel(x_ref, g_ref, w_ref, aux_ref, bd_ref, o0_ref, o1_ref, o2_ref, hf_scr, hp_scr):
    j = pl.program_id(1)
    tm = x_ref.shape[0]
    w = 3 * A_GROUP_WIDTH

    @pl.when(j == 0)
    def _():
        h = _rms(x_ref[...], g_ref[...])
        for c in range(h.shape[1] // LANE):
            hf_scr[c] = h[:, c * LANE:(c + 1) * LANE]
        for g, (_, dil) in enumerate(A_GROUPS):
            if dil == 1:
                hp_scr[g] = h.astype(BF16)
                continue
            rows = tm // dil
            for r in range(dil):
                for c in range(h.shape[1] // LANE):
                    hp_scr[g, r * rows:(r + 1) * rows, c * LANE:(c + 1) * LANE] = (
                        hf_scr[c, pl.ds(r, rows, stride=dil), :].astype(BF16))

    z = jnp.dot(hp_scr[j], w_ref[...], preferred_element_type=F32)
    cols = []
    for c in range(2 * A_GROUP_WIDTH // 256):
        sl = slice(c * 256, (c + 1) * 256)
        zc = z[:, sl]
        ss = jnp.dot((zc * zc).astype(BF16), bd_ref[...], preferred_element_type=F32)
        cols.append((zc * lax.rsqrt(ss * (1.0 / A_HEAD_DIM) + RMS_EPS) * aux_ref[:, sl]).astype(BF16))
    cols.append(z[:, 2 * A_GROUP_WIDTH:].astype(BF16))
    zn = jnp.concatenate(cols, axis=1)
    for g, ((_, dil), o_ref) in enumerate(zip(A_GROUPS, (o0_ref, o1_ref, o2_ref))):
        @pl.when(j == g)
        def _(dil=dil, o_ref=o_ref):
            rows = tm // dil
            for r in range(dil):
                o_ref[:, r * w:(r + 1) * w] = zn[r * rows:(r + 1) * rows, :]


def _norm_proj_heads(x, g, w, aux, bd, *, tm):
    n, d = x.shape
    wg = 3 * A_GROUP_WIDTH
    ng = len(A_GROUPS)
    return pl.pallas_call(
        _proj_heads_kernel,
        grid=(n // tm, ng),
        in_specs=[
            pl.BlockSpec((tm, d), lambda i, j: (i, 0)),
            pl.BlockSpec((1, d), lambda i, j: (0, 0)),
            pl.BlockSpec((d, wg), lambda i, j: (0, j)),
            pl.BlockSpec((1, wg), lambda i, j: (0, j)),
            pl.BlockSpec(bd.shape, lambda i, j: (0, 0)),
        ],
        out_specs=[pl.BlockSpec((tm // dil, dil * wg), lambda i, j: (i, 0)) for _, dil in A_GROUPS],
        out_shape=[jax.ShapeDtypeStruct((n // dil, dil * wg), BF16) for _, dil in A_GROUPS],
        scratch_shapes=[pltpu.VMEM((d // LANE, tm, LANE), F32), pltpu.VMEM((ng, tm, d), BF16)],
        compiler_params=_params(("parallel", "arbitrary")),
        name="norm_proj_heads",
    )(x, g, w, aux, bd)


def _norm_proj(x, g, w, *, mode, tm, tn, out_dtype):
    n, d = x.shape
    ncols = w.shape[1]
    return pl.pallas_call(
        functools.partial(_proj_kernel, mode=mode),
        grid=(n // tm, ncols // tn),
        in_specs=[
            pl.BlockSpec((tm, d), lambda i, j: (i, 0)),
            pl.BlockSpec((1, d), lambda i, j: (0, 0)),
            pl.BlockSpec((d, tn), lambda i, j: (0, j)),
        ],
        out_specs=pl.BlockSpec((tm, tn), lambda i, j: (i, j)),
        out_shape=jax.ShapeDtypeStruct((n, ncols), out_dtype),
        scratch_shapes=[pltpu.VMEM((tm, d), BF16)],
        compiler_params=_params(("parallel", "arbitrary")),
        name="norm_proj_" + mode,
    )(x, g, w)


def _band_attn_kernel(q_ref, kl_ref, km_ref, kh_ref, vl_ref, vm_ref, vh_ref, bias_ref, hm_ref,
                      o_ref, lse_ref, *, tq, seq_len):
    i = pl.program_id(2)
    tk = tq + 2 * A_HALF
    nsub = q_ref.shape[0] // tq
    k_all = jnp.concatenate([kl_ref[...], km_ref[...], kh_ref[...]], axis=0)
    v_all = jnp.concatenate([vl_ref[...], vm_ref[...], vh_ref[...]], axis=0)
    first = lax.broadcasted_iota(jnp.int32, (tq, LANE), 1) < A_HEAD_DIM
    for sub in range(nsub):
        q = q_ref[sub * tq:(sub + 1) * tq, :]
        k = k_all[sub * tq:sub * tq + tk, :]
        v = v_all[sub * tq:sub * tq + tk, :]
        kpos = (i * nsub + sub) * tq - A_HALF + lax.broadcasted_iota(jnp.int32, (1, tk), 1)
        valid = jnp.logical_and(kpos >= 0, kpos < seq_len)
        o_parts, lse_parts = [], []
        for j in range(A_GROUP_WIDTH // LANE):
            sl = slice(j * LANE, (j + 1) * LANE)
            qp, kp, vp = q[:, sl], k[:, sl], v[:, sl]
            res = []
            for hh in range(2):
                qm = qp * hm_ref[hh:hh + 1, :]
                s = lax.dot_general(qm, kp, _NT, preferred_element_type=F32)
                s = jnp.where(valid, s + bias_ref[2 * j + hh], NEG_INF)
                m = jnp.max(s, axis=-1, keepdims=True)
                p = jnp.exp(s - m)
                l = jnp.sum(p, axis=-1, keepdims=True)
                o = jnp.dot(p.astype(BF16), vp, preferred_element_type=F32) / l
                res.append((o, m + jnp.log(l)))
            o_parts.append(jnp.where(first, res[0][0], res[1][0]))
            lse_parts.append(jnp.where(first, res[0][1], res[1][1]))
        o_ref[sub * tq:(sub + 1) * tq, :] = jnp.concatenate(o_parts, axis=1).astype(o_ref.dtype)
        lse_ref[sub * tq:(sub + 1) * tq, :] = jnp.concatenate(lse_parts, axis=1)


def _band_attention(zg, bias, hmask, *, batch, seq, group, dilation, tq):
    sl = seq // dilation
    zv = zg.reshape(batch, sl, dilation * 3 * A_GROUP_WIDTH)
    step = min(BAND_STEP_ROWS, sl)
    hb = step // A_HALF
    nhb = sl // A_HALF
    w = A_GROUP_WIDTH

    def main(off):
        return pl.BlockSpec((None, step, w), lambda b, r, i: (b, i, r * 3 + off))

    def lo(off):
        return pl.BlockSpec((None, A_HALF, w),
                            lambda b, r, i: (b, jnp.maximum(i * hb - 1, 0), r * 3 + off))

    def hi(off):
        return pl.BlockSpec((None, A_HALF, w),
                            lambda b, r, i: (b, jnp.minimum((i + 1) * hb, nhb - 1), r * 3 + off))

    out_spec = pl.BlockSpec((None, step, w), lambda b, r, i: (b, i, r))
    o, lse = pl.pallas_call(
        functools.partial(_band_attn_kernel, tq=tq, seq_len=sl),
        grid=(batch, dilation, sl // step),
        in_specs=[main(0), lo(1), main(1), hi(1), lo(2), main(2), hi(2),
                  pl.BlockSpec(bias.shape, lambda b, r, i: (0, 0, 0)),
                  pl.BlockSpec(hmask.shape, lambda b, r, i: (0, 0))],
        out_specs=[out_spec, out_spec],
        out_shape=[jax.ShapeDtypeStruct((batch, sl, dilation * w), BF16),
                   jax.ShapeDtypeStruct((batch, sl, dilation * w), F32)],
        compiler_params=_params(("parallel", "parallel", "parallel")),
        name=f"band_attn_g{group}",
    )(zv, zv, zv, zv, zv, zv, zv, bias, hmask)
    return o.reshape(batch * sl, dilation * w), lse.reshape(batch * sl, dilation * w)


def _mla_prep_kernel(zc_ref, gql_ref, gkl_ref, wq_ref, wk_ref, wv_ref, gq_ref, gk_ref,
                     c_ref, s1_ref, s2_ref, bd_ref, q_ref, k_ref, v_ref):
    zc = zc_ref[...]
    cq = _rms(zc[:, :MLA_Q_RANK], gql_ref[...]).astype(BF16)
    ckv = _rms(zc[:, MLA_Q_RANK:MLA_Q_RANK + MLA_KV_RANK], gkl_ref[...])
    kin = jnp.concatenate([ckv, zc[:, MLA_Q_RANK + MLA_KV_RANK:]], axis=1).astype(BF16)
    q = jnp.dot(cq, wq_ref[...], preferred_element_type=F32)
    k = jnp.dot(kin, wk_ref[...], preferred_element_type=F32)
    v = jnp.dot(ckv.astype(BF16), wv_ref[...], preferred_element_type=F32)
    first = lax.broadcasted_iota(jnp.int32, (v.shape[0], LANE), 1) < MLA_V
    for j in range(MLA_HEADS // 2):
        vp = v[:, j * LANE:(j + 1) * LANE]
        v_ref[:, 2 * j * LANE:(2 * j + 1) * LANE] = jnp.where(first, vp, 1.0).astype(v_ref.dtype)
        v_ref[:, (2 * j + 1) * LANE:(2 * j + 2) * LANE] = jnp.where(first, 1.0, vp).astype(v_ref.dtype)
    cos, s1, s2 = c_ref[...], s1_ref[...], s2_ref[...]
    for src, g_ref, dst in ((q, gq_ref, q_ref), (k, gk_ref, k_ref)):
        for c in range(src.shape[1] // 256):
            sl = slice(c * 256, (c + 1) * 256)
            xc = src[:, sl]
            ss = jnp.dot((xc * xc).astype(BF16), bd_ref[...], preferred_element_type=F32)
            xn = xc * lax.rsqrt(ss * (1.0 / MLA_QK_DIM) + RMS_EPS) * g_ref[:, sl]
            for hh in range(2):
                xh = xn[:, hh * LANE:(hh + 1) * LANE]
                out = (xh * cos + pltpu.roll(xh, LANE - MLA_ROPE // 2, 1) * s1
                       + pltpu.roll(xh, MLA_ROPE // 2, 1) * s2)
                lo = c * 256 + hh * LANE
                dst[:, lo:lo + LANE] = out.astype(dst.dtype)


def _mla_prep(zc, gql, gkl, wq, wk, wv, gq, gk, cos, s1, s2, bd, *, seq, tm):
    n = zc.shape[0]
    hp = MLA_HEADS * MLA_HEAD_PAD
    vw = MLA_HEADS * MLA_HEAD_PAD
    nsb = seq // tm
    full = lambda a: pl.BlockSpec(a.shape, lambda i: (0,) * a.ndim)
    tab = pl.BlockSpec((tm, LANE), lambda i: (i % nsb, 0))
    return pl.pallas_call(
        _mla_prep_kernel,
        grid=(n // tm,),
        in_specs=[pl.BlockSpec((tm, zc.shape[1]), lambda i: (i, 0)),
                  full(gql), full(gkl), full(wq), full(wk), full(wv), full(gq), full(gk),
                  tab, tab, tab, full(bd)],
        out_specs=[pl.BlockSpec((tm, hp), lambda i: (i, 0)),
                   pl.BlockSpec((tm, hp), lambda i: (i, 0)),
                   pl.BlockSpec((tm, vw), lambda i: (i, 0))],
        out_shape=[jax.ShapeDtypeStruct((n, hp), BF16),
                   jax.ShapeDtypeStruct((n, hp), BF16),
                   jax.ShapeDtypeStruct((n, vw), BF16)],
        compiler_params=_params(("parallel",)),
        name="mla_prep",
    )(zc, gql, gkl, wq, wk, wv, gq, gk, cos, s1, s2, bd)


def _mla_attn_kernel(q_ref, k_ref, v_ref, o_ref, sa_scr, sb_scr, p_scr, m_scr, acc_scr, *, tk, rb):
    tq = q_ref.shape[0]
    nk = k_ref.shape[0] // tk
    hs = [slice(hh * MLA_HEAD_PAD, (hh + 1) * MLA_HEAD_PAD) for hh in range(2)]

    def rows_of(c):
        return pl.ds(pl.multiple_of(c * tk, tk), tk)

    def scores(c, dst):
        for hh in range(2):
            dst[hh] = lax.dot_general(q_ref[:, hs[hh]], k_ref[rows_of(c), hs[hh]], _NT,
                                      preferred_element_type=F32)

    def update(c, src):
        for hh in range(2):
            alphas = []
            for r in range(tq // rb):
                rs = slice(r * rb, (r + 1) * rb)
                m_old = m_scr[hh, rs, :]
                tiles = [src[hh, rs, lt * LANE:(lt + 1) * LANE] for lt in range(tk // LANE)]
                tmax = functools.reduce(jnp.maximum, tiles)
                m_new = jnp.maximum(m_old, jnp.broadcast_to(jnp.max(tmax, axis=-1, keepdims=True),
                                                            m_old.shape))
                for lt, st in enumerate(tiles):
                    p_scr[hh, rs, lt * LANE:(lt + 1) * LANE] = jnp.exp2(st - m_new).astype(BF16)
                m_scr[hh, rs, :] = m_new
                alphas.append(jnp.exp2(m_old - m_new))
            alpha = jnp.concatenate(alphas, axis=0)
            acc_scr[hh] = alpha * acc_scr[hh] + jnp.dot(p_scr[hh], v_ref[rows_of(c), hs[hh]],
                                                        preferred_element_type=F32)

    m_scr[...] = jnp.full(m_scr.shape, NEG_INF, F32)
    acc_scr[...] = jnp.zeros(acc_scr.shape, F32)
    scores(0, sa_scr)

    def body(i, carry):
        scores(2 * i + 1, sb_scr)
        update(2 * i, sa_scr)
        scores(2 * i + 2, sa_scr)
        update(2 * i + 1, sb_scr)
        return carry

    lax.fori_loop(0, nk // 2 - 1, body, 0)
    scores(nk - 1, sb_scr)
    update(nk - 2, sa_scr)
    update(nk - 1, sb_scr)
    acc0, acc1 = acc_scr[0], acc_scr[1]
    o0 = acc0 / pltpu.roll(acc0, MLA_V, 1)
    o1 = acc1 / pltpu.roll(acc1, MLA_V, 1)
    first = lax.broadcasted_iota(jnp.int32, (tq, LANE), 1) < MLA_V
    o_ref[...] = jnp.where(first, o0, o1).astype(o_ref.dtype)


def _mla_attention(q, k, v, *, batch, seq, tq, tk):
    n = batch * seq
    nqb = seq // tq
    assert (seq // tk) % 2 == 0
    return pl.pallas_call(
        functools.partial(_mla_attn_kernel, tk=tk, rb=MLA_ROW_BLOCK),
        grid=(batch, MLA_HEADS // 2, nqb),
        in_specs=[pl.BlockSpec((tq, 2 * MLA_HEAD_PAD), lambda b, h, i: (b * nqb + i, h)),
                  pl.BlockSpec((seq, 2 * MLA_HEAD_PAD), lambda b, h, i: (b, h)),
                  pl.BlockSpec((seq, 2 * MLA_HEAD_PAD), lambda b, h, i: (b, h))],
        out_specs=pl.BlockSpec((tq, 2 * MLA_V), lambda b, h, i: (b * nqb + i, h)),
        out_shape=jax.ShapeDtypeStruct((n, MLA_HEADS * MLA_V), BF16),
        scratch_shapes=[pltpu.VMEM((2, tq, tk), F32), pltpu.VMEM((2, tq, tk), F32),
                        pltpu.VMEM((2, tq, tk), BF16), pltpu.VMEM((2, tq, LANE), F32),
                        pltpu.VMEM((2, tq, LANE), F32)],
        compiler_params=_params(("parallel", "parallel", "arbitrary")),
        name="mla_attn",
    )(q, k, v)


def _merge_kernel(o0_ref, l0_ref, o1_ref, l1_ref, o2_ref, l2_ref, ob_ref, g_ref, x_ref,
                  wa_ref, wb_ref, wo_ref, xo_ref, nat_scr):
    tm = x_ref.shape[0]
    w = A_GROUP_WIDTH

    def natural(ref, slot, dil):
        if dil == 1:
            return ref[...].astype(F32)
        rows = tm // dil
        nc = w // LANE
        for r in range(dil):
            for c in range(nc):
                lo = r * w + c * LANE
                nat_scr[slot * nc + c, pl.ds(r, rows, stride=dil), :] = ref[:, lo:lo + LANE].astype(F32)
        return jnp.concatenate([nat_scr[slot * nc + c] for c in range(nc)], axis=1)

    dils = [dil for _, dil in A_GROUPS]
    l0, l1, l2 = (natural(ref, 2 * g, dils[g]) for g, ref in enumerate((l0_ref, l1_ref, l2_ref)))
    o0, o1, o2 = (natural(ref, 2 * g + 1, dils[g]) for g, ref in enumerate((o0_ref, o1_ref, o2_ref)))
    m = jnp.maximum(jnp.maximum(l0, l1), l2)
    e0, e1, e2 = jnp.exp(l0 - m), jnp.exp(l1 - m), jnp.exp(l2 - m)
    oa = (e0 * o0 + e1 * o1 + e2 * o2) / (e0 + e1 + e2)
    pa = jnp.dot(oa.astype(BF16), wa_ref[...], preferred_element_type=F32)
    pb = jnp.dot(ob_ref[...], wb_ref[...], preferred_element_type=F32)
    d = pa.shape[1]
    merged = g_ref[:, :d].astype(F32) * pa + g_ref[:, d:].astype(F32) * pb
    xo_ref[...] = x_ref[...] + jnp.dot(merged.astype(BF16), wo_ref[...], preferred_element_type=F32)


def _merge(oas, lses, ob, gates, x, wa, wb, wo, *, tm):
    n, d = x.shape
    row = lambda a: pl.BlockSpec((tm * a.shape[0] // n, a.shape[1]), lambda i: (i, 0))
    full = lambda a: pl.BlockSpec(a.shape, lambda i: (0, 0))
    args = [oas[0], lses[0], oas[1], lses[1], oas[2], lses[2], ob, gates, x]
    return pl.pallas_call(
        _merge_kernel,
        grid=(n // tm,),
        in_specs=[row(a) for a in args] + [full(wa), full(wb), full(wo)],
        out_specs=pl.BlockSpec((tm, d), lambda i: (i, 0)),
        out_shape=jax.ShapeDtypeStruct((n, d), F32),
        scratch_shapes=[pltpu.VMEM((2 * len(A_GROUPS) * A_GROUP_WIDTH // LANE, tm, LANE), F32)],
        compiler_params=_params(("parallel",)),
        name="merge",
    )(*args, wa, wb, wo)


def _mem_kv_kernel(mem_ref, g_ref, w_ref, gk_ref, k_ref, v_ref):
    mh = _rms(mem_ref[...], g_ref[...]).astype(BF16)
    kv = jnp.dot(mh, w_ref[...], preferred_element_type=F32)
    for h in range(MEM_HEADS):
        sl = slice(h * MEM_HEAD_DIM, (h + 1) * MEM_HEAD_DIM)
        k_ref[:, sl] = _rms(kv[:, sl], gk_ref[...]).astype(k_ref.dtype)
    v_ref[...] = kv[:, MEM_WIDTH:].astype(v_ref.dtype)


def _mem_kv(mem, g, w, gk, *, tm):
    n = mem.shape[0]
    full = lambda a: pl.BlockSpec(a.shape, lambda i: (0, 0))
    return pl.pallas_call(
        _mem_kv_kernel,
        grid=(n // tm,),
        in_specs=[pl.BlockSpec((tm, mem.shape[1]), lambda i: (i, 0)), full(g), full(w), full(gk)],
        out_specs=[pl.BlockSpec((tm, MEM_WIDTH), lambda i: (i, 0))] * 2,
        out_shape=[jax.ShapeDtypeStruct((n, MEM_WIDTH), BF16)] * 2,
        compiler_params=_params(("parallel",)),
        name="mem_kv",
    )(mem, g, w, gk)


def _mem_attn_kernel(x_ref, k_ref, v_ref, gx_ref, wq_ref, gq_ref, wo_ref, gf_ref, wr_ref,
                     xo_ref, hf_ref, aff_ref):
    x = x_ref[...]
    hx = _rms(x, gx_ref[...]).astype(BF16)
    q = jnp.dot(hx, wq_ref[...], preferred_element_type=F32)
    outs = []
    for h in range(MEM_HEADS):
        sl = slice(h * MEM_HEAD_DIM, (h + 1) * MEM_HEAD_DIM)
        qn = _rms(q[:, sl], gq_ref[...]).astype(BF16)
        s = lax.dot_general(qn, k_ref[:, sl], _NT, preferred_element_type=F32)
        m = jnp.max(s, axis=-1, keepdims=True)
        p = jnp.exp(s - m)
        l = jnp.sum(p, axis=-1, keepdims=True)
        outs.append(jnp.dot(p.astype(BF16), v_ref[:, sl], preferred_element_type=F32) / l)
    o = jnp.concatenate(outs, axis=1).astype(BF16)
    x2 = x + jnp.dot(o, wo_ref[...], preferred_element_type=F32)
    xo_ref[...] = x2
    hf = _rms(x2, gf_ref[...]).astype(BF16)
    hf_ref[...] = hf
    logits = jnp.dot(hf, wr_ref[...], preferred_element_type=F32)
    lane = lax.broadcasted_iota(jnp.int32, logits.shape, 1)
    logits = jnp.where(lane < N_EXPERTS, logits, NEG_INF)
    e = jnp.exp(logits - jnp.max(logits, axis=-1, keepdims=True))
    aff_ref[...] = e / jnp.sum(e, axis=-1, keepdims=True)


def _mem_attention(x, kmem, vmem, gx, wq, gq, wo, gf, wr, *, batch, seq, mem_tokens, tm):
    n, d = x.shape
    nsb = seq // tm
    full = lambda a: pl.BlockSpec(a.shape, lambda b, i: (0, 0))
    row = lambda w: pl.BlockSpec((tm, w), lambda b, i: (b * nsb + i, 0))
    kv = pl.BlockSpec((mem_tokens, MEM_WIDTH), lambda b, i: (b, 0))
    return pl.pallas_call(
        _mem_attn_kernel,
        grid=(batch, nsb),
        in_specs=[row(d), kv, kv, full(gx), full(wq), full(gq), full(wo), full(gf), full(wr)],
        out_specs=[row(d), row(d), row(LANE)],
        out_shape=[jax.ShapeDtypeStruct((n, d), F32),
                   jax.ShapeDtypeStruct((n, d), BF16),
                   jax.ShapeDtypeStruct((n, LANE), F32)],
        compiler_params=_params(("parallel", "parallel")),
        name="mem_attn",
    )(x, kmem, vmem, gx, wq, gq, wo, gf, wr)


def _ffn_kernel(x_ref, wg_ref, wu_ref, wd_ref, g_ref, y_ref):
    x = x_ref[...]
    a = jnp.dot(x, wg_ref[...], preferred_element_type=F32)
    b = jnp.dot(x, wu_ref[...], preferred_element_type=F32)
    hid = (a * jax.nn.sigmoid(a) * b).astype(BF16)
    y = jnp.dot(hid, wd_ref[...], preferred_element_type=F32)
    g = g_ref[...]
    y_ref[...] = (y * jnp.concatenate([g] * (y.shape[1] // LANE), axis=1)).astype(y_ref.dtype)


def _expert_ffn(xe, wg, wu, wd, gates, *, tm):
    e, cap, d = xe.shape
    f = wg.shape[2]
    return pl.pallas_call(
        _ffn_kernel,
        grid=(e, cap // tm),
        in_specs=[pl.BlockSpec((None, tm, d), lambda e, i: (e, i, 0)),
                  pl.BlockSpec((None, d, f), lambda e, i: (e, 0, 0)),
                  pl.BlockSpec((None, d, f), lambda e, i: (e, 0, 0)),
                  pl.BlockSpec((None, f, d), lambda e, i: (e, 0, 0)),
                  pl.BlockSpec((None, tm, LANE), lambda e, i: (e, i, 0))],
        out_specs=pl.BlockSpec((None, tm, d), lambda e, i: (e, i, 0)),
        out_shape=jax.ShapeDtypeStruct((e, cap, d), BF16),
        compiler_params=_params(("parallel", "arbitrary")),
        name="expert_ffn",
    )(xe, wg, wu, wd, gates)


SLOT_WINDOW = 128
ROUTE_TOKENS = 512
COMBINE_ROWS = 128
ROW_ALIGN = 16
DISPATCH_STAGES = 4


def _select_kernel(aff_ref, sel_ref, *, cap):
    bits = pltpu.bitcast(aff_ref[...], jnp.int32)
    ne, n = bits.shape
    count = lambda mask: jnp.sum(jnp.where(mask, 1, 0), axis=1, keepdims=True)

    def value_bit(i, t):
        cand = t | jnp.left_shift(1, 30 - i)
        return jnp.where(count(bits >= cand) >= cap, cand, t)

    thr = lax.fori_loop(0, 31, value_bit, jnp.zeros((ne, 1), jnp.int32))
    gt = bits > thr
    eq = bits == thr
    need = cap - count(gt)
    idx = lax.broadcasted_iota(jnp.int32, (ne, n), 1)
    nbits = max(1, (n - 1).bit_length())

    def index_bit(i, j):
        cand = j | jnp.left_shift(1, nbits - 1 - i)
        return jnp.where(count(jnp.logical_and(eq, idx < cand)) < need, cand, j)

    last = lax.fori_loop(0, nbits, index_bit, jnp.zeros((ne, 1), jnp.int32))
    sel = jnp.logical_or(gt, jnp.logical_and(eq, idx <= last))
    sel_ref[...] = jnp.where(sel, 1.0, 0.0).astype(sel_ref.dtype)


def _select(aff_t, *, cap):
    return pl.pallas_call(
        functools.partial(_select_kernel, cap=cap),
        out_shape=jax.ShapeDtypeStruct(aff_t.shape, BF16),
        compiler_params=pltpu.CompilerParams(vmem_limit_bytes=VMEM_LIMIT),
        name="route_select",
    )(aff_t)


def _slots_kernel(sel_ref, upper_ref, lower_ref, pos_ref, base_ref):
    s = sel_ref[...]
    incl = jnp.dot(s, upper_ref[...], preferred_element_type=F32)
    tot = jnp.broadcast_to(incl[:, LANE - 1:LANE], incl.shape).astype(BF16)
    base = jnp.dot(lower_ref[...], tot, preferred_element_type=F32)
    pos_ref[...] = jnp.where(s > 0, base + incl - 1.0, -1.0).astype(jnp.int32)
    base_ref[...] = base.astype(jnp.int32)


def _slots(sel3):
    ne, nt, _ = sel3.shape
    upper = (jnp.arange(LANE)[:, None] <= jnp.arange(LANE)[None, :]).astype(BF16)
    lower = (jnp.arange(nt)[None, :] < jnp.arange(nt)[:, None]).astype(BF16)
    blk = pl.BlockSpec((None, nt, LANE), lambda e: (e, 0, 0))
    return pl.pallas_call(
        _slots_kernel,
        grid=(ne,),
        in_specs=[blk, pl.BlockSpec(upper.shape, lambda e: (0, 0)), pl.BlockSpec(lower.shape, lambda e: (0, 0))],
        out_specs=[blk, blk],
        out_shape=[jax.ShapeDtypeStruct(sel3.shape, jnp.int32)] * 2,
        compiler_params=_params(("parallel",)),
        name="route_slots",
    )(sel3, upper, lower)


def _dispatch_kernel(offs_ref, h_ref, pos_ref, aff_ref, xe_ref, gs_ref,
                     acc_scr, gacc_scr, stage_scr, gstage_scr, cnt_scr, sem, gsem, *, nb):
    b = pl.program_id(0)
    t = h_ref.shape[0]
    w = SLOT_WINDOW
    ns = stage_scr.shape[0]

    @pl.when(b == 0)
    def _():
        acc_scr[...] = jnp.zeros(acc_scr.shape, F32)
        gacc_scr[...] = jnp.zeros(gacc_scr.shape, F32)
        cnt_scr[0] = 0

    hb = h_ref[...]
    slot_iota = lax.broadcasted_iota(jnp.int32, (w, t), 0)

    def xe_copy(slot, e, win):
        return pltpu.make_async_copy(
            stage_scr.at[slot], xe_ref.at[e, pl.ds(pl.multiple_of(win * w, w), w), :], sem.at[slot])

    def gs_copy(slot, e, win):
        return pltpu.make_async_copy(
            gstage_scr.at[slot], gs_ref.at[e, pl.ds(pl.multiple_of(win * w, w), w), :], gsem.at[slot])

    for e in range(N_EXPERTS):
        o0 = offs_ref[e * (nb + 1) + b]
        o1 = offs_ref[e * (nb + 1) + b + 1]
        prow = pos_ref[e:e + 1, :]
        arow = aff_ref[e:e + 1, :]

        def visit(win, carry, e=e, o1=o1, prow=prow, arow=arow):
            place = jnp.where(slot_iota == prow - win * w, 1.0, 0.0)
            acc_scr[e] += jnp.dot(place.astype(BF16), hb, preferred_element_type=F32)
            gacc_scr[e] += jnp.sum(place * arow, axis=1, keepdims=True)

            @pl.when(o1 >= (win + 1) * w)
            def _():
                c = cnt_scr[0]
                slot = c % ns

                @pl.when(c >= ns)
                def _():
                    xe_copy(slot, e, win).wait()
                    gs_copy(slot, e, win).wait()

                stage_scr[slot] = acc_scr[e].astype(BF16)
                gstage_scr[slot] = jnp.broadcast_to(gacc_scr[e], (w, LANE))
                acc_scr[e] = jnp.zeros(acc_scr.shape[1:], F32)
                gacc_scr[e] = jnp.zeros(gacc_scr.shape[1:], F32)
                xe_copy(slot, e, win).start()
                gs_copy(slot, e, win).start()
                cnt_scr[0] = c + 1

            return carry

        lax.fori_loop(o0 // w, (o1 + w - 1) // w, visit, 0)

    @pl.when(b == nb - 1)
    def _():
        c = cnt_scr[0]
        for s in range(ns):
            @pl.when(c > s)
            def _(s=s):
                xe_copy(s, 0, 0).wait()
                gs_copy(s, 0, 0).wait()


def _dispatch(offs, h, pos, aff, *, cap):
    n, d = h.shape
    t = min(ROUTE_TOKENS, n)
    nb = n // t
    ne = N_EXPERTS
    w = SLOT_WINDOW
    grid_spec = pltpu.PrefetchScalarGridSpec(
        num_scalar_prefetch=1,
        grid=(nb,),
        in_specs=[pl.BlockSpec((t, d), lambda b, offs: (b, 0)),
                  pl.BlockSpec((ne, t), lambda b, offs: (0, b)),
                  pl.BlockSpec((ne, t), lambda b, offs: (0, b))],
        out_specs=[pl.BlockSpec(memory_space=pl.ANY), pl.BlockSpec(memory_space=pl.ANY)],
        scratch_shapes=[pltpu.VMEM((ne, w, d), F32), pltpu.VMEM((ne, w, 1), F32),
                        pltpu.VMEM((DISPATCH_STAGES, w, d), BF16), pltpu.VMEM((DISPATCH_STAGES, w, LANE), F32),
                        pltpu.SMEM((1,), jnp.int32),
                        pltpu.SemaphoreType.DMA((DISPATCH_STAGES,)), pltpu.SemaphoreType.DMA((DISPATCH_STAGES,))])
    return pl.pallas_call(
        functools.partial(_dispatch_kernel, nb=nb),
        grid_spec=grid_spec,
        out_shape=[jax.ShapeDtypeStruct((ne, cap, d), BF16), jax.ShapeDtypeStruct((ne, cap, LANE), F32)],
        compiler_params=_params(("arbitrary",)),
        name="route_dispatch",
    )(offs, h, pos, aff)


_TN = (((0,), (0,)), ((), ()))


def _combine_kernel(offs_ref, x_ref, pos_ref, ye_ref, o_ref, buf_scr, sem, *, nb, cap):
    b = pl.program_id(0)
    t = x_ref.shape[0]
    rows = buf_scr.shape[1] // 2
    al = ROW_ALIGN

    def first_row(e):
        return jnp.minimum((offs_ref[e * (nb + 1) + b] // al) * al, cap - rows)

    def fetch(e, start):
        return pltpu.make_async_copy(
            ye_ref.at[e, pl.ds(pl.multiple_of(start, al), rows), :],
            buf_scr.at[e // 2, pl.ds((e % 2) * rows, rows), :], sem.at[e])

    for e in range(N_EXPERTS):
        fetch(e, first_row(e)).start()
    o_ref[...] = x_ref[...]
    slot_iota = lax.broadcasted_iota(jnp.int32, (rows, t), 0)
    for pair in range(N_EXPERTS // 2):
        places = []
        for e in (2 * pair, 2 * pair + 1):
            start = first_row(e)
            fetch(e, start).wait()
            places.append(jnp.where(slot_iota == pos_ref[e:e + 1, :] - start, 1.0, 0.0).astype(BF16))
        place = jnp.concatenate(places, axis=0)
        o_ref[...] += lax.dot_general(place, buf_scr[pair], _TN, preferred_element_type=F32)

        for e in (2 * pair, 2 * pair + 1):
            start = first_row(e)
            o1 = offs_ref[e * (nb + 1) + b + 1]
            half = pl.ds((e % 2) * rows, rows)

            def more(j, carry, e=e, start=start, half=half):
                lo = start + j * rows
                ws = jnp.minimum(lo, cap - rows)
                cp = fetch(e, ws)
                cp.start()
                cp.wait()
                prow = pos_ref[e:e + 1, :]
                hit = jnp.logical_and(slot_iota == prow - ws, prow >= lo)
                o_ref[...] += lax.dot_general(jnp.where(hit, 1.0, 0.0).astype(BF16), buf_scr[pair, half, :],
                                              _TN, preferred_element_type=F32)
                return carry

            lax.fori_loop(1, (jnp.maximum(o1 - start, 1) + rows - 1) // rows, more, 0)


def _combine(offs, x, pos, ye):
    n, d = x.shape
    ne, cap, _ = ye.shape
    t = min(ROUTE_TOKENS, n)
    nb = n // t
    rows = min(COMBINE_ROWS, cap)
    grid_spec = pltpu.PrefetchScalarGridSpec(
        num_scalar_prefetch=1,
        grid=(nb,),
        in_specs=[pl.BlockSpec((t, d), lambda b, offs: (b, 0)),
                  pl.BlockSpec((ne, t), lambda b, offs: (0, b)),
                  pl.BlockSpec(memory_space=pl.ANY)],
        out_specs=pl.BlockSpec((t, d), lambda b, offs: (b, 0)),
        scratch_shapes=[pltpu.VMEM((ne // 2, 2 * rows, d), BF16), pltpu.SemaphoreType.DMA((ne,))])
    return pl.pallas_call(
        functools.partial(_combine_kernel, nb=nb, cap=cap),
        grid_spec=grid_spec,
        out_shape=jax.ShapeDtypeStruct((n, d), F32),
        compiler_params=_params(("arbitrary",)),
        name="route_combine",
    )(offs, x, pos, ye)


def _moe(x, hf, aff, lp):
    n, d = x.shape
    ne = N_EXPERTS
    cap = EC_CAPACITY_FACTOR * n // ne
    t = min(ROUTE_TOKENS, n)
    nb = n // t
    aff_t = aff[:, :ne].T
    sel = _select(aff_t, cap=cap)
    pos3, base3 = _slots(sel.reshape(ne, n // LANE, LANE))
    pos = pos3.reshape(ne, n)
    offs = jnp.concatenate([base3[:, ::t // LANE, 0], jnp.full((ne, 1), cap, jnp.int32)], axis=1)
    offs = offs.reshape(-1)
    xe, gs = _dispatch(offs, hf, pos, aff_t, cap=cap)
    ye = _expert_ffn(xe, lp["w_gate"], lp["w_up"], lp["w_down"], gs, tm=min(512, cap))
    return _combine(offs, x, pos, ye)


def _t5_bucket(rel):
    half = REL_BUCKETS // 2
    max_exact = half // 2
    n = jnp.abs(rel)
    base = jnp.where(rel > 0, half, 0)
    nf = jnp.maximum(n, 1).astype(F32)
    large = max_exact + (jnp.log(nf / max_exact) / math.log(REL_MAX_DIST / max_exact)
                         * (half - max_exact)).astype(jnp.int32)
    large = jnp.minimum(large, half - 1)
    return base + jnp.where(n < max_exact, n, large)


def _band_bias(rel_bias, group, dilation, tq):
    tk = tq + 2 * A_HALF
    rel = jnp.arange(tk)[None, :] - A_HALF - jnp.arange(tq)[:, None]
    heads = slice(group * A_HEADS_PER_GROUP, (group + 1) * A_HEADS_PER_GROUP)
    b = jnp.transpose(rel_bias[:, heads][_t5_bucket(rel * dilation)], (2, 0, 1)).astype(F32)
    return jnp.where((jnp.abs(rel) <= A_HALF)[None], b, NEG_INF)


def _block_diag_ones(size, block):
    idx = jnp.arange(size) // block
    return (idx[:, None] == idx[None, :]).astype(BF16)


def _rope_tables(seq):
    half = MLA_ROPE // 2
    freqs = ROPE_THETA ** (-jnp.arange(half, dtype=F32) / half)
    ang = jnp.arange(seq).astype(F32)[:, None] * freqs[None, :]
    cos, sin = jnp.cos(ang), jnp.sin(ang)
    ones = jnp.ones((seq, MLA_NOPE), F32)
    z64 = jnp.zeros((seq, MLA_NOPE), F32)
    z16 = jnp.zeros((seq, half), F32)
    z32 = jnp.zeros((seq, LANE - MLA_QK_DIM), F32)
    c = jnp.concatenate([ones, cos, cos, z32], axis=1)
    s1 = jnp.concatenate([z64, -sin, z16, z32], axis=1)
    s2 = jnp.concatenate([z64, z16, sin, z32], axis=1)
    return c, s1, s2


def _pad_heads(w, heads, width, lo, hi):
    w = w.reshape(w.shape[0], heads, width)[:, :, lo:hi]
    w = jnp.pad(w, ((0, 0), (0, 0), (0, MLA_HEAD_PAD - (hi - lo))))
    return w.reshape(w.shape[0], heads * MLA_HEAD_PAD)


def _prep_layer(p, l):
    d = D_MODEL
    row = lambda v: v.reshape(1, -1).astype(F32)
    w_in = p["w_in"][l]
    out = {}
    out["g_mix"] = row(p["norm_mix"][l])
    ng = len(A_GROUPS)
    out["w_a"] = w_in[:, :COL_CQ].reshape(d, 3, ng, A_GROUP_WIDTH).transpose(0, 2, 1, 3).reshape(
        d, COL_CQ).astype(BF16)
    out["aux_a"] = row(jnp.tile(jnp.concatenate([
        jnp.tile(p["a_q_norm"][l], A_HEADS_PER_GROUP) * (A_HEAD_DIM ** -0.5),
        jnp.tile(p["a_k_norm"][l], A_HEADS_PER_GROUP),
        jnp.ones((A_GROUP_WIDTH,), F32)]), ng))
    out["w_c"] = jnp.pad(w_in[:, COL_CQ:COL_GA], ((0, 0), (0, d - (COL_GA - COL_CQ)))).astype(BF16)
    out["w_g"] = w_in[:, COL_GA:].astype(BF16)
    out["g_ql"] = row(p["mla_q_lat_norm"][l])
    out["g_kl"] = row(p["mla_kv_lat_norm"][l])
    out["w_uq"] = _pad_heads(p["w_uq"][l], MLA_HEADS, MLA_QK_DIM, 0, MLA_QK_DIM).astype(BF16)
    w_ukv = p["w_ukv"][l]
    k_nope = _pad_heads(w_ukv, MLA_HEADS, MLA_NOPE + MLA_V, 0, MLA_NOPE)
    eye = jnp.pad(jnp.eye(MLA_ROPE, dtype=F32), ((0, 0), (MLA_NOPE, MLA_HEAD_PAD - MLA_QK_DIM)))
    k_pe = jnp.tile(eye, (1, MLA_HEADS))
    k_rows = d - MLA_Q_RANK - MLA_KV_RANK - MLA_ROPE
    out["w_uk"] = jnp.concatenate(
        [k_nope, k_pe, jnp.zeros((k_rows, MLA_HEADS * MLA_HEAD_PAD), F32)], axis=0).astype(BF16)
    out["w_uv"] = w_ukv.reshape(MLA_KV_RANK, MLA_HEADS, MLA_NOPE + MLA_V)[:, :, MLA_NOPE:].reshape(
        MLA_KV_RANK, MLA_HEADS * MLA_V).astype(BF16)
    pad_gain = lambda g: jnp.tile(jnp.pad(g, (0, MLA_HEAD_PAD - MLA_QK_DIM)), MLA_HEADS)
    out["g_q"] = row(pad_gain(p["mla_q_norm"][l]) * (MLA_QK_DIM ** -0.5 * math.log2(math.e)))
    out["g_k"] = row(pad_gain(p["mla_k_norm"][l]))
    out["w_pa"] = p["w_proj_a"][l].astype(BF16)
    out["w_pb"] = p["w_proj_b"][l].astype(BF16)
    out["w_out"] = p["w_out"][l].astype(BF16)
    out["g_mx"] = row(p["norm_mem_x"][l])
    out["g_mkv"] = row(p["norm_mem_kv"][l])
    out["w_mq"] = p["w_mq"][l].astype(BF16)
    out["w_mkv"] = p["w_mkv"][l].astype(BF16)
    out["g_mq"] = row(p["mem_q_norm"][l] * (MEM_HEAD_DIM ** -0.5))
    out["g_mk"] = row(p["mem_k_norm"][l])
    out["w_mo"] = p["w_mo"][l].astype(BF16)
    out["g_ffn"] = row(p["norm_ffn"][l])
    out["w_router"] = jnp.pad(p["w_router"][l], ((0, 0), (0, LANE - N_EXPERTS))).astype(BF16)
    out["w_gate"] = p["w_gate"][l].astype(BF16)
    out["w_up"] = p["w_up"][l].astype(BF16)
    out["w_down"] = p["w_down"][l].astype(BF16)
    return out


BAND_TQ = 128
BAND_STEP_ROWS = 256
MLA_ROW_BLOCK = 32


def _layer(x, mem, lp, shared, *, batch, seq):
    n = batch * seq
    mem_tokens = mem.shape[0] // batch
    tm = min(1024, n)
    bd64, bd128 = shared["bd64"], shared["bd128"]
    zgs = _norm_proj_heads(x, lp["g_mix"], lp["w_a"], lp["aux_a"], bd64, tm=min(512, n))
    zc = _norm_proj(x, lp["g_mix"], lp["w_c"], mode="plain", tm=tm, tn=D_MODEL, out_dtype=F32)
    gates = _norm_proj(x, lp["g_mix"], lp["w_g"], mode="sigmoid", tm=tm, tn=D_MODEL, out_dtype=BF16)
    oas, lses = [], []
    for g, (_, dil) in enumerate(A_GROUPS):
        o, lse = _band_attention(zgs[g], shared["band_bias"][g], shared["hmask"], batch=batch, seq=seq,
                                 group=g, dilation=dil, tq=BAND_TQ)
        oas.append(o)
        lses.append(lse)
    qm, km, vm = _mla_prep(zc, lp["g_ql"], lp["g_kl"], lp["w_uq"], lp["w_uk"], lp["w_uv"],
                           lp["g_q"], lp["g_k"], shared["rope_c"], shared["rope_s1"],
                           shared["rope_s2"], bd128, seq=seq, tm=min(512, seq))
    ob = _mla_attention(qm, km, vm, batch=batch, seq=seq, tq=min(512, seq), tk=min(512, seq))
    x = _merge(oas, lses, ob, gates, x, lp["w_pa"], lp["w_pb"], lp["w_out"], tm=min(512, n))
    kmem, vmem = _mem_kv(mem, lp["g_mkv"], lp["w_mkv"], lp["g_mk"], tm=min(512, mem.shape[0]))
    x, hf, aff = _mem_attention(x, kmem, vmem, lp["g_mx"], lp["w_mq"], lp["g_mq"], lp["w_mo"],
                                lp["g_ffn"], lp["w_router"], batch=batch, seq=seq,
                                mem_tokens=mem_tokens, tm=min(512, seq))
    return _moe(x, hf, aff, lp)


def _trunk(x, mem, layers, shared):
    batch, seq, d = x.shape
    xf = x.reshape(batch * seq, d)
    memf = mem.reshape(-1, d)
    for lp in layers:
        xf = _layer(xf, memf, lp, shared, batch=batch, seq=seq)
    return xf.reshape(batch, seq, d)


def kernel(x_prompt, x_sample, mem_prompt, mem_sample, norm_mix, w_in, a_q_norm, a_k_norm, rel_bias,
           mla_q_lat_norm, w_uq, mla_kv_lat_norm, w_ukv, mla_q_norm, mla_k_norm, w_proj_a, w_proj_b,
           w_out, norm_mem_x, norm_mem_kv, w_mq, w_mkv, mem_q_norm, mem_k_norm, w_mo, norm_ffn,
           w_router, w_gate, w_up, w_down):
    p = dict(norm_mix=norm_mix, w_in=w_in, a_q_norm=a_q_norm, a_k_norm=a_k_norm,
             mla_q_lat_norm=mla_q_lat_norm, w_uq=w_uq, mla_kv_lat_norm=mla_kv_lat_norm, w_ukv=w_ukv,
             mla_q_norm=mla_q_norm, mla_k_norm=mla_k_norm, w_proj_a=w_proj_a, w_proj_b=w_proj_b,
             w_out=w_out, norm_mem_x=norm_mem_x, norm_mem_kv=norm_mem_kv, w_mq=w_mq, w_mkv=w_mkv,
             mem_q_norm=mem_q_norm, mem_k_norm=mem_k_norm, w_mo=w_mo, norm_ffn=norm_ffn,
             w_router=w_router, w_gate=w_gate, w_up=w_up, w_down=w_down)
    layers = [_prep_layer(p, l) for l in range(DEPTH)]
    outs = []
    for x, mem in ((x_prompt, mem_prompt), (x_sample, mem_sample)):
        seq = x.shape[1]
        c, s1, s2 = _rope_tables(seq)
        hm = (jnp.arange(LANE)[None, :] // A_HEAD_DIM == jnp.arange(16)[:, None]).astype(BF16)
        shared = dict(
            bd64=_block_diag_ones(256, A_HEAD_DIM), bd128=_block_diag_ones(256, MLA_HEAD_PAD),
            band_bias=[_band_bias(rel_bias, g, dil, BAND_TQ) for g, (_, dil) in enumerate(A_GROUPS)],
            hmask=hm, rope_c=c, rope_s1=s1, rope_s2=s2)
        outs.append(_trunk(x, mem, layers, shared))
    return tuple(outs)
```

```python
import functools
import math

import jax
import jax.numpy as jnp
from jax import lax
from jax.experimental import pallas as pl
from jax.experimental.pallas import tpu as pltpu

F32 = jnp.float32
BF16 = jnp.bfloat16

D_MODEL = 1024
DEPTH = 2
A_GROUPS = ((128, 1), (512, 4), (2048, 16))
A_HEADS_PER_GROUP = 8
A_HEAD_DIM = 64
A_GROUP_WIDTH = A_HEADS_PER_GROUP * A_HEAD_DIM
A_WIDTH = A_GROUP_WIDTH * len(A_GROUPS)
A_HALF = 64
MLA_HEADS = 8
MLA_NOPE = 64
MLA_ROPE = 32
MLA_QK_DIM = MLA_NOPE + MLA_ROPE
MLA_V = 64
MLA_Q_RANK = 512
MLA_KV_RANK = 256
MLA_HEAD_PAD = 128
ROPE_THETA = 10000.0
MEM_HEADS = 4
MEM_HEAD_DIM = 128
MEM_WIDTH = MEM_HEADS * MEM_HEAD_DIM
REL_BUCKETS = 32
REL_MAX_DIST = 1024
N_EXPERTS = 16
EC_CAPACITY_FACTOR = 2
RMS_EPS = 1e-6
NEG_INF = -1e30
COL_CQ = 3 * A_WIDTH
COL_CKV = COL_CQ + MLA_Q_RANK
COL_GA = COL_CKV + MLA_KV_RANK + MLA_ROPE
IN_COLS = COL_GA + 2 * D_MODEL

LANE = 128
VMEM_LIMIT = 48 * 1024 * 1024

_NT = (((1,), (1,)), ((), ()))


def _rms(x, g):
    ms = jnp.mean(x * x, axis=-1, keepdims=True)
    return x * lax.rsqrt(ms + RMS_EPS) * g


def _params(sem):
    return pltpu.CompilerParams(dimension_semantics=sem, vmem_limit_bytes=VMEM_LIMIT)


def _proj_kernel(x_ref, g_ref, w_ref, o_ref, h_scr, *, mode):
    j = pl.program_id(1)

    @pl.when(j == 0)
    def _():
        h_scr[...] = _rms(x_ref[...], g_ref[...]).astype(BF16)

    z = jnp.dot(h_scr[...], w_ref[...], preferred_element_type=F32)
    if mode == "sigmoid":
        z = jax.nn.sigmoid(z)
    o_ref[...] = z.astype(o_ref.dtype)


def _proj_heads_kernel(x_ref, g_ref, w_ref, aux_ref, bd_ref, o0_ref, o1_ref, o2_ref, hf_scr, hp_scr):
    j = pl.program_id(1)
    tm = x_ref.shape[0]
    w = 3 * A_GROUP_WIDTH

    @pl.when(j == 0)
    def _():
        h = _rms(x_ref[...], g_ref[...])
        for c in range(h.shape[1] // LANE):
            hf_scr[c] = h[:, c * LANE:(c + 1) * LANE]
        for g, (_, dil) in enumerate(A_GROUPS):
            if dil == 1:
                hp_scr[g] = h.astype(BF16)
                continue
            rows = tm // dil
            for r in range(dil):
                for c in range(h.shape[1] // LANE):
                    hp_scr[g, r * rows:(r + 1) * rows, c * LANE:(c + 1) * LANE] = (
                        hf_scr[c, pl.ds(r, rows, stride=dil), :].astype(BF16))

    z = jnp.dot(hp_scr[j], w_ref[...], preferred_element_type=F32)
    cols = []
    for c in range(2 * A_GROUP_WIDTH // 256):
        sl = slice(c * 256, (c + 1) * 256)
        zc = z[:, sl]
        ss = jnp.dot((zc * zc).astype(BF16), bd_ref[...], preferred_element_type=F32)
        cols.append((zc * lax.rsqrt(ss * (1.0 / A_HEAD_DIM) + RMS_EPS) * aux_ref[:, sl]).astype(BF16))
    cols.append(z[:, 2 * A_GROUP_WIDTH:].astype(BF16))
    zn = jnp.concatenate(cols, axis=1)
    for g, ((_, dil), o_ref) in enumerate(zip(A_GROUPS, (o0_ref, o1_ref, o2_ref))):
        @pl.when(j == g)
        def _(dil=dil, o_ref=o_ref):
            rows = tm // dil
            for r in range(dil):
                o_ref[:, r * w:(r + 1) * w] = zn[r * rows:(r + 1) * rows, :]


def _norm_proj_heads(x, g, w, aux, bd, *, tm):
    n, d = x.shape
    wg = 3 * A_GROUP_WIDTH
    ng = len(A_GROUPS)
    return pl.pallas_call(
        _proj_heads_kernel,
        grid=(n // tm, ng),
        in_specs=[
            pl.BlockSpec((tm, d), lambda i, j: (i, 0)),
            pl.BlockSpec((1, d), lambda i, j: (0, 0)),
            pl.BlockSpec((d, wg), lambda i, j: (0, j)),
            pl.BlockSpec((1, wg), lambda i, j: (0, j)),
            pl.BlockSpec(bd.shape, lambda i, j: (0, 0)),
        ],
        out_specs=[pl.BlockSpec((tm // dil, dil * wg), lambda i, j: (i, 0)) for _, dil in A_GROUPS],
        out_shape=[jax.ShapeDtypeStruct((n // dil, dil * wg), BF16) for _, dil in A_GROUPS],
        scratch_shapes=[pltpu.VMEM((d // LANE, tm, LANE), F32), pltpu.VMEM((ng, tm, d), BF16)],
        compiler_params=_params(("parallel", "arbitrary")),
        name="norm_proj_heads",
    )(x, g, w, aux, bd)


def _norm_proj(x, g, w, *, mode, tm, tn, out_dtype):
    n, d = x.shape
    ncols = w.shape[1]
    return pl.pallas_call(
        functools.partial(_proj_kernel, mode=mode),
        grid=(n // tm, ncols // tn),
        in_specs=[
            pl.BlockSpec((tm, d), lambda i, j: (i, 0)),
            pl.BlockSpec((1, d), lambda i, j: (0, 0)),
            pl.BlockSpec((d, tn), lambda i, j: (0, j)),
        ],
        out_specs=pl.BlockSpec((tm, tn), lambda i, j: (i, j)),
        out_shape=jax.ShapeDtypeStruct((n, ncols), out_dtype),
        scratch_shapes=[pltpu.VMEM((tm, d), BF16)],
        compiler_params=_params(("parallel", "arbitrary")),
        name="norm_proj_" + mode,
    )(x, g, w)


def _band_attn_kernel(q_ref, kl_ref, km_ref, kh_ref, vl_ref, vm_ref, vh_ref, bias_ref, hm_ref,
                      o_ref, lse_ref, *, tq, seq_len):
    i = pl.program_id(2)
    tk = tq + 2 * A_HALF
    nsub = q_ref.shape[0] // tq
    k_all = jnp.concatenate([kl_ref[...], km_ref[...], kh_ref[...]], axis=0)
    v_all = jnp.concatenate([vl_ref[...], vm_ref[...], vh_ref[...]], axis=0)
    first = lax.broadcasted_iota(jnp.int32, (tq, LANE), 1) < A_HEAD_DIM
    for sub in range(nsub):
        q = q_ref[sub * tq:(sub + 1) * tq, :]
        k = k_all[sub * tq:sub * tq + tk, :]
        v = v_all[sub * tq:sub * tq + tk, :]
        kpos = (i * nsub + sub) * tq - A_HALF + lax.broadcasted_iota(jnp.int32, (1, tk), 1)
        valid = jnp.logical_and(kpos >= 0, kpos < seq_len)
        o_parts, lse_parts = [], []
        for j in range(A_GROUP_WIDTH // LANE):
            sl = slice(j * LANE, (j + 1) * LANE)
            qp, kp, vp = q[:, sl], k[:, sl], v[:, sl]
            res = []
            for hh in range(2):
                qm = qp * hm_ref[hh:hh + 1, :]
                s = lax.dot_general(qm, kp, _NT, preferred_element_type=F32)
                s = jnp.where(valid, s + bias_ref[2 * j + hh], NEG_INF)
                m = jnp.max(s, axis=-1, keepdims=True)
                p = jnp.exp(s - m)
                l = jnp.sum(p, axis=-1, keepdims=True)
                o = jnp.dot(p.astype(BF16), vp, preferred_element_type=F32) / l
                res.append((o, m + jnp.log(l)))
            o_parts.append(jnp.where(first, res[0][0], res[1][0]))
            lse_parts.append(jnp.where(first, res[0][1], res[1][1]))
        o_ref[sub * tq:(sub + 1) * tq, :] = jnp.concatenate(o_parts, axis=1).astype(o_ref.dtype)
        lse_ref[sub * tq:(sub + 1) * tq, :] = jnp.concatenate(lse_parts, axis=1)


def _band_attention(zg, bias, hmask, *, batch, seq, group, dilation, tq):
    sl = seq // dilation
    zv = zg.reshape(batch, sl, dilation * 3 * A_GROUP_WIDTH)
    step = min(BAND_STEP_ROWS, sl)
    hb = step // A_HALF
    nhb = sl // A_HALF
    w = A_GROUP_WIDTH

    def main(off):
        return pl.BlockSpec((None, step, w), lambda b, r, i: (b, i, r * 3 + off))

    def lo(off):
        return pl.BlockSpec((None, A_HALF, w),
                            lambda b, r, i: (b, jnp.maximum(i * hb - 1, 0), r * 3 + off))

    def hi(off):
        return pl.BlockSpec((None, A_HALF, w),
                            lambda b, r, i: (b, jnp.minimum((i + 1) * hb, nhb - 1), r * 3 + off))

    out_spec = pl.BlockSpec((None, step, w), lambda b, r, i: (b, i, r))
    o, lse = pl.pallas_call(
        functools.partial(_band_attn_kernel, tq=tq, seq_len=sl),
        grid=(batch, dilation, sl // step),
        in_specs=[main(0), lo(1), main(1), hi(1), lo(2), main(2), hi(2),
                  pl.BlockSpec(bias.shape, lambda b, r, i: (0, 0, 0)),
                  pl.BlockSpec(hmask.shape, lambda b, r, i: (0, 0))],
        out_specs=[out_spec, out_spec],
        out_shape=[jax.ShapeDtypeStruct((batch, sl, dilation * w), BF16),
                   jax.ShapeDtypeStruct((batch, sl, dilation * w), F32)],
        compiler_params=_params(("parallel", "parallel", "parallel")),
        name=f"band_attn_g{group}",
    )(zv, zv, zv, zv, zv, zv, zv, bias, hmask)
    return o.reshape(batch * sl, dilation * w), lse.reshape(batch * sl, dilation * w)


def _mla_prep_kernel(zc_ref, gql_ref, gkl_ref, wq_ref, wk_ref, wv_ref, gq_ref, gk_ref,
                     c_ref, s1_ref, s2_ref, bd_ref, q_ref, k_ref, v_ref):
    zc = zc_ref[...]
    cq = _rms(zc[:, :MLA_Q_RANK], gql_ref[...]).astype(BF16)
    ckv = _rms(zc[:, MLA_Q_RANK:MLA_Q_RANK + MLA_KV_RANK], gkl_ref[...])
    kin = jnp.concatenate([ckv, zc[:, MLA_Q_RANK + MLA_KV_RANK:]], axis=1).astype(BF16)
    q = jnp.dot(cq, wq_ref[...], preferred_element_type=F32)
    k = jnp.dot(kin, wk_ref[...], preferred_element_type=F32)
    v = jnp.dot(ckv.astype(BF16), wv_ref[...], preferred_element_type=F32)
    first = lax.broadcasted_iota(jnp.int32, (v.shape[0], LANE), 1) < MLA_V
    for j in range(MLA_HEADS // 2):
        vp = v[:, j * LANE:(j + 1) * LANE]
        v_ref[:, 2 * j * LANE:(2 * j + 1) * LANE] = jnp.where(first, vp, 1.0).astype(v_ref.dtype)
        v_ref[:, (2 * j + 1) * LANE:(2 * j + 2) * LANE] = jnp.where(first, 1.0, vp).astype(v_ref.dtype)
    cos, s1, s2 = c_ref[...], s1_ref[...], s2_ref[...]
    for src, g_ref, dst in ((q, gq_ref, q_ref), (k, gk_ref, k_ref)):
        for c in range(src.shape[1] // 256):
            sl = slice(c * 256, (c + 1) * 256)
            xc = src[:, sl]
            ss = jnp.dot((xc * xc).astype(BF16), bd_ref[...], preferred_element_type=F32)
            xn = xc * lax.rsqrt(ss * (1.0 / MLA_QK_DIM) + RMS_EPS) * g_ref[:, sl]
            for hh in range(2):
                xh = xn[:, hh * LANE:(hh + 1) * LANE]
                out = (xh * cos + pltpu.roll(xh, LANE - MLA_ROPE // 2, 1) * s1
                       + pltpu.roll(xh, MLA_ROPE // 2, 1) * s2)
                lo = c * 256 + hh * LANE
                dst[:, lo:lo + LANE] = out.astype(dst.dtype)


def _mla_prep(zc, gql, gkl, wq, wk, wv, gq, gk, cos, s1, s2, bd, *, seq, tm):
    n = zc.shape[0]
    hp = MLA_HEADS * MLA_HEAD_PAD
    vw = MLA_HEADS * MLA_HEAD_PAD
    nsb = seq // tm
    full = lambda a: pl.BlockSpec(a.shape, lambda i: (0,) * a.ndim)
    tab = pl.BlockSpec((tm, LANE), lambda i: (i % nsb, 0))
    return pl.pallas_call(
        _mla_prep_kernel,
        grid=(n // tm,),
        in_specs=[pl.BlockSpec((tm, zc.shape[1]), lambda i: (i, 0)),
                  full(gql), full(gkl), full(wq), full(wk), full(wv), full(gq), full(gk),
                  tab, tab, tab, full(bd)],
        out_specs=[pl.BlockSpec((tm, hp), lambda i: (i, 0)),
                   pl.BlockSpec((tm, hp), lambda i: (i, 0)),
                   pl.BlockSpec((tm, vw), lambda i: (i, 0))],
        out_shape=[jax.ShapeDtypeStruct((n, hp), BF16),
                   jax.ShapeDtypeStruct((n, hp), BF16),
                   jax.ShapeDtypeStruct((n, vw), BF16)],
        compiler_params=_params(("parallel",)),
        name="mla_prep",
    )(zc, gql, gkl, wq, wk, wv, gq, gk, cos, s1, s2, bd)


def _mla_attn_kernel(q_ref, k_ref, v_ref, o_ref, sa_scr, sb_scr, pa_scr, pb_scr, aa_scr, ab_scr,
                     m_scr, acc_scr, *, tk, rb):
    tq = q_ref.shape[0]
    nk = k_ref.shape[0] // tk
    hs = [slice(hh * MLA_HEAD_PAD, (hh + 1) * MLA_HEAD_PAD) for hh in range(2)]

    def rows_of(c):
        return pl.ds(pl.multiple_of(c * tk, tk), tk)

    def scores(c, dst):
        for hh in range(2):
            dst[hh] = lax.dot_general(q_ref[:, hs[hh]], k_ref[rows_of(c), hs[hh]], _NT,
                                      preferred_element_type=F32)

    def softmax(src, p_dst, a_dst):
        for hh in range(2):
            for r in range(tq // rb):
                rs = slice(r * rb, (r + 1) * rb)
                s = src[hh, rs, :]
                m_old = m_scr[hh, rs, :]
                m_new = jnp.maximum(m_old, jnp.max(s, axis=-1, keepdims=True))
                p_dst[hh, rs, :] = jnp.exp2(s - m_new).astype(BF16)
                m_scr[hh, rs, :] = m_new
                a_dst[hh, rs, :] = jnp.exp2(m_old - m_new)

    def values(c, p_src, a_src):
        for hh in range(2):
            acc_scr[hh] = a_src[hh] * acc_scr[hh] + jnp.dot(p_src[hh], v_ref[rows_of(c), hs[hh]],
                                                            preferred_element_type=F32)

    m_scr[...] = jnp.full(m_scr.shape, NEG_INF, F32)
    acc_scr[...] = jnp.zeros(acc_scr.shape, F32)
    scores(0, sa_scr)
    softmax(sa_scr, pa_scr, aa_scr)
    scores(1, sb_scr)

    def body(i, carry):
        values(2 * i, pa_scr, aa_scr)
        softmax(sb_scr, pb_scr, ab_scr)
        scores(2 * i + 2, sa_scr)
        values(2 * i + 1, pb_scr, ab_scr)
        softmax(sa_scr, pa_scr, aa_scr)
        scores(2 * i + 3, sb_scr)
        return carry

    lax.fori_loop(0, nk // 2 - 1, body, 0)
    values(nk - 2, pa_scr, aa_scr)
    softmax(sb_scr, pb_scr, ab_scr)
    values(nk - 1, pb_scr, ab_scr)
    acc0, acc1 = acc_scr[0], acc_scr[1]
    o0 = acc0 / pltpu.roll(acc0, MLA_V, 1)
    o1 = acc1 / pltpu.roll(acc1, MLA_V, 1)
    first = lax.broadcasted_iota(jnp.int32, (tq, LANE), 1) < MLA_V
    o_ref[...] = jnp.where(first, o0, o1).astype(o_ref.dtype)


def _mla_attention(q, k, v, *, batch, seq, tq, tk):
    n = batch * seq
    nqb = seq // tq
    assert (seq // tk) % 2 == 0
    return pl.pallas_call(
        functools.partial(_mla_attn_kernel, tk=tk, rb=MLA_ROW_BLOCK),
        grid=(batch, MLA_HEADS // 2, nqb),
        in_specs=[pl.BlockSpec((tq, 2 * MLA_HEAD_PAD), lambda b, h, i: (b * nqb + i, h)),
                  pl.BlockSpec((seq, 2 * MLA_HEAD_PAD), lambda b, h, i: (b, h)),
                  pl.BlockSpec((seq, 2 * MLA_HEAD_PAD), lambda b, h, i: (b, h))],
        out_specs=pl.BlockSpec((tq, 2 * MLA_V), lambda b, h, i: (b * nqb + i, h)),
        out_shape=jax.ShapeDtypeStruct((n, MLA_HEADS * MLA_V), BF16),
        scratch_shapes=[pltpu.VMEM((2, tq, tk), F32), pltpu.VMEM((2, tq, tk), F32),
                        pltpu.VMEM((2, tq, tk), BF16), pltpu.VMEM((2, tq, tk), BF16),
                        pltpu.VMEM((2, tq, 1), F32), pltpu.VMEM((2, tq, 1), F32),
                        pltpu.VMEM((2, tq, 1), F32), pltpu.VMEM((2, tq, LANE), F32)],
        compiler_params=_params(("parallel", "parallel", "arbitrary")),
        name="mla_attn",
    )(q, k, v)


def _merge_kernel(o0_ref, l0_ref, o1_ref, l1_ref, o2_ref, l2_ref, ob_ref, g_ref, x_ref,
                  wa_ref, wb_ref, wo_ref, xo_ref, nat_scr):
    tm = x_ref.shape[0]
    w = A_GROUP_WIDTH

    def natural(ref, slot, dil):
        if dil == 1:
            return ref[...].astype(F32)
        rows = tm // dil
        nc = w // LANE
        for r in range(dil):
            for c in range(nc):
                lo = r * w + c * LANE
                nat_scr[slot * nc + c, pl.ds(r, rows, stride=dil), :] = ref[:, lo:lo + LANE].astype(F32)
        return jnp.concatenate([nat_scr[slot * nc + c] for c in range(nc)], axis=1)

    dils = [dil for _, dil in A_GROUPS]
    l0, l1, l2 = (natural(ref, 2 * g, dils[g]) for g, ref in enumerate((l0_ref, l1_ref, l2_ref)))
    o0, o1, o2 = (natural(ref, 2 * g + 1, dils[g]) for g, ref in enumerate((o0_ref, o1_ref, o2_ref)))
    m = jnp.maximum(jnp.maximum(l0, l1), l2)
    e0, e1, e2 = jnp.exp(l0 - m), jnp.exp(l1 - m), jnp.exp(l2 - m)
    oa = (e0 * o0 + e1 * o1 + e2 * o2) / (e0 + e1 + e2)
    pa = jnp.dot(oa.astype(BF16), wa_ref[...], preferred_element_type=F32)
    pb = jnp.dot(ob_ref[...], wb_ref[...], preferred_element_type=F32)
    d = pa.shape[1]
    merged = g_ref[:, :d].astype(F32) * pa + g_ref[:, d:].astype(F32) * pb
    xo_ref[...] = x_ref[...] + jnp.dot(merged.astype(BF16), wo_ref[...], preferred_element_type=F32)


def _merge(oas, lses, ob, gates, x, wa, wb, wo, *, tm):
    n, d = x.shape
    row = lambda a: pl.BlockSpec((tm * a.shape[0] // n, a.shape[1]), lambda i: (i, 0))
    full = lambda a: pl.BlockSpec(a.shape, lambda i: (0, 0))
    args = [oas[0], lses[0], oas[1], lses[1], oas[2], lses[2], ob, gates, x]
    return pl.pallas_call(
        _merge_kernel,
        grid=(n // tm,),
        in_specs=[row(a) for a in args] + [full(wa), full(wb), full(wo)],
        out_specs=pl.BlockSpec((tm, d), lambda i: (i, 0)),
        out_shape=jax.ShapeDtypeStruct((n, d), F32),
        scratch_shapes=[pltpu.VMEM((2 * len(A_GROUPS) * A_GROUP_WIDTH // LANE, tm, LANE), F32)],
        compiler_params=_params(("parallel",)),
        name="merge",
    )(*args, wa, wb, wo)


def _mem_kv_kernel(mem_ref, g_ref, w_ref, gk_ref, k_ref, v_ref):
    mh = _rms(mem_ref[...], g_ref[...]).astype(BF16)
    kv = jnp.dot(mh, w_ref[...], preferred_element_type=F32)
    for h in range(MEM_HEADS):
        sl = slice(h * MEM_HEAD_DIM, (h + 1) * MEM_HEAD_DIM)
        k_ref[:, sl] = _rms(kv[:, sl], gk_ref[...]).astype(k_ref.dtype)
    v_ref[...] = kv[:, MEM_WIDTH:].astype(v_ref.dtype)


def _mem_kv(mem, g, w, gk, *, tm):
    n = mem.shape[0]
    full = lambda a: pl.BlockSpec(a.shape, lambda i: (0, 0))
    return pl.pallas_call(
        _mem_kv_kernel,
        grid=(n // tm,),
        in_specs=[pl.BlockSpec((tm, mem.shape[1]), lambda i: (i, 0)), full(g), full(w), full(gk)],
        out_specs=[pl.BlockSpec((tm, MEM_WIDTH), lambda i: (i, 0))] * 2,
        out_shape=[jax.ShapeDtypeStruct((n, MEM_WIDTH), BF16)] * 2,
        compiler_params=_params(("parallel",)),
        name="mem_kv",
    )(mem, g, w, gk)


def _mem_attn_kernel(x_ref, k_ref, v_ref, gx_ref, wq_ref, gq_ref, wo_ref, gf_ref, wr_ref,
                     xo_ref, hf_ref, aff_ref):
    x = x_ref[...]
    hx = _rms(x, gx_ref[...]).astype(BF16)
    q = jnp.dot(hx, wq_ref[...], preferred_element_type=F32)
    outs = []
    for h in range(MEM_HEADS):
        sl = slice(h * MEM_HEAD_DIM, (h + 1) * MEM_HEAD_DIM)
        qn = _rms(q[:, sl], gq_ref[...]).astype(BF16)
        s = lax.dot_general(qn, k_ref[:, sl], _NT, preferred_element_type=F32)
        m = jnp.max(s, axis=-1, keepdims=True)
        p = jnp.exp(s - m)
        l = jnp.sum(p, axis=-1, keepdims=True)
        outs.append(jnp.dot(p.astype(BF16), v_ref[:, sl], preferred_element_type=F32) / l)
    o = jnp.concatenate(outs, axis=1).astype(BF16)
    x2 = x + jnp.dot(o, wo_ref[...], preferred_element_type=F32)
    xo_ref[...] = x2
    hf = _rms(x2, gf_ref[...]).astype(BF16)
    hf_ref[...] = hf
    logits = jnp.dot(hf, wr_ref[...], preferred_element_type=F32)
    lane = lax.broadcasted_iota(jnp.int32, logits.shape, 1)
    logits = jnp.where(lane < N_EXPERTS, logits, NEG_INF)
    e = jnp.exp(logits - jnp.max(logits, axis=-1, keepdims=True))
    aff_ref[...] = e / jnp.sum(e, axis=-1, keepdims=True)


def _mem_attention(x, kmem, vmem, gx, wq, gq, wo, gf, wr, *, batch, seq, mem_tokens, tm):
    n, d = x.shape
    nsb = seq // tm
    full = lambda a: pl.BlockSpec(a.shape, lambda b, i: (0, 0))
    row = lambda w: pl.BlockSpec((tm, w), lambda b, i: (b * nsb + i, 0))
    kv = pl.BlockSpec((mem_tokens, MEM_WIDTH), lambda b, i: (b, 0))
    return pl.pallas_call(
        _mem_attn_kernel,
        grid=(batch, nsb),
        in_specs=[row(d), kv, kv, full(gx), full(wq), full(gq), full(wo), full(gf), full(wr)],
        out_specs=[row(d), row(d), row(LANE)],
        out_shape=[jax.ShapeDtypeStruct((n, d), F32),
                   jax.ShapeDtypeStruct((n, d), BF16),
                   jax.ShapeDtypeStruct((n, LANE), F32)],
        compiler_params=_params(("parallel", "parallel")),
        name="mem_attn",
    )(x, kmem, vmem, gx, wq, gq, wo, gf, wr)


def _ffn_kernel(x_ref, wg_ref, wu_ref, wd_ref, g_ref, y_ref):
    x = x_ref[...]
    a = jnp.dot(x, wg_ref[...], preferred_element_type=F32)
    b = jnp.dot(x, wu_ref[...], preferred_element_type=F32)
    hid = (a * jax.nn.sigmoid(a) * b).astype(BF16)
    y = jnp.dot(hid, wd_ref[...], preferred_element_type=F32)
    g = g_ref[...]
    y_ref[...] = (y * jnp.concatenate([g] * (y.shape[1] // LANE), axis=1)).astype(y_ref.dtype)


def _expert_ffn(xe, wg, wu, wd, gates, *, tm):
    e, cap, d = xe.shape
    f = wg.shape[2]
    return pl.pallas_call(
        _ffn_kernel,
        grid=(e, cap // tm),
        in_specs=[pl.BlockSpec((None, tm, d), lambda e, i: (e, i, 0)),
                  pl.BlockSpec((None, d, f), lambda e, i: (e, 0, 0)),
                  pl.BlockSpec((None, d, f), lambda e, i: (e, 0, 0)),
                  pl.BlockSpec((None, f, d), lambda e, i: (e, 0, 0)),
                  pl.BlockSpec((None, tm, LANE), lambda e, i: (e, i, 0))],
        out_specs=pl.BlockSpec((None, tm, d), lambda e, i: (e, i, 0)),
        out_shape=jax.ShapeDtypeStruct((e, cap, d), BF16),
        compiler_params=_params(("parallel", "arbitrary")),
        name="expert_ffn",
    )(xe, wg, wu, wd, gates)


SLOT_WINDOW = 128
ROUTE_TOKENS = 512
COMBINE_ROWS = 128
ROW_ALIGN = 16
DISPATCH_STAGES = 4


def _select_kernel(aff_ref, sel_ref, *, cap):
    bits = pltpu.bitcast(aff_ref[...], jnp.int32)
    ne, n = bits.shape
    count = lambda mask: jnp.sum(jnp.where(mask, 1, 0), axis=1, keepdims=True)

    def value_bit(i, t):
        cand = t | jnp.left_shift(1, 30 - i)
        return jnp.where(count(bits >= cand) >= cap, cand, t)

    thr = lax.fori_loop(0, 31, value_bit, jnp.zeros((ne, 1), jnp.int32))
    gt = bits > thr
    eq = bits == thr
    need = cap - count(gt)
    idx = lax.broadcasted_iota(jnp.int32, (ne, n), 1)
    nbits = max(1, (n - 1).bit_length())

    def index_bit(i, j):
        cand = j | jnp.left_shift(1, nbits - 1 - i)
        return jnp.where(count(jnp.logical_and(eq, idx < cand)) < need, cand, j)

    last = lax.fori_loop(0, nbits, index_bit, jnp.zeros((ne, 1), jnp.int32))
    sel = jnp.logical_or(gt, jnp.logical_and(eq, idx <= last))
    sel_ref[...] = jnp.where(sel, 1.0, 0.0).astype(sel_ref.dtype)


def _select(aff_t, *, cap):
    return pl.pallas_call(
        functools.partial(_select_kernel, cap=cap),
        out_shape=jax.ShapeDtypeStruct(aff_t.shape, BF16),
        compiler_params=pltpu.CompilerParams(vmem_limit_bytes=VMEM_LIMIT),
        name="route_select",
    )(aff_t)


def _slots_kernel(sel_ref, upper_ref, lower_ref, pos_ref, base_ref):
    s = sel_ref[...]
    incl = jnp.dot(s, upper_ref[...], preferred_element_type=F32)
    tot = jnp.broadcast_to(incl[:, LANE - 1:LANE], incl.shape).astype(BF16)
    base = jnp.dot(lower_ref[...], tot, preferred_element_type=F32)
    pos_ref[...] = jnp.where(s > 0, base + incl - 1.0, -1.0).astype(jnp.int32)
    base_ref[...] = base.astype(jnp.int32)


def _slots(sel3):
    ne, nt, _ = sel3.shape
    upper = (jnp.arange(LANE)[:, None] <= jnp.arange(LANE)[None, :]).astype(BF16)
    lower = (jnp.arange(nt)[None, :] < jnp.arange(nt)[:, None]).astype(BF16)
    blk = pl.BlockSpec((None, nt, LANE), lambda e: (e, 0, 0))
    return pl.pallas_call(
        _slots_kernel,
        grid=(ne,),
        in_specs=[blk, pl.BlockSpec(upper.shape, lambda e: (0, 0)), pl.BlockSpec(lower.shape, lambda e: (0, 0))],
        out_specs=[blk, blk],
        out_shape=[jax.ShapeDtypeStruct(sel3.shape, jnp.int32)] * 2,
        compiler_params=_params(("parallel",)),
        name="route_slots",
    )(sel3, upper, lower)


def _dispatch_kernel(offs_ref, h_ref, pos_ref, aff_ref, xe_ref, gs_ref,
                     acc_scr, gacc_scr, stage_scr, gstage_scr, cnt_scr, sem, gsem, *, nb):
    b = pl.program_id(0)
    t = h_ref.shape[0]
    w = SLOT_WINDOW
    ns = stage_scr.shape[0]

    @pl.when(b == 0)
    def _():
        acc_scr[...] = jnp.zeros(acc_scr.shape, F32)
        gacc_scr[...] = jnp.zeros(gacc_scr.shape, F32)
        cnt_scr[0] = 0

    hb = h_ref[...]
    slot_iota = lax.broadcasted_iota(jnp.int32, (w, t), 0)

    def xe_copy(slot, e, win):
        return pltpu.make_async_copy(
            stage_scr.at[slot], xe_ref.at[e, pl.ds(pl.multiple_of(win * w, w), w), :], sem.at[slot])

    def gs_copy(slot, e, win):
        return pltpu.make_async_copy(
            gstage_scr.at[slot], gs_ref.at[e, pl.ds(pl.multiple_of(win * w, w), w), :], gsem.at[slot])

    for e in range(N_EXPERTS):
        o0 = offs_ref[e * (nb + 1) + b]
        o1 = offs_ref[e * (nb + 1) + b + 1]
        prow = pos_ref[e:e + 1, :]
        arow = aff_ref[e:e + 1, :]

        def visit(win, carry, e=e, o1=o1, prow=prow, arow=arow):
            place = jnp.where(slot_iota == prow - win * w, 1.0, 0.0)
            acc_scr[e] += jnp.dot(place.astype(BF16), hb, preferred_element_type=F32)
            gacc_scr[e] += jnp.sum(place * arow, axis=1, keepdims=True)

            @pl.when(o1 >= (win + 1) * w)
            def _():
                c = cnt_scr[0]
                slot = c % ns

                @pl.when(c >= ns)
                def _():
                    xe_copy(slot, e, win).wait()
                    gs_copy(slot, e, win).wait()

                stage_scr[slot] = acc_scr[e].astype(BF16)
                gstage_scr[slot] = jnp.broadcast_to(gacc_scr[e], (w, LANE))
                acc_scr[e] = jnp.zeros(acc_scr.shape[1:], F32)
                gacc_scr[e] = jnp.zeros(gacc_scr.shape[1:], F32)
                xe_copy(slot, e, win).start()
                gs_copy(slot, e, win).start()
                cnt_scr[0] = c + 1

            return carry

        lax.fori_loop(o0 // w, (o1 + w - 1) // w, visit, 0)

    @pl.when(b == nb - 1)
    def _():
        c = cnt_scr[0]
        for s in range(ns):
            @pl.when(c > s)
            def _(s=s):
                xe_copy(s, 0, 0).wait()
                gs_copy(s, 0, 0).wait()


def _dispatch(offs, h, pos, aff, *, cap):
    n, d = h.shape
    t = min(ROUTE_TOKENS, n)
    nb = n // t
    ne = N_EXPERTS
    w = SLOT_WINDOW
    grid_spec = pltpu.PrefetchScalarGridSpec(
        num_scalar_prefetch=1,
        grid=(nb,),
        in_specs=[pl.BlockSpec((t, d), lambda b, offs: (b, 0)),
                  pl.BlockSpec((ne, t), lambda b, offs: (0, b)),
                  pl.BlockSpec((ne, t), lambda b, offs: (0, b))],
        out_specs=[pl.BlockSpec(memory_space=pl.ANY), pl.BlockSpec(memory_space=pl.ANY)],
        scratch_shapes=[pltpu.VMEM((ne, w, d), F32), pltpu.VMEM((ne, w, 1), F32),
                        pltpu.VMEM((DISPATCH_STAGES, w, d), BF16), pltpu.VMEM((DISPATCH_STAGES, w, LANE), F32),
                        pltpu.SMEM((1,), jnp.int32),
                        pltpu.SemaphoreType.DMA((DISPATCH_STAGES,)), pltpu.SemaphoreType.DMA((DISPATCH_STAGES,))])
    return pl.pallas_call(
        functools.partial(_dispatch_kernel, nb=nb),
        grid_spec=grid_spec,
        out_shape=[jax.ShapeDtypeStruct((ne, cap, d), BF16), jax.ShapeDtypeStruct((ne, cap, LANE), F32)],
        compiler_params=_params(("arbitrary",)),
        name="route_dispatch",
    )(offs, h, pos, aff)


_TN = (((0,), (0,)), ((), ()))


def _combine_kernel(offs_ref, x_ref, pos_ref, ye_ref, o_ref, buf_scr, sem, *, nb, cap):
    b = pl.program_id(0)
    t = x_ref.shape[0]
    rows = buf_scr.shape[1] // 2
    al = ROW_ALIGN

    def first_row(e):
        return jnp.minimum((offs_ref[e * (nb + 1) + b] // al) * al, cap - rows)

    def fetch(e, start):
        return pltpu.make_async_copy(
            ye_ref.at[e, pl.ds(pl.multiple_of(start, al), rows), :],
            buf_scr.at[e // 2, pl.ds((e % 2) * rows, rows), :], sem.at[e])

    for e in range(N_EXPERTS):
        fetch(e, first_row(e)).start()
    o_ref[...] = x_ref[...]
    slot_iota = lax.broadcasted_iota(jnp.int32, (rows, t), 0)
    for pair in range(N_EXPERTS // 2):
        places = []
        for e in (2 * pair, 2 * pair + 1):
            start = first_row(e)
            fetch(e, start).wait()
            places.append(jnp.where(slot_iota == pos_ref[e:e + 1, :] - start, 1.0, 0.0).astype(BF16))
        place = jnp.concatenate(places, axis=0)
        o_ref[...] += lax.dot_general(place, buf_scr[pair], _TN, preferred_element_type=F32)

        for e in (2 * pair, 2 * pair + 1):
            start = first_row(e)
            o1 = offs_ref[e * (nb + 1) + b + 1]
            half = pl.ds((e % 2) * rows, rows)

            def more(j, carry, e=e, start=start, half=half):
                lo = start + j * rows
                ws = jnp.minimum(lo, cap - rows)
                cp = fetch(e, ws)
                cp.start()
                cp.wait()
                prow = pos_ref[e:e + 1, :]
                hit = jnp.logical_and(slot_iota == prow - ws, prow >= lo)
                o_ref[...] += lax.dot_general(jnp.where(hit, 1.0, 0.0).astype(BF16), buf_scr[pair, half, :],
                                              _TN, preferred_element_type=F32)
                return carry

            lax.fori_loop(1, (jnp.maximum(o1 - start, 1) + rows - 1) // rows, more, 0)


def _combine(offs, x, pos, ye):
    n, d = x.shape
    ne, cap, _ = ye.shape
    t = min(ROUTE_TOKENS, n)
    nb = n // t
    rows = min(COMBINE_ROWS, cap)
    grid_spec = pltpu.PrefetchScalarGridSpec(
        num_scalar_prefetch=1,
        grid=(nb,),
        in_specs=[pl.BlockSpec((t, d), lambda b, offs: (b, 0)),
                  pl.BlockSpec((ne, t), lambda b, offs: (0, b)),
                  pl.BlockSpec(memory_space=pl.ANY)],
        out_specs=pl.BlockSpec((t, d), lambda b, offs: (b, 0)),
        scratch_shapes=[pltpu.VMEM((ne // 2, 2 * rows, d), BF16), pltpu.SemaphoreType.DMA((ne,))])
    return pl.pallas_call(
        functools.partial(_combine_kernel, nb=nb, cap=cap),
        grid_spec=grid_spec,
        out_shape=jax.ShapeDtypeStruct((n, d), F32),
        compiler_params=_params(("arbitrary",)),
        name="route_combine",
    )(offs, x, pos, ye)


def _moe(x, hf, aff, lp):
    n, d = x.shape
    ne = N_EXPERTS
    cap = EC_CAPACITY_FACTOR * n // ne
    t = min(ROUTE_TOKENS, n)
    nb = n // t
    aff_t = aff[:, :ne].T
    sel = _select(aff_t, cap=cap)
    pos3, base3 = _slots(sel.reshape(ne, n // LANE, LANE))
    pos = pos3.reshape(ne, n)
    offs = jnp.concatenate([base3[:, ::t // LANE, 0], jnp.full((ne, 1), cap, jnp.int32)], axis=1)
    offs = offs.reshape(-1)
    xe, gs = _dispatch(offs, hf, pos, aff_t, cap=cap)
    ye = _expert_ffn(xe, lp["w_gate"], lp["w_up"], lp["w_down"], gs, tm=min(512, cap))
    return _combine(offs, x, pos, ye)


def _t5_bucket(rel):
    half = REL_BUCKETS // 2
    max_exact = half // 2
    n = jnp.abs(rel)
    base = jnp.where(rel > 0, half, 0)
    nf = jnp.maximum(n, 1).astype(F32)
    large = max_exact + (jnp.log(nf / max_exact) / math.log(REL_MAX_DIST / max_exact)
                         * (half - max_exact)).astype(jnp.int32)
    large = jnp.minimum(large, half - 1)
    return base + jnp.where(n < max_exact, n, large)


def _band_bias(rel_bias, group, dilation, tq):
    tk = tq + 2 * A_HALF
    rel = jnp.arange(tk)[None, :] - A_HALF - jnp.arange(tq)[:, None]
    heads = slice(group * A_HEADS_PER_GROUP, (group + 1) * A_HEADS_PER_GROUP)
    b = jnp.transpose(rel_bias[:, heads][_t5_bucket(rel * dilation)], (2, 0, 1)).astype(F32)
    return jnp.where((jnp.abs(rel) <= A_HALF)[None], b, NEG_INF)


def _block_diag_ones(size, block):
    idx = jnp.arange(size) // block
    return (idx[:, None] == idx[None, :]).astype(BF16)


def _rope_tables(seq):
    half = MLA_ROPE // 2
    freqs = ROPE_THETA ** (-jnp.arange(half, dtype=F32) / half)
    ang = jnp.arange(seq).astype(F32)[:, None] * freqs[None, :]
    cos, sin = jnp.cos(ang), jnp.sin(ang)
    ones = jnp.ones((seq, MLA_NOPE), F32)
    z64 = jnp.zeros((seq, MLA_NOPE), F32)
    z16 = jnp.zeros((seq, half), F32)
    z32 = jnp.zeros((seq, LANE - MLA_QK_DIM), F32)
    c = jnp.concatenate([ones, cos, cos, z32], axis=1)
    s1 = jnp.concatenate([z64, -sin, z16, z32], axis=1)
    s2 = jnp.concatenate([z64, z16, sin, z32], axis=1)
    return c, s1, s2


def _pad_heads(w, heads, width, lo, hi):
    w = w.reshape(w.shape[0], heads, width)[:, :, lo:hi]
    w = jnp.pad(w, ((0, 0), (0, 0), (0, MLA_HEAD_PAD - (hi - lo))))
    return w.reshape(w.shape[0], heads * MLA_HEAD_PAD)


def _prep_layer(p, l):
    d = D_MODEL
    row = lambda v: v.reshape(1, -1).astype(F32)
    w_in = p["w_in"][l]
    out = {}
    out["g_mix"] = row(p["norm_mix"][l])
    ng = len(A_GROUPS)
    out["w_a"] = w_in[:, :COL_CQ].reshape(d, 3, ng, A_GROUP_WIDTH).transpose(0, 2, 1, 3).reshape(
        d, COL_CQ).astype(BF16)
    out["aux_a"] = row(jnp.tile(jnp.concatenate([
        jnp.tile(p["a_q_norm"][l], A_HEADS_PER_GROUP) * (A_HEAD_DIM ** -0.5),
        jnp.tile(p["a_k_norm"][l], A_HEADS_PER_GROUP),
        jnp.ones((A_GROUP_WIDTH,), F32)]), ng))
    out["w_c"] = jnp.pad(w_in[:, COL_CQ:COL_GA], ((0, 0), (0, d - (COL_GA - COL_CQ)))).astype(BF16)
    out["w_g"] = w_in[:, COL_GA:].astype(BF16)
    out["g_ql"] = row(p["mla_q_lat_norm"][l])
    out["g_kl"] = row(p["mla_kv_lat_norm"][l])
    out["w_uq"] = _pad_heads(p["w_uq"][l], MLA_HEADS, MLA_QK_DIM, 0, MLA_QK_DIM).astype(BF16)
    w_ukv = p["w_ukv"][l]
    k_nope = _pad_heads(w_ukv, MLA_HEADS, MLA_NOPE + MLA_V, 0, MLA_NOPE)
    eye = jnp.pad(jnp.eye(MLA_ROPE, dtype=F32), ((0, 0), (MLA_NOPE, MLA_HEAD_PAD - MLA_QK_DIM)))
    k_pe = jnp.tile(eye, (1, MLA_HEADS))
    k_rows = d - MLA_Q_RANK - MLA_KV_RANK - MLA_ROPE
    out["w_uk"] = jnp.concatenate(
        [k_nope, k_pe, jnp.zeros((k_rows, MLA_HEADS * MLA_HEAD_PAD), F32)], axis=0).astype(BF16)
    out["w_uv"] = w_ukv.reshape(MLA_KV_RANK, MLA_HEADS, MLA_NOPE + MLA_V)[:, :, MLA_NOPE:].reshape(
        MLA_KV_RANK, MLA_HEADS * MLA_V).astype(BF16)
    pad_gain = lambda g: jnp.tile(jnp.pad(g, (0, MLA_HEAD_PAD - MLA_QK_DIM)), MLA_HEADS)
    out["g_q"] = row(pad_gain(p["mla_q_norm"][l]) * (MLA_QK_DIM ** -0.5 * math.log2(math.e)))
    out["g_k"] = row(pad_gain(p["mla_k_norm"][l]))
    out["w_pa"] = p["w_proj_a"][l].astype(BF16)
    out["w_pb"] = p["w_proj_b"][l].astype(BF16)
    out["w_out"] = p["w_out"][l].astype(BF16)
    out["g_mx"] = row(p["norm_mem_x"][l])
    out["g_mkv"] = row(p["norm_mem_kv"][l])
    out["w_mq"] = p["w_mq"][l].astype(BF16)
    out["w_mkv"] = p["w_mkv"][l].astype(BF16)
    out["g_mq"] = row(p["mem_q_norm"][l] * (MEM_HEAD_DIM ** -0.5))
    out["g_mk"] = row(p["mem_k_norm"][l])
    out["w_mo"] = p["w_mo"][l].astype(BF16)
    out["g_ffn"] = row(p["norm_ffn"][l])
    out["w_router"] = jnp.pad(p["w_router"][l], ((0, 0), (0, LANE - N_EXPERTS))).astype(BF16)
    out["w_gate"] = p["w_gate"][l].astype(BF16)
    out["w_up"] = p["w_up"][l].astype(BF16)
    out["w_down"] = p["w_down"][l].astype(BF16)
    return out


BAND_TQ = 128
BAND_STEP_ROWS = 256
MLA_ROW_BLOCK = 32


def _layer(x, mem, lp, shared, *, batch, seq):
    n = batch * seq
    mem_tokens = mem.shape[0] // batch
    tm = min(1024, n)
    bd64, bd128 = shared["bd64"], shared["bd128"]
    zgs = _norm_proj_heads(x, lp["g_mix"], lp["w_a"], lp["aux_a"], bd64, tm=min(512, n))
    zc = _norm_proj(x, lp["g_mix"], lp["w_c"], mode="plain", tm=tm, tn=D_MODEL, out_dtype=F32)
    gates = _norm_proj(x, lp["g_mix"], lp["w_g"], mode="sigmoid", tm=tm, tn=D_MODEL, out_dtype=BF16)
    oas, lses = [], []
    for g, (_, dil) in enumerate(A_GROUPS):
        o, lse = _band_attention(zgs[g], shared["band_bias"][g], shared["hmask"], batch=batch, seq=seq,
                                 group=g, dilation=dil, tq=BAND_TQ)
        oas.append(o)
        lses.append(lse)
    qm, km, vm = _mla_prep(zc, lp["g_ql"], lp["g_kl"], lp["w_uq"], lp["w_uk"], lp["w_uv"],
                           lp["g_q"], lp["g_k"], shared["rope_c"], shared["rope_s1"],
                           shared["rope_s2"], bd128, seq=seq, tm=min(512, seq))
    ob = _mla_attention(qm, km, vm, batch=batch, seq=seq, tq=min(512, seq), tk=min(512, seq))
    x = _merge(oas, lses, ob, gates, x, lp["w_pa"], lp["w_pb"], lp["w_out"], tm=min(512, n))
    kmem, vmem = _mem_kv(mem, lp["g_mkv"], lp["w_mkv"], lp["g_mk"], tm=min(512, mem.shape[0]))
    x, hf, aff = _mem_attention(x, kmem, vmem, lp["g_mx"], lp["w_mq"], lp["g_mq"], lp["w_mo"],
                                lp["g_ffn"], lp["w_router"], batch=batch, seq=seq,
                                mem_tokens=mem_tokens, tm=min(512, seq))
    return _moe(x, hf, aff, lp)


def _trunk(x, mem, layers, shared):
    batch, seq, d = x.shape
    xf = x.reshape(batch * seq, d)
    memf = mem.reshape(-1, d)
    for lp in layers:
        xf = _layer(xf, memf, lp, shared, batch=batch, seq=seq)
    return xf.reshape(batch, seq, d)


def kernel(x_prompt, x_sample, mem_prompt, mem_sample, norm_mix, w_in, a_q_norm, a_k_norm, rel_bias,
           mla_q_lat_norm, w_uq, mla_kv_lat_norm, w_ukv, mla_q_norm, mla_k_norm, w_proj_a, w_proj_b,
           w_out, norm_mem_x, norm_mem_kv, w_mq, w_mkv, mem_q_norm, mem_k_norm, w_mo, norm_ffn,
           w_router, w_gate, w_up, w_down):
    p = dict(norm_mix=norm_mix, w_in=w_in, a_q_norm=a_q_norm, a_k_norm=a_k_norm,
             mla_q_lat_norm=mla_q_lat_norm, w_uq=w_uq, mla_kv_lat_norm=mla_kv_lat_norm, w_ukv=w_ukv,
             mla_q_norm=mla_q_norm, mla_k_norm=mla_k_norm, w_proj_a=w_proj_a, w_proj_b=w_proj_b,
             w_out=w_out, norm_mem_x=norm_mem_x, norm_mem_kv=norm_mem_kv, w_mq=w_mq, w_mkv=w_mkv,
             mem_q_norm=mem_q_norm, mem_k_norm=mem_k_norm, w_mo=w_mo, norm_ffn=norm_ffn,
             w_router=w_router, w_gate=w_gate, w_up=w_up, w_down=w_down)
    layers = [_prep_layer(p, l) for l in range(DEPTH)]
    outs = []
    for x, mem in ((x_prompt, mem_prompt), (x_sample, mem_sample)):
        seq = x.shape[1]
        c, s1, s2 = _rope_tables(seq)
        hm = (jnp.arange(LANE)[None, :] // A_HEAD_DIM == jnp.arange(16)[:, None]).astype(BF16)
        shared = dict(
            bd64=_block_diag_ones(256, A_HEAD_DIM), bd128=_block_diag_ones(256, MLA_HEAD_PAD),
            band_bias=[_band_bias(rel_bias, g, dil, BAND_TQ) for g, (_, dil) in enumerate(A_GROUPS)],
            hmask=hm, rope_c=c, rope_s1=s1, rope_s2=s2)
        outs.append(_trunk(x, mem, layers, shared))
    return tuple(outs)
```

```python
import functools
import math

import jax
import jax.numpy as jnp
from jax import lax
from jax.experimental import pallas as pl
from jax.experimental.pallas import tpu as pltpu

F32 = jnp.float32
BF16 = jnp.bfloat16

D_MODEL = 1024
DEPTH = 2
A_GROUPS = ((128, 1), (512, 4), (2048, 16))
A_HEADS_PER_GROUP = 8
A_HEAD_DIM = 64
A_GROUP_WIDTH = A_HEADS_PER_GROUP * A_HEAD_DIM
A_WIDTH = A_GROUP_WIDTH * len(A_GROUPS)
A_HALF = 64
MLA_HEADS = 8
MLA_NOPE = 64
MLA_ROPE = 32
MLA_QK_DIM = MLA_NOPE + MLA_ROPE
MLA_V = 64
MLA_Q_RANK = 512
MLA_KV_RANK = 256
MLA_HEAD_PAD = 128
ROPE_THETA = 10000.0
MEM_HEADS = 4
MEM_HEAD_DIM = 128
MEM_WIDTH = MEM_HEADS * MEM_HEAD_DIM
REL_BUCKETS = 32
REL_MAX_DIST = 1024
N_EXPERTS = 16
EC_CAPACITY_FACTOR = 2
RMS_EPS = 1e-6
NEG_INF = -1e30
COL_CQ = 3 * A_WIDTH
COL_CKV = COL_CQ + MLA_Q_RANK
COL_GA = COL_CKV + MLA_KV_RANK + MLA_ROPE
IN_COLS = COL_GA + 2 * D_MODEL

LANE = 128
MXU_COLS = 256
VMEM_LIMIT = 48 * 1024 * 1024

_NT = (((1,), (1,)), ((), ()))


def _rms(x, g):
    ms = jnp.mean(x * x, axis=-1, keepdims=True)
    return x * lax.rsqrt(ms + RMS_EPS) * g


def _params(sem):
    return pltpu.CompilerParams(dimension_semantics=sem, vmem_limit_bytes=VMEM_LIMIT)


def _proj_kernel(x_ref, g_ref, w_ref, o_ref, h_scr, *, mode):
    j = pl.program_id(1)

    @pl.when(j == 0)
    def _():
        h_scr[...] = _rms(x_ref[...], g_ref[...]).astype(BF16)

    z = jnp.dot(h_scr[...], w_ref[...], preferred_element_type=F32)
    if mode == "sigmoid":
        z = jax.nn.sigmoid(z)
    o_ref[...] = z.astype(o_ref.dtype)


def _proj_heads_kernel(x_ref, g_ref, w_ref, aux_ref, bd_ref, o0_ref, o1_ref, o2_ref, hf_scr, hp_scr):
    j = pl.program_id(1)
    tm = x_ref.shape[0]
    w = 3 * A_GROUP_WIDTH

    @pl.when(j == 0)
    def _():
        h = _rms(x_ref[...], g_ref[...])
        for c in range(h.shape[1] // LANE):
            hf_scr[c] = h[:, c * LANE:(c + 1) * LANE]
        for g, (_, dil) in enumerate(A_GROUPS):
            if dil == 1:
                hp_scr[g] = h.astype(BF16)
                continue
            rows = tm // dil
            for r in range(dil):
                for c in range(h.shape[1] // LANE):
                    hp_scr[g, r * rows:(r + 1) * rows, c * LANE:(c + 1) * LANE] = (
                        hf_scr[c, pl.ds(r, rows, stride=dil), :].astype(BF16))

    z = jnp.dot(hp_scr[j], w_ref[...], preferred_element_type=F32)
    cols = []
    for c in range(2 * A_GROUP_WIDTH // MXU_COLS):
        sl = slice(c * MXU_COLS, (c + 1) * MXU_COLS)
        zc = z[:, sl]
        ss = jnp.dot((zc * zc).astype(BF16), bd_ref[...], preferred_element_type=F32)
        cols.append((zc * lax.rsqrt(ss * (1.0 / A_HEAD_DIM) + RMS_EPS) * aux_ref[:, sl]).astype(BF16))
    cols.append(z[:, 2 * A_GROUP_WIDTH:].astype(BF16))
    zn = jnp.concatenate(cols, axis=1)
    for g, ((_, dil), o_ref) in enumerate(zip(A_GROUPS, (o0_ref, o1_ref, o2_ref))):
        @pl.when(j == g)
        def _(dil=dil, o_ref=o_ref):
            rows = tm // dil
            for r in range(dil):
                o_ref[:, r * w:(r + 1) * w] = zn[r * rows:(r + 1) * rows, :]


def _norm_proj_heads(x, g, w, aux, bd, *, tm):
    n, d = x.shape
    wg = 3 * A_GROUP_WIDTH
    ng = len(A_GROUPS)
    return pl.pallas_call(
        _proj_heads_kernel,
        grid=(n // tm, ng),
        in_specs=[
            pl.BlockSpec((tm, d), lambda i, j: (i, 0)),
            pl.BlockSpec((1, d), lambda i, j: (0, 0)),
            pl.BlockSpec((d, wg), lambda i, j: (0, j)),
            pl.BlockSpec((1, wg), lambda i, j: (0, j)),
            pl.BlockSpec(bd.shape, lambda i, j: (0, 0)),
        ],
        out_specs=[pl.BlockSpec((tm // dil, dil * wg), lambda i, j: (i, 0)) for _, dil in A_GROUPS],
        out_shape=[jax.ShapeDtypeStruct((n // dil, dil * wg), BF16) for _, dil in A_GROUPS],
        scratch_shapes=[pltpu.VMEM((d // LANE, tm, LANE), F32), pltpu.VMEM((ng, tm, d), BF16)],
        compiler_params=_params(("parallel", "arbitrary")),
        name="norm_proj_heads",
    )(x, g, w, aux, bd)


def _norm_proj(x, g, w, *, mode, tm, tn, out_dtype):
    n, d = x.shape
    ncols = w.shape[1]
    return pl.pallas_call(
        functools.partial(_proj_kernel, mode=mode),
        grid=(n // tm, ncols // tn),
        in_specs=[
            pl.BlockSpec((tm, d), lambda i, j: (i, 0)),
            pl.BlockSpec((1, d), lambda i, j: (0, 0)),
            pl.BlockSpec((d, tn), lambda i, j: (0, j)),
        ],
        out_specs=pl.BlockSpec((tm, tn), lambda i, j: (i, j)),
        out_shape=jax.ShapeDtypeStruct((n, ncols), out_dtype),
        scratch_shapes=[pltpu.VMEM((tm, d), BF16)],
        compiler_params=_params(("parallel", "arbitrary")),
        name="norm_proj_" + mode,
    )(x, g, w)


def _band_attn_kernel(q_ref, kl_ref, km_ref, kh_ref, vl_ref, vm_ref, vh_ref, bias_ref, hm_ref,
                      o_ref, lse_ref, *, tq, seq_len):
    i = pl.program_id(2)
    tk = tq + 2 * A_HALF
    nsub = q_ref.shape[0] // tq
    k_all = jnp.concatenate([kl_ref[...], km_ref[...], kh_ref[...]], axis=0)
    v_all = jnp.concatenate([vl_ref[...], vm_ref[...], vh_ref[...]], axis=0)
    first = lax.broadcasted_iota(jnp.int32, (tq, LANE), 1) < A_HEAD_DIM
    for sub in range(nsub):
        q = q_ref[sub * tq:(sub + 1) * tq, :]
        k = k_all[sub * tq:sub * tq + tk, :]
        v = v_all[sub * tq:sub * tq + tk, :]
        kpos = (i * nsub + sub) * tq - A_HALF + lax.broadcasted_iota(jnp.int32, (1, tk), 1)
        valid = jnp.logical_and(kpos >= 0, kpos < seq_len)
        o_parts, lse_parts = [], []
        for j in range(A_GROUP_WIDTH // LANE):
            sl = slice(j * LANE, (j + 1) * LANE)
            qp, kp, vp = q[:, sl], k[:, sl], v[:, sl]
            res = []
            for hh in range(2):
                qm = qp * hm_ref[hh:hh + 1, :]
                s = lax.dot_general(qm, kp, _NT, preferred_element_type=F32)
                s = jnp.where(valid, s + bias_ref[2 * j + hh], NEG_INF)
                m = jnp.max(s, axis=-1, keepdims=True)
                p = jnp.exp(s - m)
                l = jnp.sum(p, axis=-1, keepdims=True)
                o = jnp.dot(p.astype(BF16), vp, preferred_element_type=F32) / l
                res.append((o, m + jnp.log(l)))
            o_parts.append(jnp.where(first, res[0][0], res[1][0]))
            lse_parts.append(jnp.where(first, res[0][1], res[1][1]))
        o_ref[sub * tq:(sub + 1) * tq, :] = jnp.concatenate(o_parts, axis=1).astype(o_ref.dtype)
        lse_ref[sub * tq:(sub + 1) * tq, :] = jnp.concatenate(lse_parts, axis=1)


def _band_attention(zg, bias, hmask, *, batch, seq, group, dilation, tq):
    sl = seq // dilation
    zv = zg.reshape(batch, sl, dilation * 3 * A_GROUP_WIDTH)
    step = min(BAND_STEP_ROWS, sl)
    hb = step // A_HALF
    nhb = sl // A_HALF
    w = A_GROUP_WIDTH

    def main(off):
        return pl.BlockSpec((None, step, w), lambda b, r, i: (b, i, r * 3 + off))

    def lo(off):
        return pl.BlockSpec((None, A_HALF, w),
                            lambda b, r, i: (b, jnp.maximum(i * hb - 1, 0), r * 3 + off))

    def hi(off):
        return pl.BlockSpec((None, A_HALF, w),
                            lambda b, r, i: (b, jnp.minimum((i + 1) * hb, nhb - 1), r * 3 + off))

    out_spec = pl.BlockSpec((None, step, w), lambda b, r, i: (b, i, r))
    o, lse = pl.pallas_call(
        functools.partial(_band_attn_kernel, tq=tq, seq_len=sl),
        grid=(batch, dilation, sl // step),
        in_specs=[main(0), lo(1), main(1), hi(1), lo(2), main(2), hi(2),
                  pl.BlockSpec(bias.shape, lambda b, r, i: (0, 0, 0)),
                  pl.BlockSpec(hmask.shape, lambda b, r, i: (0, 0))],
        out_specs=[out_spec, out_spec],
        out_shape=[jax.ShapeDtypeStruct((batch, sl, dilation * w), BF16),
                   jax.ShapeDtypeStruct((batch, sl, dilation * w), F32)],
        compiler_params=_params(("parallel", "parallel", "parallel")),
        name=f"band_attn_g{group}",
    )(zv, zv, zv, zv, zv, zv, zv, bias, hmask)
    return o.reshape(batch * sl, dilation * w), lse.reshape(batch * sl, dilation * w)


def _mla_prep_kernel(zc_ref, gql_ref, gkl_ref, wq_ref, wk_ref, wv_ref, gq_ref, gk_ref,
                     c_ref, s1_ref, s2_ref, bd_ref, q_ref, k_ref, v_ref):
    zc = zc_ref[...]
    cq = _rms(zc[:, :MLA_Q_RANK], gql_ref[...]).astype(BF16)
    ckv = _rms(zc[:, MLA_Q_RANK:MLA_Q_RANK + MLA_KV_RANK], gkl_ref[...])
    kin = jnp.concatenate([ckv, zc[:, MLA_Q_RANK + MLA_KV_RANK:]], axis=1).astype(BF16)
    q = jnp.dot(cq, wq_ref[...], preferred_element_type=F32)
    k = jnp.dot(kin, wk_ref[...], preferred_element_type=F32)
    v = jnp.dot(ckv.astype(BF16), wv_ref[...], preferred_element_type=F32)
    first = lax.broadcasted_iota(jnp.int32, (v.shape[0], LANE), 1) < MLA_V
    for j in range(MLA_HEADS // 2):
        vp = v[:, j * LANE:(j + 1) * LANE]
        v_ref[:, 2 * j * LANE:(2 * j + 1) * LANE] = jnp.where(first, vp, 1.0).astype(v_ref.dtype)
        v_ref[:, (2 * j + 1) * LANE:(2 * j + 2) * LANE] = jnp.where(first, 1.0, vp).astype(v_ref.dtype)
    cos, s1, s2 = c_ref[...], s1_ref[...], s2_ref[...]
    for src, g_ref, dst in ((q, gq_ref, q_ref), (k, gk_ref, k_ref)):
        for c in range(src.shape[1] // 256):
            sl = slice(c * 256, (c + 1) * 256)
            xc = src[:, sl]
            ss = jnp.dot((xc * xc).astype(BF16), bd_ref[...], preferred_element_type=F32)
            xn = xc * lax.rsqrt(ss * (1.0 / MLA_QK_DIM) + RMS_EPS) * g_ref[:, sl]
            for hh in range(2):
                xh = xn[:, hh * LANE:(hh + 1) * LANE]
                out = (xh * cos + pltpu.roll(xh, LANE - MLA_ROPE // 2, 1) * s1
                       + pltpu.roll(xh, MLA_ROPE // 2, 1) * s2)
                lo = c * 256 + hh * LANE
                dst[:, lo:lo + LANE] = out.astype(dst.dtype)


def _mla_prep(zc, gql, gkl, wq, wk, wv, gq, gk, cos, s1, s2, bd, *, seq, tm):
    n = zc.shape[0]
    hp = MLA_HEADS * MLA_HEAD_PAD
    vw = MLA_HEADS * MLA_HEAD_PAD
    nsb = seq // tm
    full = lambda a: pl.BlockSpec(a.shape, lambda i: (0,) * a.ndim)
    tab = pl.BlockSpec((tm, LANE), lambda i: (i % nsb, 0))
    return pl.pallas_call(
        _mla_prep_kernel,
        grid=(n // tm,),
        in_specs=[pl.BlockSpec((tm, zc.shape[1]), lambda i: (i, 0)),
                  full(gql), full(gkl), full(wq), full(wk), full(wv), full(gq), full(gk),
                  tab, tab, tab, full(bd)],
        out_specs=[pl.BlockSpec((tm, hp), lambda i: (i, 0)),
                   pl.BlockSpec((tm, hp), lambda i: (i, 0)),
                   pl.BlockSpec((tm, vw), lambda i: (i, 0))],
        out_shape=[jax.ShapeDtypeStruct((n, hp), BF16),
                   jax.ShapeDtypeStruct((n, hp), BF16),
                   jax.ShapeDtypeStruct((n, vw), BF16)],
        compiler_params=_params(("parallel",)),
        name="mla_prep",
    )(zc, gql, gkl, wq, wk, wv, gq, gk, cos, s1, s2, bd)


def _mla_attn_kernel(q_ref, k_ref, v_ref, o_ref, sa_scr, sb_scr, pa_scr, pb_scr, aa_scr, ab_scr,
                     m_scr, acc_scr, *, tk, rb):
    tq = q_ref.shape[0]
    nk = k_ref.shape[0] // tk
    hs = [slice(hh * MLA_HEAD_PAD, (hh + 1) * MLA_HEAD_PAD) for hh in range(2)]

    def rows_of(c):
        return pl.ds(pl.multiple_of(c * tk, tk), tk)

    def scores(c, dst):
        for hh in range(2):
            dst[hh] = lax.dot_general(q_ref[:, hs[hh]], k_ref[rows_of(c), hs[hh]], _NT,
                                      preferred_element_type=F32)

    def softmax(src, p_dst, a_dst):
        for hh in range(2):
            for r in range(tq // rb):
                rs = slice(r * rb, (r + 1) * rb)
                s = src[hh, rs, :]
                m_old = m_scr[hh, rs, :]
                m_new = jnp.maximum(m_old, jnp.max(s, axis=-1, keepdims=True))
                p_dst[hh, rs, :] = jnp.exp2(s - m_new).astype(BF16)
                m_scr[hh, rs, :] = m_new
                a_dst[hh, rs, :] = jnp.exp2(m_old - m_new)

    def values(c, p_src, a_src):
        for hh in range(2):
            acc_scr[hh] = a_src[hh] * acc_scr[hh] + jnp.dot(p_src[hh], v_ref[rows_of(c), hs[hh]],
                                                            preferred_element_type=F32)

    m_scr[...] = jnp.full(m_scr.shape, NEG_INF, F32)
    acc_scr[...] = jnp.zeros(acc_scr.shape, F32)
    scores(0, sa_scr)
    softmax(sa_scr, pa_scr, aa_scr)
    scores(1, sb_scr)

    def body(i, carry):
        values(2 * i, pa_scr, aa_scr)
        softmax(sb_scr, pb_scr, ab_scr)
        scores(2 * i + 2, sa_scr)
        values(2 * i + 1, pb_scr, ab_scr)
        softmax(sa_scr, pa_scr, aa_scr)
        scores(2 * i + 3, sb_scr)
        return carry

    lax.fori_loop(0, nk // 2 - 1, body, 0)
    values(nk - 2, pa_scr, aa_scr)
    softmax(sb_scr, pb_scr, ab_scr)
    values(nk - 1, pb_scr, ab_scr)
    acc0, acc1 = acc_scr[0], acc_scr[1]
    o0 = acc0 / pltpu.roll(acc0, MLA_V, 1)
    o1 = acc1 / pltpu.roll(acc1, MLA_V, 1)
    first = lax.broadcasted_iota(jnp.int32, (tq, LANE), 1) < MLA_V
    o_ref[...] = jnp.where(first, o0, o1).astype(o_ref.dtype)


def _mla_attention(q, k, v, *, batch, seq, tq, tk):
    n = batch * seq
    nqb = seq // tq
    assert (seq // tk) % 2 == 0
    return pl.pallas_call(
        functools.partial(_mla_attn_kernel, tk=tk, rb=MLA_ROW_BLOCK),
        grid=(batch, MLA_HEADS // 2, nqb),
        in_specs=[pl.BlockSpec((tq, 2 * MLA_HEAD_PAD), lambda b, h, i: (b * nqb + i, h)),
                  pl.BlockSpec((seq, 2 * MLA_HEAD_PAD), lambda b, h, i: (b, h)),
                  pl.BlockSpec((seq, 2 * MLA_HEAD_PAD), lambda b, h, i: (b, h))],
        out_specs=pl.BlockSpec((tq, 2 * MLA_V), lambda b, h, i: (b * nqb + i, h)),
        out_shape=jax.ShapeDtypeStruct((n, MLA_HEADS * MLA_V), BF16),
        scratch_shapes=[pltpu.VMEM((2, tq, tk), F32), pltpu.VMEM((2, tq, tk), F32),
                        pltpu.VMEM((2, tq, tk), BF16), pltpu.VMEM((2, tq, tk), BF16),
                        pltpu.VMEM((2, tq, 1), F32), pltpu.VMEM((2, tq, 1), F32),
                        pltpu.VMEM((2, tq, 1), F32), pltpu.VMEM((2, tq, LANE), F32)],
        compiler_params=_params(("parallel", "parallel", "arbitrary")),
        name="mla_attn",
    )(q, k, v)


def _merge_kernel(o0_ref, l0_ref, o1_ref, l1_ref, o2_ref, l2_ref, ob_ref, g_ref, x_ref,
                  wa_ref, wb_ref, wo_ref, xo_ref, nat_scr):
    tm = x_ref.shape[0]
    w = A_GROUP_WIDTH

    def natural(ref, slot, dil):
        if dil == 1:
            return ref[...].astype(F32)
        rows = tm // dil
        nc = w // LANE
        for r in range(dil):
            for c in range(nc):
                lo = r * w + c * LANE
                nat_scr[slot * nc + c, pl.ds(r, rows, stride=dil), :] = ref[:, lo:lo + LANE].astype(F32)
        return jnp.concatenate([nat_scr[slot * nc + c] for c in range(nc)], axis=1)

    dils = [dil for _, dil in A_GROUPS]
    l0, l1, l2 = (natural(ref, 2 * g, dils[g]) for g, ref in enumerate((l0_ref, l1_ref, l2_ref)))
    o0, o1, o2 = (natural(ref, 2 * g + 1, dils[g]) for g, ref in enumerate((o0_ref, o1_ref, o2_ref)))
    m = jnp.maximum(jnp.maximum(l0, l1), l2)
    e0, e1, e2 = jnp.exp(l0 - m), jnp.exp(l1 - m), jnp.exp(l2 - m)
    oa = (e0 * o0 + e1 * o1 + e2 * o2) / (e0 + e1 + e2)
    pa = jnp.dot(oa.astype(BF16), wa_ref[...], preferred_element_type=F32)
    pb = jnp.dot(ob_ref[...], wb_ref[...], preferred_element_type=F32)
    d = pa.shape[1]
    merged = g_ref[:, :d].astype(F32) * pa + g_ref[:, d:].astype(F32) * pb
    xo_ref[...] = x_ref[...] + jnp.dot(merged.astype(BF16), wo_ref[...], preferred_element_type=F32)


def _merge(oas, lses, ob, gates, x, wa, wb, wo, *, tm):
    n, d = x.shape
    row = lambda a: pl.BlockSpec((tm * a.shape[0] // n, a.shape[1]), lambda i: (i, 0))
    full = lambda a: pl.BlockSpec(a.shape, lambda i: (0, 0))
    args = [oas[0], lses[0], oas[1], lses[1], oas[2], lses[2], ob, gates, x]
    return pl.pallas_call(
        _merge_kernel,
        grid=(n // tm,),
        in_specs=[row(a) for a in args] + [full(wa), full(wb), full(wo)],
        out_specs=pl.BlockSpec((tm, d), lambda i: (i, 0)),
        out_shape=jax.ShapeDtypeStruct((n, d), F32),
        scratch_shapes=[pltpu.VMEM((2 * len(A_GROUPS) * A_GROUP_WIDTH // LANE, tm, LANE), F32)],
        compiler_params=_params(("parallel",)),
        name="merge",
    )(*args, wa, wb, wo)


def _mem_kv_kernel(mem_ref, g_ref, w_ref, gk_ref, k_ref, v_ref):
    mh = _rms(mem_ref[...], g_ref[...]).astype(BF16)
    kv = jnp.dot(mh, w_ref[...], preferred_element_type=F32)
    for h in range(MEM_HEADS):
        sl = slice(h * MEM_HEAD_DIM, (h + 1) * MEM_HEAD_DIM)
        k_ref[:, sl] = _rms(kv[:, sl], gk_ref[...]).astype(k_ref.dtype)
    v_ref[...] = kv[:, MEM_WIDTH:].astype(v_ref.dtype)


def _mem_kv(mem, g, w, gk, *, tm):
    n = mem.shape[0]
    full = lambda a: pl.BlockSpec(a.shape, lambda i: (0, 0))
    return pl.pallas_call(
        _mem_kv_kernel,
        grid=(n // tm,),
        in_specs=[pl.BlockSpec((tm, mem.shape[1]), lambda i: (i, 0)), full(g), full(w), full(gk)],
        out_specs=[pl.BlockSpec((tm, MEM_WIDTH), lambda i: (i, 0))] * 2,
        out_shape=[jax.ShapeDtypeStruct((n, MEM_WIDTH), BF16)] * 2,
        compiler_params=_params(("parallel",)),
        name="mem_kv",
    )(mem, g, w, gk)


def _mem_attn_kernel(x_ref, k_ref, v_ref, gx_ref, wq_ref, gq_ref, wo_ref, gf_ref, wr_ref,
                     xo_ref, hf_ref, aff_ref):
    x = x_ref[...]
    hx = _rms(x, gx_ref[...]).astype(BF16)
    q = jnp.dot(hx, wq_ref[...], preferred_element_type=F32)
    outs = []
    for h in range(MEM_HEADS):
        sl = slice(h * MEM_HEAD_DIM, (h + 1) * MEM_HEAD_DIM)
        qn = _rms(q[:, sl], gq_ref[...]).astype(BF16)
        s = lax.dot_general(qn, k_ref[:, sl], _NT, preferred_element_type=F32)
        m = jnp.max(s, axis=-1, keepdims=True)
        p = jnp.exp(s - m)
        l = jnp.sum(p, axis=-1, keepdims=True)
        outs.append(jnp.dot(p.astype(BF16), v_ref[:, sl], preferred_element_type=F32) / l)
    o = jnp.concatenate(outs, axis=1).astype(BF16)
    x2 = x + jnp.dot(o, wo_ref[...], preferred_element_type=F32)
    xo_ref[...] = x2
    hf = _rms(x2, gf_ref[...]).astype(BF16)
    hf_ref[...] = hf
    logits = jnp.dot(hf, wr_ref[...], preferred_element_type=F32)
    lane = lax.broadcasted_iota(jnp.int32, logits.shape, 1)
    logits = jnp.where(lane < N_EXPERTS, logits, NEG_INF)
    e = jnp.exp(logits - jnp.max(logits, axis=-1, keepdims=True))
    aff_ref[...] = e / jnp.sum(e, axis=-1, keepdims=True)


def _mem_attention(x, kmem, vmem, gx, wq, gq, wo, gf, wr, *, batch, seq, mem_tokens, tm):
    n, d = x.shape
    nsb = seq // tm
    full = lambda a: pl.BlockSpec(a.shape, lambda b, i: (0, 0))
    row = lambda w: pl.BlockSpec((tm, w), lambda b, i: (b * nsb + i, 0))
    kv = pl.BlockSpec((mem_tokens, MEM_WIDTH), lambda b, i: (b, 0))
    return pl.pallas_call(
        _mem_attn_kernel,
        grid=(batch, nsb),
        in_specs=[row(d), kv, kv, full(gx), full(wq), full(gq), full(wo), full(gf), full(wr)],
        out_specs=[row(d), row(d), row(LANE)],
        out_shape=[jax.ShapeDtypeStruct((n, d), F32),
                   jax.ShapeDtypeStruct((n, d), BF16),
                   jax.ShapeDtypeStruct((n, LANE), F32)],
        compiler_params=_params(("parallel", "parallel")),
        name="mem_attn",
    )(x, kmem, vmem, gx, wq, gq, wo, gf, wr)


def _ffn_kernel(x_ref, wg_ref, wu_ref, wd_ref, g_ref, y_ref):
    x = x_ref[...]
    a = jnp.dot(x, wg_ref[...], preferred_element_type=F32)
    b = jnp.dot(x, wu_ref[...], preferred_element_type=F32)
    hid = (a * jax.nn.sigmoid(a) * b).astype(BF16)
    y = jnp.dot(hid, wd_ref[...], preferred_element_type=F32)
    g = g_ref[...]
    y_ref[...] = (y * jnp.concatenate([g] * (y.shape[1] // LANE), axis=1)).astype(y_ref.dtype)


def _expert_ffn(xe, wg, wu, wd, gates, *, tm):
    e, cap, d = xe.shape
    f = wg.shape[2]
    return pl.pallas_call(
        _ffn_kernel,
        grid=(e, cap // tm),
        in_specs=[pl.BlockSpec((None, tm, d), lambda e, i: (e, i, 0)),
                  pl.BlockSpec((None, d, f), lambda e, i: (e, 0, 0)),
                  pl.BlockSpec((None, d, f), lambda e, i: (e, 0, 0)),
                  pl.BlockSpec((None, f, d), lambda e, i: (e, 0, 0)),
                  pl.BlockSpec((None, tm, LANE), lambda e, i: (e, i, 0))],
        out_specs=pl.BlockSpec((None, tm, d), lambda e, i: (e, i, 0)),
        out_shape=jax.ShapeDtypeStruct((e, cap, d), BF16),
        compiler_params=_params(("parallel", "arbitrary")),
        name="expert_ffn",
    )(xe, wg, wu, wd, gates)


SLOT_WINDOW = 128
ROUTE_TOKENS = 512
COMBINE_ROWS = 128
ROW_ALIGN = 16
DISPATCH_STAGES = 4


def _select_kernel(aff_ref, sel_ref, *, cap):
    bits = pltpu.bitcast(aff_ref[...], jnp.int32)
    ne, n = bits.shape
    count = lambda mask: jnp.sum(jnp.where(mask, 1, 0), axis=1, keepdims=True)

    def value_bit(i, t):
        cand = t | jnp.left_shift(1, 30 - i)
        return jnp.where(count(bits >= cand) >= cap, cand, t)

    thr = lax.fori_loop(0, 31, value_bit, jnp.zeros((ne, 1), jnp.int32))
    gt = bits > thr
    eq = bits == thr
    need = cap - count(gt)
    idx = lax.broadcasted_iota(jnp.int32, (ne, n), 1)
    nbits = max(1, (n - 1).bit_length())

    def index_bit(i, j):
        cand = j | jnp.left_shift(1, nbits - 1 - i)
        return jnp.where(count(jnp.logical_and(eq, idx < cand)) < need, cand, j)

    last = lax.fori_loop(0, nbits, index_bit, jnp.zeros((ne, 1), jnp.int32))
    sel = jnp.logical_or(gt, jnp.logical_and(eq, idx <= last))
    sel_ref[...] = jnp.where(sel, 1.0, 0.0).astype(sel_ref.dtype)


def _select(aff_t, *, cap):
    return pl.pallas_call(
        functools.partial(_select_kernel, cap=cap),
        out_shape=jax.ShapeDtypeStruct(aff_t.shape, BF16),
        compiler_params=pltpu.CompilerParams(vmem_limit_bytes=VMEM_LIMIT),
        name="route_select",
    )(aff_t)


def _slots_kernel(sel_ref, upper_ref, lower_ref, pos_ref, base_ref):
    s = sel_ref[...]
    incl = jnp.dot(s, upper_ref[...], preferred_element_type=F32)
    tot = jnp.broadcast_to(incl[:, LANE - 1:LANE], incl.shape).astype(BF16)
    base = jnp.dot(lower_ref[...], tot, preferred_element_type=F32)
    pos_ref[...] = jnp.where(s > 0, base + incl - 1.0, -1.0).astype(jnp.int32)
    base_ref[...] = base.astype(jnp.int32)


def _slots(sel3):
    ne, nt, _ = sel3.shape
    upper = (jnp.arange(LANE)[:, None] <= jnp.arange(LANE)[None, :]).astype(BF16)
    lower = (jnp.arange(nt)[None, :] < jnp.arange(nt)[:, None]).astype(BF16)
    blk = pl.BlockSpec((None, nt, LANE), lambda e: (e, 0, 0))
    return pl.pallas_call(
        _slots_kernel,
        grid=(ne,),
        in_specs=[blk, pl.BlockSpec(upper.shape, lambda e: (0, 0)), pl.BlockSpec(lower.shape, lambda e: (0, 0))],
        out_specs=[blk, blk],
        out_shape=[jax.ShapeDtypeStruct(sel3.shape, jnp.int32)] * 2,
        compiler_params=_params(("parallel",)),
        name="route_slots",
    )(sel3, upper, lower)


def _dispatch_kernel(offs_ref, h_ref, pos_ref, aff_ref, xe_ref, gs_ref,
                     acc_scr, gacc_scr, stage_scr, gstage_scr, cnt_scr, sem, gsem, *, nb):
    b = pl.program_id(0)
    t = h_ref.shape[0]
    w = SLOT_WINDOW
    ns = stage_scr.shape[0]
    lo, hi = slice(0, w), slice(w, 2 * w)

    @pl.when(b == 0)
    def _():
        acc_scr[...] = jnp.zeros(acc_scr.shape, F32)
        gacc_scr[...] = jnp.zeros(gacc_scr.shape, F32)
        cnt_scr[0] = 0

    hb = h_ref[...]
    slot_iota = lax.broadcasted_iota(jnp.int32, (2 * w, t), 0)
    slot_iota_w = lax.broadcasted_iota(jnp.int32, (w, t), 0)

    def xe_copy(slot, e, win):
        return pltpu.make_async_copy(
            stage_scr.at[slot], xe_ref.at[e, pl.ds(pl.multiple_of(win * w, w), w), :], sem.at[slot])

    def gs_copy(slot, e, win):
        return pltpu.make_async_copy(
            gstage_scr.at[slot], gs_ref.at[e, pl.ds(pl.multiple_of(win * w, w), w), :], gsem.at[slot])

    def add_rows(e, rows, place, arow):
        acc_scr[e, rows, :] += jnp.dot(place.astype(BF16), hb, preferred_element_type=F32)
        g = jnp.sum(place * arow, axis=1, keepdims=True)
        gacc_scr[e, rows, :] += jnp.broadcast_to(g, (g.shape[0], LANE))

    def flush(e, rows, win):
        c = cnt_scr[0]
        slot = c % ns

        @pl.when(c >= ns)
        def _():
            xe_copy(slot, e, win).wait()
            gs_copy(slot, e, win).wait()

        stage_scr[slot] = acc_scr[e, rows, :].astype(BF16)
        gstage_scr[slot] = gacc_scr[e, rows, :]
        xe_copy(slot, e, win).start()
        gs_copy(slot, e, win).start()
        cnt_scr[0] = c + 1

    def clear(e, rows):
        acc_scr[e, rows, :] = jnp.zeros((w, acc_scr.shape[2]), F32)
        gacc_scr[e, rows, :] = jnp.zeros((w, LANE), F32)

    for e in range(N_EXPERTS):
        w0 = offs_ref[e * (nb + 1) + b] // w
        place = jnp.where(slot_iota == pos_ref[e:e + 1, :] - w0 * w, 1.0, 0.0)
        add_rows(e, slice(0, 2 * w), place, aff_ref[e:e + 1, :])

    for e in range(N_EXPERTS):
        w0 = offs_ref[e * (nb + 1) + b] // w
        o1 = offs_ref[e * (nb + 1) + b + 1]
        done = jnp.minimum(o1 // w - w0, 2)

        @pl.when(done >= 1)
        def _(e=e, w0=w0):
            flush(e, lo, w0)

        @pl.when(done == 1)
        def _(e=e):
            acc_scr[e, lo, :] = acc_scr[e, hi, :]
            gacc_scr[e, lo, :] = gacc_scr[e, hi, :]
            clear(e, hi)

        @pl.when(done == 2)
        def _(e=e, w0=w0):
            flush(e, hi, w0 + 1)
            clear(e, lo)
            clear(e, hi)

        def extra(win, carry, e=e, o1=o1):
            place = jnp.where(slot_iota_w == pos_ref[e:e + 1, :] - win * w, 1.0, 0.0)
            add_rows(e, lo, place, aff_ref[e:e + 1, :])

            @pl.when(o1 >= (win + 1) * w)
            def _():
                flush(e, lo, win)
                clear(e, lo)

            return carry

        lax.fori_loop(w0 + 2, (o1 + w - 1) // w, extra, 0)

    @pl.when(b == nb - 1)
    def _():
        c = cnt_scr[0]
        for s in range(ns):
            @pl.when(c > s)
            def _(s=s):
                xe_copy(s, 0, 0).wait()
                gs_copy(s, 0, 0).wait()


def _dispatch(offs, h, pos, aff, *, cap):
    n, d = h.shape
    t = min(ROUTE_TOKENS, n)
    nb = n // t
    ne = N_EXPERTS
    w = SLOT_WINDOW
    grid_spec = pltpu.PrefetchScalarGridSpec(
        num_scalar_prefetch=1,
        grid=(nb,),
        in_specs=[pl.BlockSpec((t, d), lambda b, offs: (b, 0)),
                  pl.BlockSpec((ne, t), lambda b, offs: (0, b)),
                  pl.BlockSpec((ne, t), lambda b, offs: (0, b))],
        out_specs=[pl.BlockSpec(memory_space=pl.ANY), pl.BlockSpec(memory_space=pl.ANY)],
        scratch_shapes=[pltpu.VMEM((ne, 2 * w, d), F32), pltpu.VMEM((ne, 2 * w, LANE), F32),
                        pltpu.VMEM((DISPATCH_STAGES, w, d), BF16), pltpu.VMEM((DISPATCH_STAGES, w, LANE), F32),
                        pltpu.SMEM((1,), jnp.int32),
                        pltpu.SemaphoreType.DMA((DISPATCH_STAGES,)), pltpu.SemaphoreType.DMA((DISPATCH_STAGES,))])
    return pl.pallas_call(
        functools.partial(_dispatch_kernel, nb=nb),
        grid_spec=grid_spec,
        out_shape=[jax.ShapeDtypeStruct((ne, cap, d), BF16), jax.ShapeDtypeStruct((ne, cap, LANE), F32)],
        compiler_params=_params(("arbitrary",)),
        name="route_dispatch",
    )(offs, h, pos, aff)


_TN = (((0,), (0,)), ((), ()))


def _combine_kernel(offs_ref, x_ref, pos_ref, ye_ref, o_ref, buf_scr, sem, *, nb, cap):
    b = pl.program_id(0)
    t = x_ref.shape[0]
    rows = buf_scr.shape[1] // 2
    al = ROW_ALIGN

    def first_row(e):
        return jnp.minimum((offs_ref[e * (nb + 1) + b] // al) * al, cap - rows)

    def fetch(e, start):
        return pltpu.make_async_copy(
            ye_ref.at[e, pl.ds(pl.multiple_of(start, al), rows), :],
            buf_scr.at[e // 2, pl.ds((e % 2) * rows, rows), :], sem.at[e])

    for e in range(N_EXPERTS):
        fetch(e, first_row(e)).start()
    o_ref[...] = x_ref[...]
    slot_iota = lax.broadcasted_iota(jnp.int32, (rows, t), 0)
    for pair in range(N_EXPERTS // 2):
        places = []
        for e in (2 * pair, 2 * pair + 1):
            start = first_row(e)
            fetch(e, start).wait()
            places.append(jnp.where(slot_iota == pos_ref[e:e + 1, :] - start, 1.0, 0.0).astype(BF16))
        place = jnp.concatenate(places, axis=0)
        o_ref[...] += lax.dot_general(place, buf_scr[pair], _TN, preferred_element_type=F32)

        for e in (2 * pair, 2 * pair + 1):
            start = first_row(e)
            o1 = offs_ref[e * (nb + 1) + b + 1]
            half = pl.ds((e % 2) * rows, rows)

            def more(j, carry, e=e, start=start, half=half):
                lo = start + j * rows
                ws = jnp.minimum(lo, cap - rows)
                cp = fetch(e, ws)
                cp.start()
                cp.wait()
                prow = pos_ref[e:e + 1, :]
                hit = jnp.logical_and(slot_iota == prow - ws, prow >= lo)
                o_ref[...] += lax.dot_general(jnp.where(hit, 1.0, 0.0).astype(BF16), buf_scr[pair, half, :],
                                              _TN, preferred_element_type=F32)
                return carry

            lax.fori_loop(1, (jnp.maximum(o1 - start, 1) + rows - 1) // rows, more, 0)


def _combine(offs, x, pos, ye):
    n, d = x.shape
    ne, cap, _ = ye.shape
    t = min(ROUTE_TOKENS, n)
    nb = n // t
    rows = min(COMBINE_ROWS, cap)
    grid_spec = pltpu.PrefetchScalarGridSpec(
        num_scalar_prefetch=1,
        grid=(nb,),
        in_specs=[pl.BlockSpec((t, d), lambda b, offs: (b, 0)),
                  pl.BlockSpec((ne, t), lambda b, offs: (0, b)),
                  pl.BlockSpec(memory_space=pl.ANY)],
        out_specs=pl.BlockSpec((t, d), lambda b, offs: (b, 0)),
        scratch_shapes=[pltpu.VMEM((ne // 2, 2 * rows, d), BF16), pltpu.SemaphoreType.DMA((ne,))])
    return pl.pallas_call(
        functools.partial(_combine_kernel, nb=nb, cap=cap),
        grid_spec=grid_spec,
        out_shape=jax.ShapeDtypeStruct((n, d), F32),
        compiler_params=_params(("arbitrary",)),
        name="route_combine",
    )(offs, x, pos, ye)


def _moe(x, hf, aff, lp):
    n, d = x.shape
    ne = N_EXPERTS
    cap = EC_CAPACITY_FACTOR * n // ne
    t = min(ROUTE_TOKENS, n)
    nb = n // t
    aff_t = aff[:, :ne].T
    sel = _select(aff_t, cap=cap)
    pos3, base3 = _slots(sel.reshape(ne, n // LANE, LANE))
    pos = pos3.reshape(ne, n)
    offs = jnp.concatenate([base3[:, ::t // LANE, 0], jnp.full((ne, 1), cap, jnp.int32)], axis=1)
    offs = offs.reshape(-1)
    xe, gs = _dispatch(offs, hf, pos, aff_t, cap=cap)
    ye = _expert_ffn(xe, lp["w_gate"], lp["w_up"], lp["w_down"], gs, tm=min(512, cap))
    return _combine(offs, x, pos, ye)


def _t5_bucket(rel):
    half = REL_BUCKETS // 2
    max_exact = half // 2
    n = jnp.abs(rel)
    base = jnp.where(rel > 0, half, 0)
    nf = jnp.maximum(n, 1).astype(F32)
    large = max_exact + (jnp.log(nf / max_exact) / math.log(REL_MAX_DIST / max_exact)
                         * (half - max_exact)).astype(jnp.int32)
    large = jnp.minimum(large, half - 1)
    return base + jnp.where(n < max_exact, n, large)


def _band_bias(rel_bias, group, dilation, tq):
    tk = tq + 2 * A_HALF
    rel = jnp.arange(tk)[None, :] - A_HALF - jnp.arange(tq)[:, None]
    heads = slice(group * A_HEADS_PER_GROUP, (group + 1) * A_HEADS_PER_GROUP)
    b = jnp.transpose(rel_bias[:, heads][_t5_bucket(rel * dilation)], (2, 0, 1)).astype(F32)
    return jnp.where((jnp.abs(rel) <= A_HALF)[None], b, NEG_INF)


def _block_diag_ones(size, block):
    idx = jnp.arange(size) // block
    return (idx[:, None] == idx[None, :]).astype(BF16)


def _rope_tables(seq):
    half = MLA_ROPE // 2
    freqs = ROPE_THETA ** (-jnp.arange(half, dtype=F32) / half)
    ang = jnp.arange(seq).astype(F32)[:, None] * freqs[None, :]
    cos, sin = jnp.cos(ang), jnp.sin(ang)
    ones = jnp.ones((seq, MLA_NOPE), F32)
    z64 = jnp.zeros((seq, MLA_NOPE), F32)
    z16 = jnp.zeros((seq, half), F32)
    z32 = jnp.zeros((seq, LANE - MLA_QK_DIM), F32)
    c = jnp.concatenate([ones, cos, cos, z32], axis=1)
    s1 = jnp.concatenate([z64, -sin, z16, z32], axis=1)
    s2 = jnp.concatenate([z64, z16, sin, z32], axis=1)
    return c, s1, s2


def _pad_heads(w, heads, width, lo, hi):
    w = w.reshape(w.shape[0], heads, width)[:, :, lo:hi]
    w = jnp.pad(w, ((0, 0), (0, 0), (0, MLA_HEAD_PAD - (hi - lo))))
    return w.reshape(w.shape[0], heads * MLA_HEAD_PAD)


def _prep_layer(p, l):
    d = D_MODEL
    row = lambda v: v.reshape(1, -1).astype(F32)
    w_in = p["w_in"][l]
    out = {}
    out["g_mix"] = row(p["norm_mix"][l])
    ng = len(A_GROUPS)
    out["w_a"] = w_in[:, :COL_CQ].reshape(d, 3, ng, A_GROUP_WIDTH).transpose(0, 2, 1, 3).reshape(
        d, COL_CQ).astype(BF16)
    out["aux_a"] = row(jnp.tile(jnp.concatenate([
        jnp.tile(p["a_q_norm"][l], A_HEADS_PER_GROUP) * (A_HEAD_DIM ** -0.5),
        jnp.tile(p["a_k_norm"][l], A_HEADS_PER_GROUP),
        jnp.ones((A_GROUP_WIDTH,), F32)]), ng))
    out["w_c"] = jnp.pad(w_in[:, COL_CQ:COL_GA], ((0, 0), (0, d - (COL_GA - COL_CQ)))).astype(BF16)
    out["w_g"] = w_in[:, COL_GA:].astype(BF16)
    out["g_ql"] = row(p["mla_q_lat_norm"][l])
    out["g_kl"] = row(p["mla_kv_lat_norm"][l])
    out["w_uq"] = _pad_heads(p["w_uq"][l], MLA_HEADS, MLA_QK_DIM, 0, MLA_QK_DIM).astype(BF16)
    w_ukv = p["w_ukv"][l]
    k_nope = _pad_heads(w_ukv, MLA_HEADS, MLA_NOPE + MLA_V, 0, MLA_NOPE)
    eye = jnp.pad(jnp.eye(MLA_ROPE, dtype=F32), ((0, 0), (MLA_NOPE, MLA_HEAD_PAD - MLA_QK_DIM)))
    k_pe = jnp.tile(eye, (1, MLA_HEADS))
    k_rows = d - MLA_Q_RANK - MLA_KV_RANK - MLA_ROPE
    out["w_uk"] = jnp.concatenate(
        [k_nope, k_pe, jnp.zeros((k_rows, MLA_HEADS * MLA_HEAD_PAD), F32)], axis=0).astype(BF16)
    out["w_uv"] = w_ukv.reshape(MLA_KV_RANK, MLA_HEADS, MLA_NOPE + MLA_V)[:, :, MLA_NOPE:].reshape(
        MLA_KV_RANK, MLA_HEADS * MLA_V).astype(BF16)
    pad_gain = lambda g: jnp.tile(jnp.pad(g, (0, MLA_HEAD_PAD - MLA_QK_DIM)), MLA_HEADS)
    out["g_q"] = row(pad_gain(p["mla_q_norm"][l]) * (MLA_QK_DIM ** -0.5 * math.log2(math.e)))
    out["g_k"] = row(pad_gain(p["mla_k_norm"][l]))
    out["w_pa"] = p["w_proj_a"][l].astype(BF16)
    out["w_pb"] = p["w_proj_b"][l].astype(BF16)
    out["w_out"] = p["w_out"][l].astype(BF16)
    out["g_mx"] = row(p["norm_mem_x"][l])
    out["g_mkv"] = row(p["norm_mem_kv"][l])
    out["w_mq"] = p["w_mq"][l].astype(BF16)
    out["w_mkv"] = p["w_mkv"][l].astype(BF16)
    out["g_mq"] = row(p["mem_q_norm"][l] * (MEM_HEAD_DIM ** -0.5))
    out["g_mk"] = row(p["mem_k_norm"][l])
    out["w_mo"] = p["w_mo"][l].astype(BF16)
    out["g_ffn"] = row(p["norm_ffn"][l])
    out["w_router"] = jnp.pad(p["w_router"][l], ((0, 0), (0, LANE - N_EXPERTS))).astype(BF16)
    out["w_gate"] = p["w_gate"][l].astype(BF16)
    out["w_up"] = p["w_up"][l].astype(BF16)
    out["w_down"] = p["w_down"][l].astype(BF16)
    return out


BAND_TQ = 128
BAND_STEP_ROWS = 512
MLA_ROW_BLOCK = 32


def _layer(x, mem, lp, shared, *, batch, seq):
    n = batch * seq
    mem_tokens = mem.shape[0] // batch
    tm = min(1024, n)
    bd64, bd128 = shared["bd64"], shared["bd128"]
    zgs = _norm_proj_heads(x, lp["g_mix"], lp["w_a"], lp["aux_a"], bd64, tm=min(512, n))
    zc = _norm_proj(x, lp["g_mix"], lp["w_c"], mode="plain", tm=tm, tn=D_MODEL, out_dtype=F32)
    gates = _norm_proj(x, lp["g_mix"], lp["w_g"], mode="sigmoid", tm=tm, tn=D_MODEL, out_dtype=BF16)
    oas, lses = [], []
    for g, (_, dil) in enumerate(A_GROUPS):
        o, lse = _band_attention(zgs[g], shared["band_bias"][g], shared["hmask"], batch=batch, seq=seq,
                                 group=g, dilation=dil, tq=BAND_TQ)
        oas.append(o)
        lses.append(lse)
    qm, km, vm = _mla_prep(zc, lp["g_ql"], lp["g_kl"], lp["w_uq"], lp["w_uk"], lp["w_uv"],
                           lp["g_q"], lp["g_k"], shared["rope_c"], shared["rope_s1"],
                           shared["rope_s2"], bd128, seq=seq, tm=min(512, seq))
    ob = _mla_attention(qm, km, vm, batch=batch, seq=seq, tq=min(512, seq), tk=min(512, seq))
    x = _merge(oas, lses, ob, gates, x, lp["w_pa"], lp["w_pb"], lp["w_out"], tm=min(512, n))
    kmem, vmem = _mem_kv(mem, lp["g_mkv"], lp["w_mkv"], lp["g_mk"], tm=min(512, mem.shape[0]))
    x, hf, aff = _mem_attention(x, kmem, vmem, lp["g_mx"], lp["w_mq"], lp["g_mq"], lp["w_mo"],
                                lp["g_ffn"], lp["w_router"], batch=batch, seq=seq,
                                mem_tokens=mem_tokens, tm=min(512, seq))
    return _moe(x, hf, aff, lp)


def _trunk(x, mem, layers, shared):
    batch, seq, d = x.shape
    xf = x.reshape(batch * seq, d)
    memf = mem.reshape(-1, d)
    for lp in layers:
        xf = _layer(xf, memf, lp, shared, batch=batch, seq=seq)
    return xf.reshape(batch, seq, d)


def kernel(x_prompt, x_sample, mem_prompt, mem_sample, norm_mix, w_in, a_q_norm, a_k_norm, rel_bias,
           mla_q_lat_norm, w_uq, mla_kv_lat_norm, w_ukv, mla_q_norm, mla_k_norm, w_proj_a, w_proj_b,
           w_out, norm_mem_x, norm_mem_kv, w_mq, w_mkv, mem_q_norm, mem_k_norm, w_mo, norm_ffn,
           w_router, w_gate, w_up, w_down):
    p = dict(norm_mix=norm_mix, w_in=w_in, a_q_norm=a_q_norm, a_k_norm=a_k_norm,
             mla_q_lat_norm=mla_q_lat_norm, w_uq=w_uq, mla_kv_lat_norm=mla_kv_lat_norm, w_ukv=w_ukv,
             mla_q_norm=mla_q_norm, mla_k_norm=mla_k_norm, w_proj_a=w_proj_a, w_proj_b=w_proj_b,
             w_out=w_out, norm_mem_x=norm_mem_x, norm_mem_kv=norm_mem_kv, w_mq=w_mq, w_mkv=w_mkv,
             mem_q_norm=mem_q_norm, mem_k_norm=mem_k_norm, w_mo=w_mo, norm_ffn=norm_ffn,
             w_router=w_router, w_gate=w_gate, w_up=w_up, w_down=w_down)
    layers = [_prep_layer(p, l) for l in range(DEPTH)]
    outs = []
    for x, mem in ((x_prompt, mem_prompt), (x_sample, mem_sample)):
        seq = x.shape[1]
        c, s1, s2 = _rope_tables(seq)
        hm = (jnp.arange(LANE)[None, :] // A_HEAD_DIM == jnp.arange(16)[:, None]).astype(BF16)
        shared = dict(
            bd64=_block_diag_ones(256, A_HEAD_DIM), bd128=_block_diag_ones(256, MLA_HEAD_PAD),
            band_bias=[_band_bias(rel_bias, g, dil, BAND_TQ) for g, (_, dil) in enumerate(A_GROUPS)],
            hmask=hm, rope_c=c, rope_s1=s1, rope_s2=s2)
        outs.append(_trunk(x, mem, layers, shared))
    return tuple(outs)
```

```python
import functools
import math

import jax
import jax.numpy as jnp
from jax import lax
from jax.experimental import pallas as pl
from jax.experimental.pallas import tpu as pltpu

F32 = jnp.float32
BF16 = jnp.bfloat16

D_MODEL = 1024
DEPTH = 2
A_GROUPS = ((128, 1), (512, 4), (2048, 16))
A_HEADS_PER_GROUP = 8
A_HEAD_DIM = 64
A_GROUP_WIDTH = A_HEADS_PER_GROUP * A_HEAD_DIM
A_WIDTH = A_GROUP_WIDTH * len(A_GROUPS)
A_HALF = 64
MLA_HEADS = 8
MLA_NOPE = 64
MLA_ROPE = 32
MLA_QK_DIM = MLA_NOPE + MLA_ROPE
MLA_V = 64
MLA_Q_RANK = 512
MLA_KV_RANK = 256
MLA_HEAD_PAD = 128
ROPE_THETA = 10000.0
MEM_HEADS = 4
MEM_HEAD_DIM = 128
MEM_WIDTH = MEM_HEADS * MEM_HEAD_DIM
REL_BUCKETS = 32
REL_MAX_DIST = 1024
N_EXPERTS = 16
EC_CAPACITY_FACTOR = 2
RMS_EPS = 1e-6
NEG_INF = -1e30
COL_CQ = 3 * A_WIDTH
COL_CKV = COL_CQ + MLA_Q_RANK
COL_GA = COL_CKV + MLA_KV_RANK + MLA_ROPE
IN_COLS = COL_GA + 2 * D_MODEL

LANE = 128
MXU_COLS = 256
VMEM_LIMIT = 48 * 1024 * 1024

_NT = (((1,), (1,)), ((), ()))


def _rms(x, g):
    ms = jnp.mean(x * x, axis=-1, keepdims=True)
    return x * lax.rsqrt(ms + RMS_EPS) * g


def _params(sem):
    return pltpu.CompilerParams(dimension_semantics=sem, vmem_limit_bytes=VMEM_LIMIT)


def _proj_kernel(x_ref, g_ref, w_ref, o_ref, h_scr, *, mode):
    j = pl.program_id(1)

    @pl.when(j == 0)
    def _():
        h_scr[...] = _rms(x_ref[...], g_ref[...]).astype(BF16)

    z = jnp.dot(h_scr[...], w_ref[...], preferred_element_type=F32)
    if mode == "sigmoid":
        z = jax.nn.sigmoid(z)
    o_ref[...] = z.astype(o_ref.dtype)


def _proj_heads_kernel(x_ref, g_ref, w_ref, aux_ref, bd_ref, o0_ref, o1_ref, o2_ref, hf_scr, hp_scr):
    j = pl.program_id(1)
    tm = x_ref.shape[0]
    w = 3 * A_GROUP_WIDTH

    @pl.when(j == 0)
    def _():
        h = _rms(x_ref[...], g_ref[...])
        for c in range(h.shape[1] // LANE):
            hf_scr[c] = h[:, c * LANE:(c + 1) * LANE]
        for g, (_, dil) in enumerate(A_GROUPS):
            if dil == 1:
                hp_scr[g] = h.astype(BF16)
                continue
            rows = tm // dil
            for r in range(dil):
                for c in range(h.shape[1] // LANE):
                    hp_scr[g, r * rows:(r + 1) * rows, c * LANE:(c + 1) * LANE] = (
                        hf_scr[c, pl.ds(r, rows, stride=dil), :].astype(BF16))

    z = jnp.dot(hp_scr[j], w_ref[...], preferred_element_type=F32)
    cols = []
    for c in range(2 * A_GROUP_WIDTH // MXU_COLS):
        sl = slice(c * MXU_COLS, (c + 1) * MXU_COLS)
        zc = z[:, sl]
        ss = jnp.dot((zc * zc).astype(BF16), bd_ref[...], preferred_element_type=F32)
        cols.append((zc * lax.rsqrt(ss * (1.0 / A_HEAD_DIM) + RMS_EPS) * aux_ref[:, sl]).astype(BF16))
    cols.append(z[:, 2 * A_GROUP_WIDTH:].astype(BF16))
    zn = jnp.concatenate(cols, axis=1)
    for g, ((_, dil), o_ref) in enumerate(zip(A_GROUPS, (o0_ref, o1_ref, o2_ref))):
        @pl.when(j == g)
        def _(dil=dil, o_ref=o_ref):
            rows = tm // dil
            for r in range(dil):
                o_ref[:, r * w:(r + 1) * w] = zn[r * rows:(r + 1) * rows, :]


def _norm_proj_heads(x, g, w, aux, bd, *, tm):
    n, d = x.shape
    wg = 3 * A_GROUP_WIDTH
    ng = len(A_GROUPS)
    return pl.pallas_call(
        _proj_heads_kernel,
        grid=(n // tm, ng),
        in_specs=[
            pl.BlockSpec((tm, d), lambda i, j: (i, 0)),
            pl.BlockSpec((1, d), lambda i, j: (0, 0)),
            pl.BlockSpec((d, wg), lambda i, j: (0, j)),
            pl.BlockSpec((1, wg), lambda i, j: (0, j)),
            pl.BlockSpec(bd.shape, lambda i, j: (0, 0)),
        ],
        out_specs=[pl.BlockSpec((tm // dil, dil * wg), lambda i, j: (i, 0)) for _, dil in A_GROUPS],
        out_shape=[jax.ShapeDtypeStruct((n // dil, dil * wg), BF16) for _, dil in A_GROUPS],
        scratch_shapes=[pltpu.VMEM((d // LANE, tm, LANE), F32), pltpu.VMEM((ng, tm, d), BF16)],
        compiler_params=_params(("parallel", "arbitrary")),
        name="norm_proj_heads",
    )(x, g, w, aux, bd)


def _norm_proj(x, g, w, *, mode, tm, tn, out_dtype):
    n, d = x.shape
    ncols = w.shape[1]
    return pl.pallas_call(
        functools.partial(_proj_kernel, mode=mode),
        grid=(n // tm, ncols // tn),
        in_specs=[
            pl.BlockSpec((tm, d), lambda i, j: (i, 0)),
            pl.BlockSpec((1, d), lambda i, j: (0, 0)),
            pl.BlockSpec((d, tn), lambda i, j: (0, j)),
        ],
        out_specs=pl.BlockSpec((tm, tn), lambda i, j: (i, j)),
        out_shape=jax.ShapeDtypeStruct((n, ncols), out_dtype),
        scratch_shapes=[pltpu.VMEM((tm, d), BF16)],
        compiler_params=_params(("parallel", "arbitrary")),
        name="norm_proj_" + mode,
    )(x, g, w)


def _band_attn_kernel(q_ref, kl_ref, km_ref, kh_ref, vl_ref, vm_ref, vh_ref, bias_ref, hm_ref,
                      o_ref, lse_ref, *, tq, seq_len):
    i = pl.program_id(2)
    tk = tq + 2 * A_HALF
    nsub = q_ref.shape[0] // tq
    k_all = jnp.concatenate([kl_ref[...], km_ref[...], kh_ref[...]], axis=0)
    v_all = jnp.concatenate([vl_ref[...], vm_ref[...], vh_ref[...]], axis=0)
    first = lax.broadcasted_iota(jnp.int32, (tq, LANE), 1) < A_HEAD_DIM
    for sub in range(nsub):
        q = q_ref[sub * tq:(sub + 1) * tq, :]
        k = k_all[sub * tq:sub * tq + tk, :]
        v = v_all[sub * tq:sub * tq + tk, :]
        kpos = (i * nsub + sub) * tq - A_HALF + lax.broadcasted_iota(jnp.int32, (1, tk), 1)
        valid = jnp.logical_and(kpos >= 0, kpos < seq_len)
        o_parts, lse_parts = [], []
        for j in range(A_GROUP_WIDTH // LANE):
            sl = slice(j * LANE, (j + 1) * LANE)
            qp, kp, vp = q[:, sl], k[:, sl], v[:, sl]
            res = []
            for hh in range(2):
                qm = qp * hm_ref[hh:hh + 1, :]
                s = lax.dot_general(qm, kp, _NT, preferred_element_type=F32)
                s = jnp.where(valid, s + bias_ref[2 * j + hh], NEG_INF)
                m = jnp.max(s, axis=-1, keepdims=True)
                p = jnp.exp(s - m)
                l = jnp.sum(p, axis=-1, keepdims=True)
                o = jnp.dot(p.astype(BF16), vp, preferred_element_type=F32) / l
                res.append((o, m + jnp.log(l)))
            o_parts.append(jnp.where(first, res[0][0], res[1][0]))
            lse_parts.append(jnp.where(first, res[0][1], res[1][1]))
        o_ref[sub * tq:(sub + 1) * tq, :] = jnp.concatenate(o_parts, axis=1).astype(o_ref.dtype)
        lse_ref[sub * tq:(sub + 1) * tq, :] = jnp.concatenate(lse_parts, axis=1)


def _band_attention(zg, bias, hmask, *, batch, seq, group, dilation, tq):
    sl = seq // dilation
    zv = zg.reshape(batch, sl, dilation * 3 * A_GROUP_WIDTH)
    step = min(BAND_STEP_ROWS, sl)
    hb = step // A_HALF
    nhb = sl // A_HALF
    w = A_GROUP_WIDTH

    def main(off):
        return pl.BlockSpec((None, step, w), lambda b, r, i: (b, i, r * 3 + off))

    def lo(off):
        return pl.BlockSpec((None, A_HALF, w),
                            lambda b, r, i: (b, jnp.maximum(i * hb - 1, 0), r * 3 + off))

    def hi(off):
        return pl.BlockSpec((None, A_HALF, w),
                            lambda b, r, i: (b, jnp.minimum((i + 1) * hb, nhb - 1), r * 3 + off))

    out_spec = pl.BlockSpec((None, step, w), lambda b, r, i: (b, i, r))
    o, lse = pl.pallas_call(
        functools.partial(_band_attn_kernel, tq=tq, seq_len=sl),
        grid=(batch, dilation, sl // step),
        in_specs=[main(0), lo(1), main(1), hi(1), lo(2), main(2), hi(2),
                  pl.BlockSpec(bias.shape, lambda b, r, i: (0, 0, 0)),
                  pl.BlockSpec(hmask.shape, lambda b, r, i: (0, 0))],
        out_specs=[out_spec, out_spec],
        out_shape=[jax.ShapeDtypeStruct((batch, sl, dilation * w), BF16),
                   jax.ShapeDtypeStruct((batch, sl, dilation * w), F32)],
        compiler_params=_params(("parallel", "parallel", "parallel")),
        name=f"band_attn_g{group}",
    )(zv, zv, zv, zv, zv, zv, zv, bias, hmask)
    return o.reshape(batch * sl, dilation * w), lse.reshape(batch * sl, dilation * w)


def _mla_prep_kernel(zc_ref, gql_ref, gkl_ref, wq_ref, wk_ref, wv_ref, gq_ref, gk_ref,
                     c_ref, s1_ref, s2_ref, bd_ref, q_ref, k_ref, v_ref):
    zc = zc_ref[...]
    cq = _rms(zc[:, :MLA_Q_RANK], gql_ref[...]).astype(BF16)
    ckv = _rms(zc[:, MLA_Q_RANK:MLA_Q_RANK + MLA_KV_RANK], gkl_ref[...])
    kin = jnp.concatenate([ckv, zc[:, MLA_Q_RANK + MLA_KV_RANK:]], axis=1).astype(BF16)
    q = jnp.dot(cq, wq_ref[...], preferred_element_type=F32)
    k = jnp.dot(kin, wk_ref[...], preferred_element_type=F32)
    v = jnp.dot(ckv.astype(BF16), wv_ref[...], preferred_element_type=F32)
    first = lax.broadcasted_iota(jnp.int32, (v.shape[0], LANE), 1) < MLA_V
    for j in range(MLA_HEADS // 2):
        vp = v[:, j * LANE:(j + 1) * LANE]
        v_ref[:, 2 * j * LANE:(2 * j + 1) * LANE] = jnp.where(first, vp, 1.0).astype(v_ref.dtype)
        v_ref[:, (2 * j + 1) * LANE:(2 * j + 2) * LANE] = jnp.where(first, 1.0, vp).astype(v_ref.dtype)
    cos, s1, s2 = c_ref[...], s1_ref[...], s2_ref[...]
    for src, g_ref, dst in ((q, gq_ref, q_ref), (k, gk_ref, k_ref)):
        for c in range(src.shape[1] // 256):
            sl = slice(c * 256, (c + 1) * 256)
            xc = src[:, sl]
            ss = jnp.dot((xc * xc).astype(BF16), bd_ref[...], preferred_element_type=F32)
            xn = xc * lax.rsqrt(ss * (1.0 / MLA_QK_DIM) + RMS_EPS) * g_ref[:, sl]
            for hh in range(2):
                xh = xn[:, hh * LANE:(hh + 1) * LANE]
                out = (xh * cos + pltpu.roll(xh, LANE - MLA_ROPE // 2, 1) * s1
                       + pltpu.roll(xh, MLA_ROPE // 2, 1) * s2)
                lo = c * 256 + hh * LANE
                dst[:, lo:lo + LANE] = out.astype(dst.dtype)


def _mla_prep(zc, gql, gkl, wq, wk, wv, gq, gk, cos, s1, s2, bd, *, seq, tm):
    n = zc.shape[0]
    hp = MLA_HEADS * MLA_HEAD_PAD
    vw = MLA_HEADS * MLA_HEAD_PAD
    nsb = seq // tm
    full = lambda a: pl.BlockSpec(a.shape, lambda i: (0,) * a.ndim)
    tab = pl.BlockSpec((tm, LANE), lambda i: (i % nsb, 0))
    return pl.pallas_call(
        _mla_prep_kernel,
        grid=(n // tm,),
        in_specs=[pl.BlockSpec((tm, zc.shape[1]), lambda i: (i, 0)),
                  full(gql), full(gkl), full(wq), full(wk), full(wv), full(gq), full(gk),
                  tab, tab, tab, full(bd)],
        out_specs=[pl.BlockSpec((tm, hp), lambda i: (i, 0)),
                   pl.BlockSpec((tm, hp), lambda i: (i, 0)),
                   pl.BlockSpec((tm, vw), lambda i: (i, 0))],
        out_shape=[jax.ShapeDtypeStruct((n, hp), BF16),
                   jax.ShapeDtypeStruct((n, hp), BF16),
                   jax.ShapeDtypeStruct((n, vw), BF16)],
        compiler_params=_params(("parallel",)),
        name="mla_prep",
    )(zc, gql, gkl, wq, wk, wv, gq, gk, cos, s1, s2, bd)


def _mla_attn_kernel(q_ref, k_ref, v_ref, o_ref, sa_scr, sb_scr, pa_scr, pb_scr, aa_scr, ab_scr,
                     m_scr, acc_scr, *, tk, rb):
    tq = q_ref.shape[0]
    nk = k_ref.shape[0] // tk
    hs = [slice(hh * MLA_HEAD_PAD, (hh + 1) * MLA_HEAD_PAD) for hh in range(2)]

    def rows_of(c):
        return pl.ds(pl.multiple_of(c * tk, tk), tk)

    def scores(c, dst):
        for hh in range(2):
            dst[hh] = lax.dot_general(q_ref[:, hs[hh]], k_ref[rows_of(c), hs[hh]], _NT,
                                      preferred_element_type=F32)

    def softmax(src, p_dst, a_dst):
        for hh in range(2):
            for r in range(tq // rb):
                rs = slice(r * rb, (r + 1) * rb)
                s = src[hh, rs, :]
                m_old = m_scr[hh, rs, :]
                m_new = jnp.maximum(m_old, jnp.max(s, axis=-1, keepdims=True))
                p_dst[hh, rs, :] = jnp.exp2(s - m_new).astype(BF16)
                m_scr[hh, rs, :] = m_new
                a_dst[hh, rs, :] = jnp.exp2(m_old - m_new)

    def values(c, p_src, a_src):
        for hh in range(2):
            acc_scr[hh] = a_src[hh] * acc_scr[hh] + jnp.dot(p_src[hh], v_ref[rows_of(c), hs[hh]],
                                                            preferred_element_type=F32)

    m_scr[...] = jnp.full(m_scr.shape, NEG_INF, F32)
    acc_scr[...] = jnp.zeros(acc_scr.shape, F32)
    scores(0, sa_scr)
    softmax(sa_scr, pa_scr, aa_scr)
    scores(1, sb_scr)

    def body(i, carry):
        values(2 * i, pa_scr, aa_scr)
        softmax(sb_scr, pb_scr, ab_scr)
        scores(2 * i + 2, sa_scr)
        values(2 * i + 1, pb_scr, ab_scr)
        softmax(sa_scr, pa_scr, aa_scr)
        scores(2 * i + 3, sb_scr)
        return carry

    lax.fori_loop(0, nk // 2 - 1, body, 0)
    values(nk - 2, pa_scr, aa_scr)
    softmax(sb_scr, pb_scr, ab_scr)
    values(nk - 1, pb_scr, ab_scr)
    acc0, acc1 = acc_scr[0], acc_scr[1]
    o0 = acc0 / pltpu.roll(acc0, MLA_V, 1)
    o1 = acc1 / pltpu.roll(acc1, MLA_V, 1)
    first = lax.broadcasted_iota(jnp.int32, (tq, LANE), 1) < MLA_V
    o_ref[...] = jnp.where(first, o0, o1).astype(o_ref.dtype)


def _mla_attention(q, k, v, *, batch, seq, tq, tk):
    n = batch * seq
    nqb = seq // tq
    assert (seq // tk) % 2 == 0
    return pl.pallas_call(
        functools.partial(_mla_attn_kernel, tk=tk, rb=MLA_ROW_BLOCK),
        grid=(batch, MLA_HEADS // 2, nqb),
        in_specs=[pl.BlockSpec((tq, 2 * MLA_HEAD_PAD), lambda b, h, i: (b * nqb + i, h)),
                  pl.BlockSpec((seq, 2 * MLA_HEAD_PAD), lambda b, h, i: (b, h)),
                  pl.BlockSpec((seq, 2 * MLA_HEAD_PAD), lambda b, h, i: (b, h))],
        out_specs=pl.BlockSpec((tq, 2 * MLA_V), lambda b, h, i: (b * nqb + i, h)),
        out_shape=jax.ShapeDtypeStruct((n, MLA_HEADS * MLA_V), BF16),
        scratch_shapes=[pltpu.VMEM((2, tq, tk), F32), pltpu.VMEM((2, tq, tk), F32),
                        pltpu.VMEM((2, tq, tk), BF16), pltpu.VMEM((2, tq, tk), BF16),
                        pltpu.VMEM((2, tq, 1), F32), pltpu.VMEM((2, tq, 1), F32),
                        pltpu.VMEM((2, tq, 1), F32), pltpu.VMEM((2, tq, LANE), F32)],
        compiler_params=_params(("parallel", "parallel", "arbitrary")),
        name="mla_attn",
    )(q, k, v)


def _merge_kernel(o0_ref, l0_ref, o1_ref, l1_ref, o2_ref, l2_ref, ob_ref, g_ref, x_ref,
                  wa_ref, wb_ref, wo_ref, xo_ref, nat_scr):
    tm = x_ref.shape[0]
    w = A_GROUP_WIDTH

    def natural(ref, slot, dil):
        if dil == 1:
            return ref[...].astype(F32)
        rows = tm // dil
        nc = w // LANE
        for r in range(dil):
            for c in range(nc):
                lo = r * w + c * LANE
                nat_scr[slot * nc + c, pl.ds(r, rows, stride=dil), :] = ref[:, lo:lo + LANE].astype(F32)
        return jnp.concatenate([nat_scr[slot * nc + c] for c in range(nc)], axis=1)

    dils = [dil for _, dil in A_GROUPS]
    l0, l1, l2 = (natural(ref, 2 * g, dils[g]) for g, ref in enumerate((l0_ref, l1_ref, l2_ref)))
    o0, o1, o2 = (natural(ref, 2 * g + 1, dils[g]) for g, ref in enumerate((o0_ref, o1_ref, o2_ref)))
    m = jnp.maximum(jnp.maximum(l0, l1), l2)
    e0, e1, e2 = jnp.exp(l0 - m), jnp.exp(l1 - m), jnp.exp(l2 - m)
    oa = (e0 * o0 + e1 * o1 + e2 * o2) / (e0 + e1 + e2)
    pa = jnp.dot(oa.astype(BF16), wa_ref[...], preferred_element_type=F32)
    pb = jnp.dot(ob_ref[...], wb_ref[...], preferred_element_type=F32)
    d = pa.shape[1]
    merged = g_ref[:, :d].astype(F32) * pa + g_ref[:, d:].astype(F32) * pb
    xo_ref[...] = x_ref[...] + jnp.dot(merged.astype(BF16), wo_ref[...], preferred_element_type=F32)


def _merge(oas, lses, ob, gates, x, wa, wb, wo, *, tm):
    n, d = x.shape
    row = lambda a: pl.BlockSpec((tm * a.shape[0] // n, a.shape[1]), lambda i: (i, 0))
    full = lambda a: pl.BlockSpec(a.shape, lambda i: (0, 0))
    args = [oas[0], lses[0], oas[1], lses[1], oas[2], lses[2], ob, gates, x]
    return pl.pallas_call(
        _merge_kernel,
        grid=(n // tm,),
        in_specs=[row(a) for a in args] + [full(wa), full(wb), full(wo)],
        out_specs=pl.BlockSpec((tm, d), lambda i: (i, 0)),
        out_shape=jax.ShapeDtypeStruct((n, d), F32),
        scratch_shapes=[pltpu.VMEM((2 * len(A_GROUPS) * A_GROUP_WIDTH // LANE, tm, LANE), F32)],
        compiler_params=_params(("parallel",)),
        name="merge",
    )(*args, wa, wb, wo)


def _mem_kv_kernel(mem_ref, g_ref, w_ref, gk_ref, k_ref, v_ref):
    mh = _rms(mem_ref[...], g_ref[...]).astype(BF16)
    kv = jnp.dot(mh, w_ref[...], preferred_element_type=F32)
    for h in range(MEM_HEADS):
        sl = slice(h * MEM_HEAD_DIM, (h + 1) * MEM_HEAD_DIM)
        k_ref[:, sl] = _rms(kv[:, sl], gk_ref[...]).astype(k_ref.dtype)
    v_ref[...] = kv[:, MEM_WIDTH:].astype(v_ref.dtype)


def _mem_kv(mem, g, w, gk, *, tm):
    n = mem.shape[0]
    full = lambda a: pl.BlockSpec(a.shape, lambda i: (0, 0))
    return pl.pallas_call(
        _mem_kv_kernel,
        grid=(n // tm,),
        in_specs=[pl.BlockSpec((tm, mem.shape[1]), lambda i: (i, 0)), full(g), full(w), full(gk)],
        out_specs=[pl.BlockSpec((tm, MEM_WIDTH), lambda i: (i, 0))] * 2,
        out_shape=[jax.ShapeDtypeStruct((n, MEM_WIDTH), BF16)] * 2,
        compiler_params=_params(("parallel",)),
        name="mem_kv",
    )(mem, g, w, gk)


def _mem_attn_kernel(x_ref, k_ref, v_ref, gx_ref, wq_ref, gq_ref, wo_ref, gf_ref, wr_ref,
                     xo_ref, hf_ref, aff_ref):
    x = x_ref[...]
    hx = _rms(x, gx_ref[...]).astype(BF16)
    q = jnp.dot(hx, wq_ref[...], preferred_element_type=F32)
    outs = []
    for h in range(MEM_HEADS):
        sl = slice(h * MEM_HEAD_DIM, (h + 1) * MEM_HEAD_DIM)
        qn = _rms(q[:, sl], gq_ref[...]).astype(BF16)
        s = lax.dot_general(qn, k_ref[:, sl], _NT, preferred_element_type=F32)
        m = jnp.max(s, axis=-1, keepdims=True)
        p = jnp.exp(s - m)
        l = jnp.sum(p, axis=-1, keepdims=True)
        outs.append(jnp.dot(p.astype(BF16), v_ref[:, sl], preferred_element_type=F32) / l)
    o = jnp.concatenate(outs, axis=1).astype(BF16)
    x2 = x + jnp.dot(o, wo_ref[...], preferred_element_type=F32)
    xo_ref[...] = x2
    hf = _rms(x2, gf_ref[...]).astype(BF16)
    hf_ref[...] = hf
    logits = jnp.dot(hf, wr_ref[...], preferred_element_type=F32)
    lane = lax.broadcasted_iota(jnp.int32, logits.shape, 1)
    logits = jnp.where(lane < N_EXPERTS, logits, NEG_INF)
    e = jnp.exp(logits - jnp.max(logits, axis=-1, keepdims=True))
    aff_ref[...] = e / jnp.sum(e, axis=-1, keepdims=True)


def _mem_attention(x, kmem, vmem, gx, wq, gq, wo, gf, wr, *, batch, seq, mem_tokens, tm):
    n, d = x.shape
    nsb = seq // tm
    full = lambda a: pl.BlockSpec(a.shape, lambda b, i: (0, 0))
    row = lambda w: pl.BlockSpec((tm, w), lambda b, i: (b * nsb + i, 0))
    kv = pl.BlockSpec((mem_tokens, MEM_WIDTH), lambda b, i: (b, 0))
    return pl.pallas_call(
        _mem_attn_kernel,
        grid=(batch, nsb),
        in_specs=[row(d), kv, kv, full(gx), full(wq), full(gq), full(wo), full(gf), full(wr)],
        out_specs=[row(d), row(d), row(LANE)],
        out_shape=[jax.ShapeDtypeStruct((n, d), F32),
                   jax.ShapeDtypeStruct((n, d), BF16),
                   jax.ShapeDtypeStruct((n, LANE), F32)],
        compiler_params=_params(("parallel", "parallel")),
        name="mem_attn",
    )(x, kmem, vmem, gx, wq, gq, wo, gf, wr)


def _ffn_kernel(x_ref, wg_ref, wu_ref, wd_ref, g_ref, y_ref):
    x = x_ref[...]
    a = jnp.dot(x, wg_ref[...], preferred_element_type=F32)
    b = jnp.dot(x, wu_ref[...], preferred_element_type=F32)
    hid = (a * jax.nn.sigmoid(a) * b).astype(BF16)
    y = jnp.dot(hid, wd_ref[...], preferred_element_type=F32)
    g = g_ref[...]
    y_ref[...] = (y * jnp.concatenate([g] * (y.shape[1] // LANE), axis=1)).astype(y_ref.dtype)


def _expert_ffn(xe, wg, wu, wd, gates, *, tm):
    e, cap, d = xe.shape
    f = wg.shape[2]
    return pl.pallas_call(
        _ffn_kernel,
        grid=(e, cap // tm),
        in_specs=[pl.BlockSpec((None, tm, d), lambda e, i: (e, i, 0)),
                  pl.BlockSpec((None, d, f), lambda e, i: (e, 0, 0)),
                  pl.BlockSpec((None, d, f), lambda e, i: (e, 0, 0)),
                  pl.BlockSpec((None, f, d), lambda e, i: (e, 0, 0)),
                  pl.BlockSpec((None, tm, LANE), lambda e, i: (e, i, 0))],
        out_specs=pl.BlockSpec((None, tm, d), lambda e, i: (e, i, 0)),
        out_shape=jax.ShapeDtypeStruct((e, cap, d), BF16),
        compiler_params=_params(("parallel", "arbitrary")),
        name="expert_ffn",
    )(xe, wg, wu, wd, gates)


SLOT_WINDOW = 128
ROUTE_TOKENS = 512
COMBINE_ROWS = 128
ROW_ALIGN = 16
DISPATCH_STAGES = 4


def _select_kernel(aff_ref, sel_ref, *, cap):
    bits = pltpu.bitcast(aff_ref[...], jnp.int32)
    ne, n = bits.shape
    count = lambda mask: jnp.sum(jnp.where(mask, 1, 0), axis=1, keepdims=True)

    def value_bit(i, t):
        cand = t | jnp.left_shift(1, 30 - i)
        return jnp.where(count(bits >= cand) >= cap, cand, t)

    thr = lax.fori_loop(0, 31, value_bit, jnp.zeros((ne, 1), jnp.int32))
    gt = bits > thr
    eq = bits == thr
    need = cap - count(gt)
    idx = lax.broadcasted_iota(jnp.int32, (ne, n), 1)
    nbits = max(1, (n - 1).bit_length())

    def index_bit(i, j):
        cand = j | jnp.left_shift(1, nbits - 1 - i)
        return jnp.where(count(jnp.logical_and(eq, idx < cand)) < need, cand, j)

    last = lax.fori_loop(0, nbits, index_bit, jnp.zeros((ne, 1), jnp.int32))
    sel = jnp.logical_or(gt, jnp.logical_and(eq, idx <= last))
    sel_ref[...] = jnp.where(sel, 1.0, 0.0).astype(sel_ref.dtype)


def _select(aff_t, *, cap):
    return pl.pallas_call(
        functools.partial(_select_kernel, cap=cap),
        out_shape=jax.ShapeDtypeStruct(aff_t.shape, BF16),
        compiler_params=pltpu.CompilerParams(vmem_limit_bytes=VMEM_LIMIT),
        name="route_select",
    )(aff_t)


def _slots_kernel(sel_ref, upper_ref, lower_ref, pos_ref, base_ref):
    s = sel_ref[...]
    incl = jnp.dot(s, upper_ref[...], preferred_element_type=F32)
    tot = jnp.broadcast_to(incl[:, LANE - 1:LANE], incl.shape).astype(BF16)
    base = jnp.dot(lower_ref[...], tot, preferred_element_type=F32)
    pos_ref[...] = jnp.where(s > 0, base + incl - 1.0, -1.0).astype(jnp.int32)
    base_ref[...] = base.astype(jnp.int32)


def _slots(sel3):
    ne, nt, _ = sel3.shape
    upper = (jnp.arange(LANE)[:, None] <= jnp.arange(LANE)[None, :]).astype(BF16)
    lower = (jnp.arange(nt)[None, :] < jnp.arange(nt)[:, None]).astype(BF16)
    blk = pl.BlockSpec((None, nt, LANE), lambda e: (e, 0, 0))
    return pl.pallas_call(
        _slots_kernel,
        grid=(ne,),
        in_specs=[blk, pl.BlockSpec(upper.shape, lambda e: (0, 0)), pl.BlockSpec(lower.shape, lambda e: (0, 0))],
        out_specs=[blk, blk],
        out_shape=[jax.ShapeDtypeStruct(sel3.shape, jnp.int32)] * 2,
        compiler_params=_params(("parallel",)),
        name="route_slots",
    )(sel3, upper, lower)


def _dispatch_kernel(offs_ref, h_ref, pos_ref, aff_ref, xe_ref, gs_ref,
                     acc_scr, gacc_scr, stage_scr, gstage_scr, cnt_scr, sem, gsem, *, nb):
    b = pl.program_id(0)
    t = h_ref.shape[0]
    w = SLOT_WINDOW
    ns = stage_scr.shape[0]
    lo, hi = slice(0, w), slice(w, 2 * w)

    @pl.when(b == 0)
    def _():
        acc_scr[...] = jnp.zeros(acc_scr.shape, F32)
        gacc_scr[...] = jnp.zeros(gacc_scr.shape, F32)
        cnt_scr[0] = 0

    hb = h_ref[...]
    slot_iota = lax.broadcasted_iota(jnp.int32, (2 * w, t), 0)
    slot_iota_w = lax.broadcasted_iota(jnp.int32, (w, t), 0)

    def xe_copy(slot, e, win):
        return pltpu.make_async_copy(
            stage_scr.at[slot], xe_ref.at[e, pl.ds(pl.multiple_of(win * w, w), w), :], sem.at[slot])

    def gs_copy(slot, e, win):
        return pltpu.make_async_copy(
            gstage_scr.at[slot], gs_ref.at[e, pl.ds(pl.multiple_of(win * w, w), w), :], gsem.at[slot])

    def add_rows(e, rows, place, arow):
        acc_scr[e, rows, :] += jnp.dot(place.astype(BF16), hb, preferred_element_type=F32)
        g = jnp.sum(place * arow, axis=1, keepdims=True)
        gacc_scr[e, rows, :] += jnp.broadcast_to(g, (g.shape[0], LANE))

    def flush(e, rows, win):
        c = cnt_scr[0]
        slot = c % ns

        @pl.when(c >= ns)
        def _():
            xe_copy(slot, e, win).wait()
            gs_copy(slot, e, win).wait()

        stage_scr[slot] = acc_scr[e, rows, :].astype(BF16)
        gstage_scr[slot] = gacc_scr[e, rows, :]
        xe_copy(slot, e, win).start()
        gs_copy(slot, e, win).start()
        cnt_scr[0] = c + 1

    def clear(e, rows):
        acc_scr[e, rows, :] = jnp.zeros((w, acc_scr.shape[2]), F32)
        gacc_scr[e, rows, :] = jnp.zeros((w, LANE), F32)

    for e in range(N_EXPERTS):
        w0 = offs_ref[e * (nb + 1) + b] // w
        place = jnp.where(slot_iota == pos_ref[e:e + 1, :] - w0 * w, 1.0, 0.0)
        add_rows(e, slice(0, 2 * w), place, aff_ref[e:e + 1, :])

    for e in range(N_EXPERTS):
        w0 = offs_ref[e * (nb + 1) + b] // w
        o1 = offs_ref[e * (nb + 1) + b + 1]
        done = jnp.minimum(o1 // w - w0, 2)

        @pl.when(done >= 1)
        def _(e=e, w0=w0):
            flush(e, lo, w0)

        @pl.when(done == 1)
        def _(e=e):
            acc_scr[e, lo, :] = acc_scr[e, hi, :]
            gacc_scr[e, lo, :] = gacc_scr[e, hi, :]
            clear(e, hi)

        @pl.when(done == 2)
        def _(e=e, w0=w0):
            flush(e, hi, w0 + 1)
            clear(e, lo)
            clear(e, hi)

        def extra(win, carry, e=e, o1=o1):
            place = jnp.where(slot_iota_w == pos_ref[e:e + 1, :] - win * w, 1.0, 0.0)
            add_rows(e, lo, place, aff_ref[e:e + 1, :])

            @pl.when(o1 >= (win + 1) * w)
            def _():
                flush(e, lo, win)
                clear(e, lo)

            return carry

        lax.fori_loop(w0 + 2, (o1 + w - 1) // w, extra, 0)

    @pl.when(b == nb - 1)
    def _():
        c = cnt_scr[0]
        for s in range(ns):
            @pl.when(c > s)
            def _(s=s):
                xe_copy(s, 0, 0).wait()
                gs_copy(s, 0, 0).wait()


def _dispatch(offs, h, pos, aff, *, cap):
    n, d = h.shape
    t = min(ROUTE_TOKENS, n)
    nb = n // t
    ne = N_EXPERTS
    w = SLOT_WINDOW
    grid_spec = pltpu.PrefetchScalarGridSpec(
        num_scalar_prefetch=1,
        grid=(nb,),
        in_specs=[pl.BlockSpec((t, d), lambda b, offs: (b, 0)),
                  pl.BlockSpec((ne, t), lambda b, offs: (0, b)),
                  pl.BlockSpec((ne, t), lambda b, offs: (0, b))],
        out_specs=[pl.BlockSpec(memory_space=pl.ANY), pl.BlockSpec(memory_space=pl.ANY)],
        scratch_shapes=[pltpu.VMEM((ne, 2 * w, d), F32), pltpu.VMEM((ne, 2 * w, LANE), F32),
                        pltpu.VMEM((DISPATCH_STAGES, w, d), BF16), pltpu.VMEM((DISPATCH_STAGES, w, LANE), F32),
                        pltpu.SMEM((1,), jnp.int32),
                        pltpu.SemaphoreType.DMA((DISPATCH_STAGES,)), pltpu.SemaphoreType.DMA((DISPATCH_STAGES,))])
    return pl.pallas_call(
        functools.partial(_dispatch_kernel, nb=nb),
        grid_spec=grid_spec,
        out_shape=[jax.ShapeDtypeStruct((ne, cap, d), BF16), jax.ShapeDtypeStruct((ne, cap, LANE), F32)],
        compiler_params=_params(("arbitrary",)),
        name="route_dispatch",
    )(offs, h, pos, aff)


_TN = (((0,), (0,)), ((), ()))


def _combine_kernel(offs_ref, x_ref, pos_ref, ye_ref, o_ref, buf_scr, sem, *, nb, cap):
    b = pl.program_id(0)
    t = x_ref.shape[0]
    rows = buf_scr.shape[2] // 2
    al = ROW_ALIGN
    par = b % 2

    def first_row(e, blk):
        return jnp.minimum((offs_ref[e * (nb + 1) + blk] // al) * al, cap - rows)

    def fetch(e, start, p):
        return pltpu.make_async_copy(
            ye_ref.at[e, pl.ds(pl.multiple_of(start, al), rows), :],
            buf_scr.at[p, e // 2, pl.ds((e % 2) * rows, rows), :], sem.at[p, e])

    @pl.when(b == 0)
    def _():
        for e in range(N_EXPERTS):
            fetch(e, first_row(e, 0), 0).start()

    @pl.when(b + 1 < nb)
    def _():
        for e in range(N_EXPERTS):
            fetch(e, first_row(e, b + 1), 1 - par).start()

    for e in range(N_EXPERTS):
        fetch(e, first_row(e, b), par).wait()
    slot_iota = lax.broadcasted_iota(jnp.int32, (rows, t), 0)
    total = x_ref[...]
    for pair in range(N_EXPERTS // 2):
        places = [jnp.where(slot_iota == pos_ref[e:e + 1, :] - first_row(e, b), 1.0, 0.0).astype(BF16)
                  for e in (2 * pair, 2 * pair + 1)]
        place = jnp.concatenate(places, axis=0)
        total = total + lax.dot_general(place, buf_scr[par, pair], _TN, preferred_element_type=F32)
    o_ref[...] = total

    for e in range(N_EXPERTS):
        start = first_row(e, b)
        o1 = offs_ref[e * (nb + 1) + b + 1]
        half = pl.ds((e % 2) * rows, rows)

        def more(j, carry, e=e, start=start, half=half):
            lo = start + j * rows
            ws = jnp.minimum(lo, cap - rows)
            cp = fetch(e, ws, par)
            cp.start()
            cp.wait()
            prow = pos_ref[e:e + 1, :]
            hit = jnp.logical_and(slot_iota == prow - ws, prow >= lo)
            o_ref[...] += lax.dot_general(jnp.where(hit, 1.0, 0.0).astype(BF16),
                                          buf_scr[par, e // 2, half, :], _TN, preferred_element_type=F32)
            return carry

        lax.fori_loop(1, (jnp.maximum(o1 - start, 1) + rows - 1) // rows, more, 0)


def _combine(offs, x, pos, ye):
    n, d = x.shape
    ne, cap, _ = ye.shape
    t = min(ROUTE_TOKENS, n)
    nb = n // t
    rows = min(COMBINE_ROWS, cap)
    grid_spec = pltpu.PrefetchScalarGridSpec(
        num_scalar_prefetch=1,
        grid=(nb,),
        in_specs=[pl.BlockSpec((t, d), lambda b, offs: (b, 0)),
                  pl.BlockSpec((ne, t), lambda b, offs: (0, b)),
                  pl.BlockSpec(memory_space=pl.ANY)],
        out_specs=pl.BlockSpec((t, d), lambda b, offs: (b, 0)),
        scratch_shapes=[pltpu.VMEM((2, ne // 2, 2 * rows, d), BF16), pltpu.SemaphoreType.DMA((2, ne))])
    return pl.pallas_call(
        functools.partial(_combine_kernel, nb=nb, cap=cap),
        grid_spec=grid_spec,
        out_shape=jax.ShapeDtypeStruct((n, d), F32),
        compiler_params=_params(("arbitrary",)),
        name="route_combine",
    )(offs, x, pos, ye)


def _moe(x, hf, aff, lp):
    n, d = x.shape
    ne = N_EXPERTS
    cap = EC_CAPACITY_FACTOR * n // ne
    t = min(ROUTE_TOKENS, n)
    nb = n // t
    aff_t = aff[:, :ne].T
    sel = _select(aff_t, cap=cap)
    pos3, base3 = _slots(sel.reshape(ne, n // LANE, LANE))
    pos = pos3.reshape(ne, n)
    offs = jnp.concatenate([base3[:, ::t // LANE, 0], jnp.full((ne, 1), cap, jnp.int32)], axis=1)
    offs = offs.reshape(-1)
    xe, gs = _dispatch(offs, hf, pos, aff_t, cap=cap)
    ye = _expert_ffn(xe, lp["w_gate"], lp["w_up"], lp["w_down"], gs, tm=min(512, cap))
    return _combine(offs, x, pos, ye)


def _t5_bucket(rel):
    half = REL_BUCKETS // 2
    max_exact = half // 2
    n = jnp.abs(rel)
    base = jnp.where(rel > 0, half, 0)
    nf = jnp.maximum(n, 1).astype(F32)
    large = max_exact + (jnp.log(nf / max_exact) / math.log(REL_MAX_DIST / max_exact)
                         * (half - max_exact)).astype(jnp.int32)
    large = jnp.minimum(large, half - 1)
    return base + jnp.where(n < max_exact, n, large)


def _band_bias(rel_bias, group, dilation, tq):
    tk = tq + 2 * A_HALF
    rel = jnp.arange(tk)[None, :] - A_HALF - jnp.arange(tq)[:, None]
    heads = slice(group * A_HEADS_PER_GROUP, (group + 1) * A_HEADS_PER_GROUP)
    onehot = jax.nn.one_hot(_t5_bucket(rel * dilation), REL_BUCKETS, dtype=F32)
    b = jnp.einsum("qkb,bh->hqk", onehot, rel_bias[:, heads].astype(F32), precision=lax.Precision.HIGHEST)
    return jnp.where((jnp.abs(rel) <= A_HALF)[None], b, NEG_INF)


def _block_diag_ones(size, block):
    idx = jnp.arange(size) // block
    return (idx[:, None] == idx[None, :]).astype(BF16)


def _rope_tables(seq):
    half = MLA_ROPE // 2
    freqs = ROPE_THETA ** (-jnp.arange(half, dtype=F32) / half)
    ang = jnp.arange(seq).astype(F32)[:, None] * freqs[None, :]
    cos, sin = jnp.cos(ang), jnp.sin(ang)
    ones = jnp.ones((seq, MLA_NOPE), F32)
    z64 = jnp.zeros((seq, MLA_NOPE), F32)
    z16 = jnp.zeros((seq, half), F32)
    z32 = jnp.zeros((seq, LANE - MLA_QK_DIM), F32)
    c = jnp.concatenate([ones, cos, cos, z32], axis=1)
    s1 = jnp.concatenate([z64, -sin, z16, z32], axis=1)
    s2 = jnp.concatenate([z64, z16, sin, z32], axis=1)
    return c, s1, s2


def _pad_heads(w, heads, width, lo, hi):
    w = w.reshape(w.shape[0], heads, width)[:, :, lo:hi]
    w = jnp.pad(w, ((0, 0), (0, 0), (0, MLA_HEAD_PAD - (hi - lo))))
    return w.reshape(w.shape[0], heads * MLA_HEAD_PAD)


def _prep_layer(p, l):
    d = D_MODEL
    row = lambda v: v.reshape(1, -1).astype(F32)
    w_in = p["w_in"][l]
    out = {}
    out["g_mix"] = row(p["norm_mix"][l])
    ng = len(A_GROUPS)
    out["w_a"] = w_in[:, :COL_CQ].reshape(d, 3, ng, A_GROUP_WIDTH).transpose(0, 2, 1, 3).reshape(
        d, COL_CQ).astype(BF16)
    out["aux_a"] = row(jnp.tile(jnp.concatenate([
        jnp.tile(p["a_q_norm"][l], A_HEADS_PER_GROUP) * (A_HEAD_DIM ** -0.5),
        jnp.tile(p["a_k_norm"][l], A_HEADS_PER_GROUP),
        jnp.ones((A_GROUP_WIDTH,), F32)]), ng))
    out["w_c"] = jnp.pad(w_in[:, COL_CQ:COL_GA], ((0, 0), (0, d - (COL_GA - COL_CQ)))).astype(BF16)
    out["w_g"] = w_in[:, COL_GA:].astype(BF16)
    out["g_ql"] = row(p["mla_q_lat_norm"][l])
    out["g_kl"] = row(p["mla_kv_lat_norm"][l])
    out["w_uq"] = _pad_heads(p["w_uq"][l], MLA_HEADS, MLA_QK_DIM, 0, MLA_QK_DIM).astype(BF16)
    w_ukv = p["w_ukv"][l]
    k_nope = _pad_heads(w_ukv, MLA_HEADS, MLA_NOPE + MLA_V, 0, MLA_NOPE)
    eye = jnp.pad(jnp.eye(MLA_ROPE, dtype=F32), ((0, 0), (MLA_NOPE, MLA_HEAD_PAD - MLA_QK_DIM)))
    k_pe = jnp.tile(eye, (1, MLA_HEADS))
    k_rows = d - MLA_Q_RANK - MLA_KV_RANK - MLA_ROPE
    out["w_uk"] = jnp.concatenate(
        [k_nope, k_pe, jnp.zeros((k_rows, MLA_HEADS * MLA_HEAD_PAD), F32)], axis=0).astype(BF16)
    out["w_uv"] = w_ukv.reshape(MLA_KV_RANK, MLA_HEADS, MLA_NOPE + MLA_V)[:, :, MLA_NOPE:].reshape(
        MLA_KV_RANK, MLA_HEADS * MLA_V).astype(BF16)
    pad_gain = lambda g: jnp.tile(jnp.pad(g, (0, MLA_HEAD_PAD - MLA_QK_DIM)), MLA_HEADS)
    out["g_q"] = row(pad_gain(p["mla_q_norm"][l]) * (MLA_QK_DIM ** -0.5 * math.log2(math.e)))
    out["g_k"] = row(pad_gain(p["mla_k_norm"][l]))
    out["w_pa"] = p["w_proj_a"][l].astype(BF16)
    out["w_pb"] = p["w_proj_b"][l].astype(BF16)
    out["w_out"] = p["w_out"][l].astype(BF16)
    out["g_mx"] = row(p["norm_mem_x"][l])
    out["g_mkv"] = row(p["norm_mem_kv"][l])
    out["w_mq"] = p["w_mq"][l].astype(BF16)
    out["w_mkv"] = p["w_mkv"][l].astype(BF16)
    out["g_mq"] = row(p["mem_q_norm"][l] * (MEM_HEAD_DIM ** -0.5))
    out["g_mk"] = row(p["mem_k_norm"][l])
    out["w_mo"] = p["w_mo"][l].astype(BF16)
    out["g_ffn"] = row(p["norm_ffn"][l])
    out["w_router"] = jnp.pad(p["w_router"][l], ((0, 0), (0, LANE - N_EXPERTS))).astype(BF16)
    out["w_gate"] = p["w_gate"][l].astype(BF16)
    out["w_up"] = p["w_up"][l].astype(BF16)
    out["w_down"] = p["w_down"][l].astype(BF16)
    return out


BAND_TQ = 128
BAND_STEP_ROWS = 512
MLA_ROW_BLOCK = 32


def _layer(x, mem, lp, shared, *, batch, seq):
    n = batch * seq
    mem_tokens = mem.shape[0] // batch
    tm = min(1024, n)
    bd64, bd128 = shared["bd64"], shared["bd128"]
    zgs = _norm_proj_heads(x, lp["g_mix"], lp["w_a"], lp["aux_a"], bd64, tm=min(512, n))
    zc = _norm_proj(x, lp["g_mix"], lp["w_c"], mode="plain", tm=tm, tn=D_MODEL, out_dtype=F32)
    gates = _norm_proj(x, lp["g_mix"], lp["w_g"], mode="sigmoid", tm=tm, tn=D_MODEL, out_dtype=BF16)
    oas, lses = [], []
    for g, (_, dil) in enumerate(A_GROUPS):
        o, lse = _band_attention(zgs[g], shared["band_bias"][g], shared["hmask"], batch=batch, seq=seq,
                                 group=g, dilation=dil, tq=BAND_TQ)
        oas.append(o)
        lses.append(lse)
    qm, km, vm = _mla_prep(zc, lp["g_ql"], lp["g_kl"], lp["w_uq"], lp["w_uk"], lp["w_uv"],
                           lp["g_q"], lp["g_k"], shared["rope_c"], shared["rope_s1"],
                           shared["rope_s2"], bd128, seq=seq, tm=min(512, seq))
    ob = _mla_attention(qm, km, vm, batch=batch, seq=seq, tq=min(512, seq), tk=min(512, seq))
    x = _merge(oas, lses, ob, gates, x, lp["w_pa"], lp["w_pb"], lp["w_out"], tm=min(512, n))
    kmem, vmem = _mem_kv(mem, lp["g_mkv"], lp["w_mkv"], lp["g_mk"], tm=min(512, mem.shape[0]))
    x, hf, aff = _mem_attention(x, kmem, vmem, lp["g_mx"], lp["w_mq"], lp["g_mq"], lp["w_mo"],
                                lp["g_ffn"], lp["w_router"], batch=batch, seq=seq,
                                mem_tokens=mem_tokens, tm=min(512, seq))
    return _moe(x, hf, aff, lp)


def _trunk(x, mem, layers, shared):
    batch, seq, d = x.shape
    xf = x.reshape(batch * seq, d)
    memf = mem.reshape(-1, d)
    for lp in layers:
        xf = _layer(xf, memf, lp, shared, batch=batch, seq=seq)
    return xf.reshape(batch, seq, d)


def kernel(x_prompt, x_sample, mem_prompt, mem_sample, norm_mix, w_in, a_q_norm, a_k_norm, rel_bias,
           mla_q_lat_norm, w_uq, mla_kv_lat_norm, w_ukv, mla_q_norm, mla_k_norm, w_proj_a, w_proj_b,
           w_out, norm_mem_x, norm_mem_kv, w_mq, w_mkv, mem_q_norm, mem_k_norm, w_mo, norm_ffn,
           w_router, w_gate, w_up, w_down):
    p = dict(norm_mix=norm_mix, w_in=w_in, a_q_norm=a_q_norm, a_k_norm=a_k_norm,
             mla_q_lat_norm=mla_q_lat_norm, w_uq=w_uq, mla_kv_lat_norm=mla_kv_lat_norm, w_ukv=w_ukv,
             mla_q_norm=mla_q_norm, mla_k_norm=mla_k_norm, w_proj_a=w_proj_a, w_proj_b=w_proj_b,
             w_out=w_out, norm_mem_x=norm_mem_x, norm_mem_kv=norm_mem_kv, w_mq=w_mq, w_mkv=w_mkv,
             mem_q_norm=mem_q_norm, mem_k_norm=mem_k_norm, w_mo=w_mo, norm_ffn=norm_ffn,
             w_router=w_router, w_gate=w_gate, w_up=w_up, w_down=w_down)
    layers = [_prep_layer(p, l) for l in range(DEPTH)]
    outs = []
    for x, mem in ((x_prompt, mem_prompt), (x_sample, mem_sample)):
        seq = x.shape[1]
        c, s1, s2 = _rope_tables(seq)
        hm = (jnp.arange(LANE)[None, :] // A_HEAD_DIM == jnp.arange(16)[:, None]).astype(BF16)
        shared = dict(
            bd64=_block_diag_ones(256, A_HEAD_DIM), bd128=_block_diag_ones(256, MLA_HEAD_PAD),
            band_bias=[_band_bias(rel_bias, g, dil, BAND_TQ) for g, (_, dil) in enumerate(A_GROUPS)],
            hmask=hm, rope_c=c, rope_s1=s1, rope_s2=s2)
        outs.append(_trunk(x, mem, layers, shared))
    return tuple(outs)
```

```python
import functools
import math

import jax
import jax.numpy as jnp
from jax import lax
from jax.experimental import pallas as pl
from jax.experimental.pallas import tpu as pltpu

F32 = jnp.float32
BF16 = jnp.bfloat16

D_MODEL = 1024
DEPTH = 2
A_GROUPS = ((128, 1), (512, 4), (2048, 16))
A_HEADS_PER_GROUP = 8
A_HEAD_DIM = 64
A_GROUP_WIDTH = A_HEADS_PER_GROUP * A_HEAD_DIM
A_WIDTH = A_GROUP_WIDTH * len(A_GROUPS)
A_HALF = 64
MLA_HEADS = 8
MLA_NOPE = 64
MLA_ROPE = 32
MLA_QK_DIM = MLA_NOPE + MLA_ROPE
MLA_V = 64
MLA_Q_RANK = 512
MLA_KV_RANK = 256
MLA_HEAD_PAD = 128
ROPE_THETA = 10000.0
MEM_HEADS = 4
MEM_HEAD_DIM = 128
MEM_WIDTH = MEM_HEADS * MEM_HEAD_DIM
REL_BUCKETS = 32
REL_MAX_DIST = 1024
N_EXPERTS = 16
EC_CAPACITY_FACTOR = 2
RMS_EPS = 1e-6
NEG_INF = -1e30
COL_CQ = 3 * A_WIDTH
COL_CKV = COL_CQ + MLA_Q_RANK
COL_GA = COL_CKV + MLA_KV_RANK + MLA_ROPE
IN_COLS = COL_GA + 2 * D_MODEL

LANE = 128
MXU_COLS = 256
VMEM_LIMIT = 48 * 1024 * 1024

_NT = (((1,), (1,)), ((), ()))


def _rms(x, g):
    ms = jnp.mean(x * x, axis=-1, keepdims=True)
    return x * lax.rsqrt(ms + RMS_EPS) * g


def _params(sem):
    return pltpu.CompilerParams(dimension_semantics=sem, vmem_limit_bytes=VMEM_LIMIT)


def _proj_kernel(x_ref, g_ref, w_ref, o_ref, h_scr, *, mode):
    j = pl.program_id(1)

    @pl.when(j == 0)
    def _():
        h_scr[...] = _rms(x_ref[...], g_ref[...]).astype(BF16)

    z = jnp.dot(h_scr[...], w_ref[...], preferred_element_type=F32)
    if mode == "sigmoid":
        z = jax.nn.sigmoid(z)
    o_ref[...] = z.astype(o_ref.dtype)


def _proj_heads_kernel(x_ref, g_ref, w_ref, aux_ref, bd_ref, o0_ref, o1_ref, o2_ref, hf_scr, hp_scr):
    j = pl.program_id(1)
    tm = x_ref.shape[0]
    w = 3 * A_GROUP_WIDTH

    @pl.when(j == 0)
    def _():
        h = _rms(x_ref[...], g_ref[...])
        for c in range(h.shape[1] // LANE):
            hf_scr[c] = h[:, c * LANE:(c + 1) * LANE]
        for g, (_, dil) in enumerate(A_GROUPS):
            if dil == 1:
                hp_scr[g] = h.astype(BF16)
                continue
            rows = tm // dil
            for r in range(dil):
                for c in range(h.shape[1] // LANE):
                    hp_scr[g, r * rows:(r + 1) * rows, c * LANE:(c + 1) * LANE] = (
                        hf_scr[c, pl.ds(r, rows, stride=dil), :].astype(BF16))

    z = jnp.dot(hp_scr[j], w_ref[...], preferred_element_type=F32)
    cols = []
    for c in range(2 * A_GROUP_WIDTH // MXU_COLS):
        sl = slice(c * MXU_COLS, (c + 1) * MXU_COLS)
        zc = z[:, sl]
        ss = jnp.dot((zc * zc).astype(BF16), bd_ref[...], preferred_element_type=F32)
        cols.append((zc * lax.rsqrt(ss * (1.0 / A_HEAD_DIM) + RMS_EPS) * aux_ref[:, sl]).astype(BF16))
    cols.append(z[:, 2 * A_GROUP_WIDTH:].astype(BF16))
    zn = jnp.concatenate(cols, axis=1)
    for g, ((_, dil), o_ref) in enumerate(zip(A_GROUPS, (o0_ref, o1_ref, o2_ref))):
        @pl.when(j == g)
        def _(dil=dil, o_ref=o_ref):
            rows = tm // dil
            for r in range(dil):
                o_ref[:, r * w:(r + 1) * w] = zn[r * rows:(r + 1) * rows, :]


def _norm_proj_heads(x, g, w, aux, bd, *, tm):
    n, d = x.shape
    wg = 3 * A_GROUP_WIDTH
    ng = len(A_GROUPS)
    return pl.pallas_call(
        _proj_heads_kernel,
        grid=(n // tm, ng),
        in_specs=[
            pl.BlockSpec((tm, d), lambda i, j: (i, 0)),
            pl.BlockSpec((1, d), lambda i, j: (0, 0)),
            pl.BlockSpec((d, wg), lambda i, j: (0, j)),
            pl.BlockSpec((1, wg), lambda i, j: (0, j)),
            pl.BlockSpec(bd.shape, lambda i, j: (0, 0)),
        ],
        out_specs=[pl.BlockSpec((tm // dil, dil * wg), lambda i, j: (i, 0)) for _, dil in A_GROUPS],
        out_shape=[jax.ShapeDtypeStruct((n // dil, dil * wg), BF16) for _, dil in A_GROUPS],
        scratch_shapes=[pltpu.VMEM((d // LANE, tm, LANE), F32), pltpu.VMEM((ng, tm, d), BF16)],
        compiler_params=_params(("parallel", "arbitrary")),
        name="norm_proj_heads",
    )(x, g, w, aux, bd)


def _norm_proj(x, g, w, *, mode, tm, tn, out_dtype):
    n, d = x.shape
    ncols = w.shape[1]
    return pl.pallas_call(
        functools.partial(_proj_kernel, mode=mode),
        grid=(n // tm, ncols // tn),
        in_specs=[
            pl.BlockSpec((tm, d), lambda i, j: (i, 0)),
            pl.BlockSpec((1, d), lambda i, j: (0, 0)),
            pl.BlockSpec((d, tn), lambda i, j: (0, j)),
        ],
        out_specs=pl.BlockSpec((tm, tn), lambda i, j: (i, j)),
        out_shape=jax.ShapeDtypeStruct((n, ncols), out_dtype),
        scratch_shapes=[pltpu.VMEM((tm, d), BF16)],
        compiler_params=_params(("parallel", "arbitrary")),
        name="norm_proj_" + mode,
    )(x, g, w)


def _band_attn_kernel(q_ref, kl_ref, km_ref, kh_ref, vl_ref, vm_ref, vh_ref, bias_ref, hm_ref,
                      o_ref, lse_ref, *, tq, seq_len):
    i = pl.program_id(2)
    tk = tq + 2 * A_HALF
    nsub = q_ref.shape[0] // tq
    k_all = jnp.concatenate([kl_ref[...], km_ref[...], kh_ref[...]], axis=0)
    v_all = jnp.concatenate([vl_ref[...], vm_ref[...], vh_ref[...]], axis=0)
    first = lax.broadcasted_iota(jnp.int32, (tq, LANE), 1) < A_HEAD_DIM
    for sub in range(nsub):
        q = q_ref[sub * tq:(sub + 1) * tq, :]
        k = k_all[sub * tq:sub * tq + tk, :]
        v = v_all[sub * tq:sub * tq + tk, :]
        kpos = (i * nsub + sub) * tq - A_HALF + lax.broadcasted_iota(jnp.int32, (1, tk), 1)
        valid = jnp.logical_and(kpos >= 0, kpos < seq_len)
        o_parts, lse_parts = [], []
        for j in range(A_GROUP_WIDTH // LANE):
            sl = slice(j * LANE, (j + 1) * LANE)
            qp, kp, vp = q[:, sl], k[:, sl], v[:, sl]
            res = []
            for hh in range(2):
                qm = qp * hm_ref[hh:hh + 1, :]
                s = lax.dot_general(qm, kp, _NT, preferred_element_type=F32)
                s = jnp.where(valid, s + bias_ref[2 * j + hh], NEG_INF)
                m = jnp.max(s, axis=-1, keepdims=True)
                p = jnp.exp(s - m)
                l = jnp.sum(p, axis=-1, keepdims=True)
                o = jnp.dot(p.astype(BF16), vp, preferred_element_type=F32) / l
                res.append((o, m + jnp.log(l)))
            o_parts.append(jnp.where(first, res[0][0], res[1][0]))
            lse_parts.append(jnp.where(first, res[0][1], res[1][1]))
        o_ref[sub * tq:(sub + 1) * tq, :] = jnp.concatenate(o_parts, axis=1).astype(o_ref.dtype)
        lse_ref[sub * tq:(sub + 1) * tq, :] = jnp.concatenate(lse_parts, axis=1)


def _band_attention(zg, bias, hmask, *, batch, seq, group, dilation, tq):
    sl = seq // dilation
    zv = zg.reshape(batch, sl, dilation * 3 * A_GROUP_WIDTH)
    step = min(BAND_STEP_ROWS, sl)
    hb = step // A_HALF
    nhb = sl // A_HALF
    w = A_GROUP_WIDTH

    def main(off):
        return pl.BlockSpec((None, step, w), lambda b, r, i: (b, i, r * 3 + off))

    def lo(off):
        return pl.BlockSpec((None, A_HALF, w),
                            lambda b, r, i: (b, jnp.maximum(i * hb - 1, 0), r * 3 + off))

    def hi(off):
        return pl.BlockSpec((None, A_HALF, w),
                            lambda b, r, i: (b, jnp.minimum((i + 1) * hb, nhb - 1), r * 3 + off))

    out_spec = pl.BlockSpec((None, step, w), lambda b, r, i: (b, i, r))
    o, lse = pl.pallas_call(
        functools.partial(_band_attn_kernel, tq=tq, seq_len=sl),
        grid=(batch, dilation, sl // step),
        in_specs=[main(0), lo(1), main(1), hi(1), lo(2), main(2), hi(2),
                  pl.BlockSpec(bias.shape, lambda b, r, i: (0, 0, 0)),
                  pl.BlockSpec(hmask.shape, lambda b, r, i: (0, 0))],
        out_specs=[out_spec, out_spec],
        out_shape=[jax.ShapeDtypeStruct((batch, sl, dilation * w), BF16),
                   jax.ShapeDtypeStruct((batch, sl, dilation * w), F32)],
        compiler_params=_params(("parallel", "parallel", "parallel")),
        name=f"band_attn_g{group}",
    )(zv, zv, zv, zv, zv, zv, zv, bias, hmask)
    return o.reshape(batch * sl, dilation * w), lse.reshape(batch * sl, dilation * w)


def _mla_prep_kernel(zc_ref, gql_ref, gkl_ref, wq_ref, wk_ref, wv_ref, gq_ref, gk_ref,
                     c_ref, s1_ref, s2_ref, bd_ref, q_ref, k_ref, v_ref):
    zc = zc_ref[...]
    cq = _rms(zc[:, :MLA_Q_RANK], gql_ref[...]).astype(BF16)
    ckv = _rms(zc[:, MLA_Q_RANK:MLA_Q_RANK + MLA_KV_RANK], gkl_ref[...])
    kin = jnp.concatenate([ckv, zc[:, MLA_Q_RANK + MLA_KV_RANK:]], axis=1).astype(BF16)
    q = jnp.dot(cq, wq_ref[...], preferred_element_type=F32)
    k = jnp.dot(kin, wk_ref[...], preferred_element_type=F32)
    v = jnp.dot(ckv.astype(BF16), wv_ref[...], preferred_element_type=F32)
    first = lax.broadcasted_iota(jnp.int32, (v.shape[0], LANE), 1) < MLA_V
    for j in range(MLA_HEADS // 2):
        vp = v[:, j * LANE:(j + 1) * LANE]
        v_ref[:, 2 * j * LANE:(2 * j + 1) * LANE] = jnp.where(first, vp, 1.0).astype(v_ref.dtype)
        v_ref[:, (2 * j + 1) * LANE:(2 * j + 2) * LANE] = jnp.where(first, 1.0, vp).astype(v_ref.dtype)
    cos, s1, s2 = c_ref[...], s1_ref[...], s2_ref[...]
    for src, g_ref, dst in ((q, gq_ref, q_ref), (k, gk_ref, k_ref)):
        for c in range(src.shape[1] // 256):
            sl = slice(c * 256, (c + 1) * 256)
            xc = src[:, sl]
            ss = jnp.dot((xc * xc).astype(BF16), bd_ref[...], preferred_element_type=F32)
            xn = xc * lax.rsqrt(ss * (1.0 / MLA_QK_DIM) + RMS_EPS) * g_ref[:, sl]
            for hh in range(2):
                xh = xn[:, hh * LANE:(hh + 1) * LANE]
                out = (xh * cos + pltpu.roll(xh, LANE - MLA_ROPE // 2, 1) * s1
                       + pltpu.roll(xh, MLA_ROPE // 2, 1) * s2)
                lo = c * 256 + hh * LANE
                dst[:, lo:lo + LANE] = out.astype(dst.dtype)


def _mla_prep(zc, gql, gkl, wq, wk, wv, gq, gk, cos, s1, s2, bd, *, seq, tm):
    n = zc.shape[0]
    hp = MLA_HEADS * MLA_HEAD_PAD
    vw = MLA_HEADS * MLA_HEAD_PAD
    nsb = seq // tm
    full = lambda a: pl.BlockSpec(a.shape, lambda i: (0,) * a.ndim)
    tab = pl.BlockSpec((tm, LANE), lambda i: (i % nsb, 0))
    return pl.pallas_call(
        _mla_prep_kernel,
        grid=(n // tm,),
        in_specs=[pl.BlockSpec((tm, zc.shape[1]), lambda i: (i, 0)),
                  full(gql), full(gkl), full(wq), full(wk), full(wv), full(gq), full(gk),
                  tab, tab, tab, full(bd)],
        out_specs=[pl.BlockSpec((tm, hp), lambda i: (i, 0)),
                   pl.BlockSpec((tm, hp), lambda i: (i, 0)),
                   pl.BlockSpec((tm, vw), lambda i: (i, 0))],
        out_shape=[jax.ShapeDtypeStruct((n, hp), BF16),
                   jax.ShapeDtypeStruct((n, hp), BF16),
                   jax.ShapeDtypeStruct((n, vw), BF16)],
        compiler_params=_params(("parallel",)),
        name="mla_prep",
    )(zc, gql, gkl, wq, wk, wv, gq, gk, cos, s1, s2, bd)


def _mla_attn_kernel(q_ref, k_ref, v_ref, o_ref, sa_scr, sb_scr, pa_scr, pb_scr, aa_scr, ab_scr,
                     m_scr, acc_scr, *, tk, rb):
    tq = q_ref.shape[0]
    nk = k_ref.shape[0] // tk
    hs = [slice(hh * MLA_HEAD_PAD, (hh + 1) * MLA_HEAD_PAD) for hh in range(2)]

    def rows_of(c):
        return pl.ds(pl.multiple_of(c * tk, tk), tk)

    def scores(c, dst):
        for hh in range(2):
            dst[hh] = lax.dot_general(q_ref[:, hs[hh]], k_ref[rows_of(c), hs[hh]], _NT,
                                      preferred_element_type=F32)

    def softmax(src, p_dst, a_dst):
        for hh in range(2):
            for r in range(tq // rb):
                rs = slice(r * rb, (r + 1) * rb)
                s = src[hh, rs, :]
                m_old = m_scr[hh, rs, :]
                m_new = jnp.maximum(m_old, jnp.max(s, axis=-1, keepdims=True))
                p_dst[hh, rs, :] = jnp.exp2(s - m_new).astype(BF16)
                m_scr[hh, rs, :] = m_new
                a_dst[hh, rs, :] = jnp.exp2(m_old - m_new)

    def values(c, p_src, a_src):
        for hh in range(2):
            acc_scr[hh] = a_src[hh] * acc_scr[hh] + jnp.dot(p_src[hh], v_ref[rows_of(c), hs[hh]],
                                                            preferred_element_type=F32)

    m_scr[...] = jnp.full(m_scr.shape, NEG_INF, F32)
    acc_scr[...] = jnp.zeros(acc_scr.shape, F32)
    scores(0, sa_scr)
    softmax(sa_scr, pa_scr, aa_scr)
    scores(1, sb_scr)

    def body(i, carry):
        values(2 * i, pa_scr, aa_scr)
        softmax(sb_scr, pb_scr, ab_scr)
        scores(2 * i + 2, sa_scr)
        values(2 * i + 1, pb_scr, ab_scr)
        softmax(sa_scr, pa_scr, aa_scr)
        scores(2 * i + 3, sb_scr)
        return carry

    lax.fori_loop(0, nk // 2 - 1, body, 0)
    values(nk - 2, pa_scr, aa_scr)
    softmax(sb_scr, pb_scr, ab_scr)
    values(nk - 1, pb_scr, ab_scr)
    acc0, acc1 = acc_scr[0], acc_scr[1]
    o0 = acc0 / pltpu.roll(acc0, MLA_V, 1)
    o1 = acc1 / pltpu.roll(acc1, MLA_V, 1)
    first = lax.broadcasted_iota(jnp.int32, (tq, LANE), 1) < MLA_V
    o_ref[...] = jnp.where(first, o0, o1).astype(o_ref.dtype)


def _mla_attention(q, k, v, *, batch, seq, tq, tk):
    n = batch * seq
    nqb = seq // tq
    assert (seq // tk) % 2 == 0
    return pl.pallas_call(
        functools.partial(_mla_attn_kernel, tk=tk, rb=MLA_ROW_BLOCK),
        grid=(batch, MLA_HEADS // 2, nqb),
        in_specs=[pl.BlockSpec((tq, 2 * MLA_HEAD_PAD), lambda b, h, i: (b * nqb + i, h)),
                  pl.BlockSpec((seq, 2 * MLA_HEAD_PAD), lambda b, h, i: (b, h)),
                  pl.BlockSpec((seq, 2 * MLA_HEAD_PAD), lambda b, h, i: (b, h))],
        out_specs=pl.BlockSpec((tq, 2 * MLA_V), lambda b, h, i: (b * nqb + i, h)),
        out_shape=jax.ShapeDtypeStruct((n, MLA_HEADS * MLA_V), BF16),
        scratch_shapes=[pltpu.VMEM((2, tq, tk), F32), pltpu.VMEM((2, tq, tk), F32),
                        pltpu.VMEM((2, tq, tk), BF16), pltpu.VMEM((2, tq, tk), BF16),
                        pltpu.VMEM((2, tq, 1), F32), pltpu.VMEM((2, tq, 1), F32),
                        pltpu.VMEM((2, tq, 1), F32), pltpu.VMEM((2, tq, LANE), F32)],
        compiler_params=_params(("parallel", "parallel", "arbitrary")),
        name="mla_attn",
    )(q, k, v)


def _merge_kernel(o0_ref, l0_ref, o1_ref, l1_ref, o2_ref, l2_ref, ob_ref, g_ref, x_ref,
                  wa_ref, wb_ref, wo_ref, xo_ref, nat_scr):
    tm = x_ref.shape[0]
    w = A_GROUP_WIDTH

    def natural(ref, slot, dil):
        if dil == 1:
            return ref[...].astype(F32)
        rows = tm // dil
        nc = w // LANE
        for r in range(dil):
            for c in range(nc):
                lo = r * w + c * LANE
                nat_scr[slot * nc + c, pl.ds(r, rows, stride=dil), :] = ref[:, lo:lo + LANE].astype(F32)
        return jnp.concatenate([nat_scr[slot * nc + c] for c in range(nc)], axis=1)

    dils = [dil for _, dil in A_GROUPS]
    l0, l1, l2 = (natural(ref, 2 * g, dils[g]) for g, ref in enumerate((l0_ref, l1_ref, l2_ref)))
    o0, o1, o2 = (natural(ref, 2 * g + 1, dils[g]) for g, ref in enumerate((o0_ref, o1_ref, o2_ref)))
    m = jnp.maximum(jnp.maximum(l0, l1), l2)
    e0, e1, e2 = jnp.exp(l0 - m), jnp.exp(l1 - m), jnp.exp(l2 - m)
    oa = (e0 * o0 + e1 * o1 + e2 * o2) / (e0 + e1 + e2)
    pa = jnp.dot(oa.astype(BF16), wa_ref[...], preferred_element_type=F32)
    pb = jnp.dot(ob_ref[...], wb_ref[...], preferred_element_type=F32)
    d = pa.shape[1]
    merged = g_ref[:, :d].astype(F32) * pa + g_ref[:, d:].astype(F32) * pb
    xo_ref[...] = x_ref[...] + jnp.dot(merged.astype(BF16), wo_ref[...], preferred_element_type=F32)


def _merge(oas, lses, ob, gates, x, wa, wb, wo, *, tm):
    n, d = x.shape
    row = lambda a: pl.BlockSpec((tm * a.shape[0] // n, a.shape[1]), lambda i: (i, 0))
    full = lambda a: pl.BlockSpec(a.shape, lambda i: (0, 0))
    args = [oas[0], lses[0], oas[1], lses[1], oas[2], lses[2], ob, gates, x]
    return pl.pallas_call(
        _merge_kernel,
        grid=(n // tm,),
        in_specs=[row(a) for a in args] + [full(wa), full(wb), full(wo)],
        out_specs=pl.BlockSpec((tm, d), lambda i: (i, 0)),
        out_shape=jax.ShapeDtypeStruct((n, d), F32),
        scratch_shapes=[pltpu.VMEM((2 * len(A_GROUPS) * A_GROUP_WIDTH // LANE, tm, LANE), F32)],
        compiler_params=_params(("parallel",)),
        name="merge",
    )(*args, wa, wb, wo)


def _mem_kv_kernel(mem_ref, g_ref, w_ref, gk_ref, k_ref, v_ref):
    mh = _rms(mem_ref[...], g_ref[...]).astype(BF16)
    kv = jnp.dot(mh, w_ref[...], preferred_element_type=F32)
    for h in range(MEM_HEADS):
        sl = slice(h * MEM_HEAD_DIM, (h + 1) * MEM_HEAD_DIM)
        k_ref[:, sl] = _rms(kv[:, sl], gk_ref[...]).astype(k_ref.dtype)
    v_ref[...] = kv[:, MEM_WIDTH:].astype(v_ref.dtype)


def _mem_kv(mem, g, w, gk, *, tm):
    n = mem.shape[0]
    full = lambda a: pl.BlockSpec(a.shape, lambda i: (0, 0))
    return pl.pallas_call(
        _mem_kv_kernel,
        grid=(n // tm,),
        in_specs=[pl.BlockSpec((tm, mem.shape[1]), lambda i: (i, 0)), full(g), full(w), full(gk)],
        out_specs=[pl.BlockSpec((tm, MEM_WIDTH), lambda i: (i, 0))] * 2,
        out_shape=[jax.ShapeDtypeStruct((n, MEM_WIDTH), BF16)] * 2,
        compiler_params=_params(("parallel",)),
        name="mem_kv",
    )(mem, g, w, gk)


def _mem_attn_kernel(x_ref, k_ref, v_ref, gx_ref, wq_ref, gq_ref, wo_ref, gf_ref, wr_ref,
                     xo_ref, hf_ref, aff_ref):
    x = x_ref[...]
    hx = _rms(x, gx_ref[...]).astype(BF16)
    q = jnp.dot(hx, wq_ref[...], preferred_element_type=F32)
    outs = []
    for h in range(MEM_HEADS):
        sl = slice(h * MEM_HEAD_DIM, (h + 1) * MEM_HEAD_DIM)
        qn = _rms(q[:, sl], gq_ref[...]).astype(BF16)
        s = lax.dot_general(qn, k_ref[:, sl], _NT, preferred_element_type=F32)
        m = jnp.max(s, axis=-1, keepdims=True)
        p = jnp.exp(s - m)
        l = jnp.sum(p, axis=-1, keepdims=True)
        outs.append(jnp.dot(p.astype(BF16), v_ref[:, sl], preferred_element_type=F32) / l)
    o = jnp.concatenate(outs, axis=1).astype(BF16)
    x2 = x + jnp.dot(o, wo_ref[...], preferred_element_type=F32)
    xo_ref[...] = x2
    hf = _rms(x2, gf_ref[...]).astype(BF16)
    hf_ref[...] = hf
    logits = jnp.dot(hf, wr_ref[...], preferred_element_type=F32)
    lane = lax.broadcasted_iota(jnp.int32, logits.shape, 1)
    logits = jnp.where(lane < N_EXPERTS, logits, NEG_INF)
    e = jnp.exp(logits - jnp.max(logits, axis=-1, keepdims=True))
    aff_ref[...] = e / jnp.sum(e, axis=-1, keepdims=True)


def _mem_attention(x, kmem, vmem, gx, wq, gq, wo, gf, wr, *, batch, seq, mem_tokens, tm):
    n, d = x.shape
    nsb = seq // tm
    full = lambda a: pl.BlockSpec(a.shape, lambda b, i: (0, 0))
    row = lambda w: pl.BlockSpec((tm, w), lambda b, i: (b * nsb + i, 0))
    kv = pl.BlockSpec((mem_tokens, MEM_WIDTH), lambda b, i: (b, 0))
    return pl.pallas_call(
        _mem_attn_kernel,
        grid=(batch, nsb),
        in_specs=[row(d), kv, kv, full(gx), full(wq), full(gq), full(wo), full(gf), full(wr)],
        out_specs=[row(d), row(d), row(LANE)],
        out_shape=[jax.ShapeDtypeStruct((n, d), F32),
                   jax.ShapeDtypeStruct((n, d), BF16),
                   jax.ShapeDtypeStruct((n, LANE), F32)],
        compiler_params=_params(("parallel", "parallel")),
        name="mem_attn",
    )(x, kmem, vmem, gx, wq, gq, wo, gf, wr)


def _ffn_kernel(x_ref, wg_ref, wu_ref, wd_ref, g_ref, y_ref):
    x = x_ref[...]
    a = jnp.dot(x, wg_ref[...], preferred_element_type=F32)
    b = jnp.dot(x, wu_ref[...], preferred_element_type=F32)
    hid = (a * jax.nn.sigmoid(a) * b).astype(BF16)
    y = jnp.dot(hid, wd_ref[...], preferred_element_type=F32)
    g = g_ref[...]
    y_ref[...] = (y * jnp.concatenate([g] * (y.shape[1] // LANE), axis=1)).astype(y_ref.dtype)


def _expert_ffn(xe, wg, wu, wd, gates, *, cap, tm):
    e, _, d = xe.shape
    f = wg.shape[2]
    return pl.pallas_call(
        _ffn_kernel,
        grid=(e, cap // tm),
        in_specs=[pl.BlockSpec((None, tm, d), lambda e, i: (e, i, 0)),
                  pl.BlockSpec((None, d, f), lambda e, i: (e, 0, 0)),
                  pl.BlockSpec((None, d, f), lambda e, i: (e, 0, 0)),
                  pl.BlockSpec((None, f, d), lambda e, i: (e, 0, 0)),
                  pl.BlockSpec((None, tm, LANE), lambda e, i: (e, i, 0))],
        out_specs=pl.BlockSpec((None, tm, d), lambda e, i: (e, i, 0)),
        out_shape=jax.ShapeDtypeStruct((e, cap, d), BF16),
        compiler_params=_params(("parallel", "arbitrary")),
        name="expert_ffn",
    )(xe, wg, wu, wd, gates)


SLOT_WINDOW = 128
ROUTE_TOKENS = 512
COMBINE_ROWS = 128
ROW_ALIGN = 16
DISPATCH_STAGES = 4


def _select_kernel(aff_ref, sel_ref, *, cap):
    bits = pltpu.bitcast(aff_ref[...], jnp.int32)
    ne, n = bits.shape
    count = lambda mask: jnp.sum(jnp.where(mask, 1, 0), axis=1, keepdims=True)

    def value_bit(i, t):
        cand = t | jnp.left_shift(1, 30 - i)
        return jnp.where(count(bits >= cand) >= cap, cand, t)

    thr = lax.fori_loop(0, 31, value_bit, jnp.zeros((ne, 1), jnp.int32))
    gt = bits > thr
    eq = bits == thr
    need = cap - count(gt)
    idx = lax.broadcasted_iota(jnp.int32, (ne, n), 1)
    nbits = max(1, (n - 1).bit_length())

    def index_bit(i, j):
        cand = j | jnp.left_shift(1, nbits - 1 - i)
        return jnp.where(count(jnp.logical_and(eq, idx < cand)) < need, cand, j)

    last = lax.fori_loop(0, nbits, index_bit, jnp.zeros((ne, 1), jnp.int32))
    sel = jnp.logical_or(gt, jnp.logical_and(eq, idx <= last))
    sel_ref[...] = jnp.where(sel, 1.0, 0.0).astype(sel_ref.dtype)


def _select(aff_t, *, cap):
    return pl.pallas_call(
        functools.partial(_select_kernel, cap=cap),
        out_shape=jax.ShapeDtypeStruct(aff_t.shape, BF16),
        compiler_params=pltpu.CompilerParams(vmem_limit_bytes=VMEM_LIMIT),
        name="route_select",
    )(aff_t)


def _slots_kernel(sel_ref, upper_ref, lower_ref, pos_ref, base_ref):
    s = sel_ref[...]
    incl = jnp.dot(s, upper_ref[...], preferred_element_type=F32)
    tot = jnp.broadcast_to(incl[:, LANE - 1:LANE], incl.shape).astype(BF16)
    base = jnp.dot(lower_ref[...], tot, preferred_element_type=F32)
    pos_ref[...] = jnp.where(s > 0, base + incl - 1.0, -1.0).astype(jnp.int32)
    base_ref[...] = base.astype(jnp.int32)


def _slots(sel3):
    ne, nt, _ = sel3.shape
    upper = (jnp.arange(LANE)[:, None] <= jnp.arange(LANE)[None, :]).astype(BF16)
    lower = (jnp.arange(nt)[None, :] < jnp.arange(nt)[:, None]).astype(BF16)
    blk = pl.BlockSpec((None, nt, LANE), lambda e: (e, 0, 0))
    return pl.pallas_call(
        _slots_kernel,
        grid=(ne,),
        in_specs=[blk, pl.BlockSpec(upper.shape, lambda e: (0, 0)), pl.BlockSpec(lower.shape, lambda e: (0, 0))],
        out_specs=[blk, blk],
        out_shape=[jax.ShapeDtypeStruct(sel3.shape, jnp.int32)] * 2,
        compiler_params=_params(("parallel",)),
        name="route_slots",
    )(sel3, upper, lower)


def _dispatch_kernel(offs_ref, h_ref, pos_ref, aff_ref, xe_ref, gs_ref,
                     rows_scr, grows_scr, carry_scr, gcarry_scr, stage_scr, gstage_scr, sem, gsem, *, nb, cap):
    b = pl.program_id(0)
    t = h_ref.shape[0]
    w = SLOT_WINDOW
    al = ROW_ALIGN

    def xe_copy(e, start):
        return pltpu.make_async_copy(
            stage_scr.at[e], xe_ref.at[e, pl.ds(pl.multiple_of(start, al), w), :], sem.at[e])

    def gs_copy(e, start):
        return pltpu.make_async_copy(
            gstage_scr.at[e], gs_ref.at[e, pl.ds(pl.multiple_of(start, al), w), :], gsem.at[e])

    @pl.when(b == 0)
    def _():
        carry_scr[...] = jnp.zeros(carry_scr.shape, F32)
        gcarry_scr[...] = jnp.zeros(gcarry_scr.shape, F32)
        stage_scr[...] = jnp.zeros(stage_scr.shape, BF16)
        gstage_scr[...] = jnp.zeros(gstage_scr.shape, F32)
        for e in range(N_EXPERTS):
            xe_copy(e, 0).start()
            gs_copy(e, 0).start()

    hb = h_ref[...]
    slot_iota = lax.broadcasted_iota(jnp.int32, (w, t), 0)

    def window(e, start):
        place = jnp.where(slot_iota == pos_ref[e:e + 1, :] - start, 1.0, 0.0)
        rows = jnp.dot(place.astype(BF16), hb, preferred_element_type=F32)
        g = jnp.sum(place * aff_ref[e:e + 1, :], axis=1, keepdims=True)
        return rows, jnp.broadcast_to(g, (w, LANE))

    def stage(e, rows, g):
        rows_scr[e] = rows
        grows_scr[e] = g
        stage_scr[e] = rows.astype(BF16)
        gstage_scr[e] = g

    def emit(e, start, rows, g):
        stage(e, rows, g)
        xe_copy(e, start).start()
        gs_copy(e, start).start()

    for e in range(N_EXPERTS):
        xe_copy(e, 0).wait()
        gs_copy(e, 0).wait()
    for e in range(N_EXPERTS):
        start = (offs_ref[e * (nb + 1) + b] // al) * al
        rows, g = window(e, start)
        rows = jnp.concatenate([rows[:al] + carry_scr[e], rows[al:]], axis=0)
        g = jnp.concatenate([g[:al] + gcarry_scr[e], g[al:]], axis=0)
        stage(e, rows, g)
    for e in range(N_EXPERTS):
        start = (offs_ref[e * (nb + 1) + b] // al) * al
        xe_copy(e, start).start()
        gs_copy(e, start).start()

    for e in range(N_EXPERTS):
        start = (offs_ref[e * (nb + 1) + b] // al) * al
        o1 = offs_ref[e * (nb + 1) + b + 1]
        nwin = jnp.maximum((o1 - start + w - 1) // w, 1)

        def extra(j, carry, e=e, start=start):
            rows, g = window(e, start + j * w)
            xe_copy(e, 0).wait()
            gs_copy(e, 0).wait()
            emit(e, start + j * w, rows, g)
            return carry

        lax.fori_loop(1, nwin, extra, 0)
        nxt = (o1 // al) * al - (start + (nwin - 1) * w)
        at = pl.ds(pl.multiple_of(jnp.minimum(nxt, w - al), al), al)
        keep = nxt < w
        carry_scr[e] = jnp.where(keep, rows_scr[e, at, :], 0.0)
        gcarry_scr[e] = jnp.where(keep, grows_scr[e, at, :], 0.0)

    @pl.when(b == nb - 1)
    def _():
        for e in range(N_EXPERTS):
            xe_copy(e, 0).wait()
            gs_copy(e, 0).wait()
        stage_scr[...] = jnp.zeros(stage_scr.shape, BF16)
        gstage_scr[...] = jnp.zeros(gstage_scr.shape, F32)
        for e in range(N_EXPERTS):
            xe_copy(e, cap).start()
            gs_copy(e, cap).start()
        for e in range(N_EXPERTS):
            xe_copy(e, cap).wait()
            gs_copy(e, cap).wait()


def _dispatch(offs, h, pos, aff, *, cap):
    n, d = h.shape
    t = min(ROUTE_TOKENS, n)
    nb = n // t
    ne = N_EXPERTS
    w = SLOT_WINDOW
    grid_spec = pltpu.PrefetchScalarGridSpec(
        num_scalar_prefetch=1,
        grid=(nb,),
        in_specs=[pl.BlockSpec((t, d), lambda b, offs: (b, 0)),
                  pl.BlockSpec((ne, t), lambda b, offs: (0, b)),
                  pl.BlockSpec((ne, t), lambda b, offs: (0, b))],
        out_specs=[pl.BlockSpec(memory_space=pl.ANY), pl.BlockSpec(memory_space=pl.ANY)],
        scratch_shapes=[pltpu.VMEM((ne, w, d), F32), pltpu.VMEM((ne, w, LANE), F32),
                        pltpu.VMEM((ne, ROW_ALIGN, d), F32), pltpu.VMEM((ne, ROW_ALIGN, LANE), F32),
                        pltpu.VMEM((ne, w, d), BF16), pltpu.VMEM((ne, w, LANE), F32),
                        pltpu.SemaphoreType.DMA((ne,)), pltpu.SemaphoreType.DMA((ne,))])
    return pl.pallas_call(
        functools.partial(_dispatch_kernel, nb=nb, cap=cap),
        grid_spec=grid_spec,
        out_shape=[jax.ShapeDtypeStruct((ne, cap + w, d), BF16),
                   jax.ShapeDtypeStruct((ne, cap + w, LANE), F32)],
        compiler_params=_params(("arbitrary",)),
        name="route_dispatch",
    )(offs, h, pos, aff)


_TN = (((0,), (0,)), ((), ()))


def _combine_kernel(offs_ref, x_ref, pos_ref, ye_ref, o_ref, buf_scr, sem, *, nb, cap):
    b = pl.program_id(0)
    t = x_ref.shape[0]
    rows = buf_scr.shape[2] // 2
    al = ROW_ALIGN
    par = b % 2

    def first_row(e, blk):
        return jnp.minimum((offs_ref[e * (nb + 1) + blk] // al) * al, cap - rows)

    def fetch(e, start, p):
        return pltpu.make_async_copy(
            ye_ref.at[e, pl.ds(pl.multiple_of(start, al), rows), :],
            buf_scr.at[p, e // 2, pl.ds((e % 2) * rows, rows), :], sem.at[p, e])

    @pl.when(b == 0)
    def _():
        for e in range(N_EXPERTS):
            fetch(e, first_row(e, 0), 0).start()

    @pl.when(b + 1 < nb)
    def _():
        for e in range(N_EXPERTS):
            fetch(e, first_row(e, b + 1), 1 - par).start()

    for e in range(N_EXPERTS):
        fetch(e, first_row(e, b), par).wait()
    slot_iota = lax.broadcasted_iota(jnp.int32, (rows, t), 0)
    total = x_ref[...]
    for pair in range(N_EXPERTS // 2):
        places = [jnp.where(slot_iota == pos_ref[e:e + 1, :] - first_row(e, b), 1.0, 0.0).astype(BF16)
                  for e in (2 * pair, 2 * pair + 1)]
        place = jnp.concatenate(places, axis=0)
        total = total + lax.dot_general(place, buf_scr[par, pair], _TN, preferred_element_type=F32)
    o_ref[...] = total

    for e in range(N_EXPERTS):
        start = first_row(e, b)
        o1 = offs_ref[e * (nb + 1) + b + 1]
        half = pl.ds((e % 2) * rows, rows)

        def more(j, carry, e=e, start=start, half=half):
            lo = start + j * rows
            ws = jnp.minimum(lo, cap - rows)
            cp = fetch(e, ws, par)
            cp.start()
            cp.wait()
            prow = pos_ref[e:e + 1, :]
            hit = jnp.logical_and(slot_iota == prow - ws, prow >= lo)
            o_ref[...] += lax.dot_general(jnp.where(hit, 1.0, 0.0).astype(BF16),
                                          buf_scr[par, e // 2, half, :], _TN, preferred_element_type=F32)
            return carry

        lax.fori_loop(1, (jnp.maximum(o1 - start, 1) + rows - 1) // rows, more, 0)


def _combine(offs, x, pos, ye):
    n, d = x.shape
    ne, cap, _ = ye.shape
    t = min(ROUTE_TOKENS, n)
    nb = n // t
    rows = min(COMBINE_ROWS, cap)
    grid_spec = pltpu.PrefetchScalarGridSpec(
        num_scalar_prefetch=1,
        grid=(nb,),
        in_specs=[pl.BlockSpec((t, d), lambda b, offs: (b, 0)),
                  pl.BlockSpec((ne, t), lambda b, offs: (0, b)),
                  pl.BlockSpec(memory_space=pl.ANY)],
        out_specs=pl.BlockSpec((t, d), lambda b, offs: (b, 0)),
        scratch_shapes=[pltpu.VMEM((2, ne // 2, 2 * rows, d), BF16), pltpu.SemaphoreType.DMA((2, ne))])
    return pl.pallas_call(
        functools.partial(_combine_kernel, nb=nb, cap=cap),
        grid_spec=grid_spec,
        out_shape=jax.ShapeDtypeStruct((n, d), F32),
        compiler_params=_params(("arbitrary",)),
        name="route_combine",
    )(offs, x, pos, ye)


def _moe(x, hf, aff, lp):
    n, d = x.shape
    ne = N_EXPERTS
    cap = EC_CAPACITY_FACTOR * n // ne
    t = min(ROUTE_TOKENS, n)
    nb = n // t
    aff_t = aff[:, :ne].T
    sel = _select(aff_t, cap=cap)
    pos3, base3 = _slots(sel.reshape(ne, n // LANE, LANE))
    pos = pos3.reshape(ne, n)
    offs = jnp.concatenate([base3[:, ::t // LANE, 0], jnp.full((ne, 1), cap, jnp.int32)], axis=1)
    offs = offs.reshape(-1)
    xe, gs = _dispatch(offs, hf, pos, aff_t, cap=cap)
    ye = _expert_ffn(xe, lp["w_gate"], lp["w_up"], lp["w_down"], gs, cap=cap, tm=min(512, cap))
    return _combine(offs, x, pos, ye)


def _t5_bucket(rel):
    half = REL_BUCKETS // 2
    max_exact = half // 2
    n = jnp.abs(rel)
    base = jnp.where(rel > 0, half, 0)
    nf = jnp.maximum(n, 1).astype(F32)
    large = max_exact + (jnp.log(nf / max_exact) / math.log(REL_MAX_DIST / max_exact)
                         * (half - max_exact)).astype(jnp.int32)
    large = jnp.minimum(large, half - 1)
    return base + jnp.where(n < max_exact, n, large)


def _band_bias(rel_bias, group, dilation, tq):
    tk = tq + 2 * A_HALF
    rel = jnp.arange(tk)[None, :] - A_HALF - jnp.arange(tq)[:, None]
    heads = slice(group * A_HEADS_PER_GROUP, (group + 1) * A_HEADS_PER_GROUP)
    onehot = jax.nn.one_hot(_t5_bucket(rel * dilation), REL_BUCKETS, dtype=F32)
    b = jnp.einsum("qkb,bh->hqk", onehot, rel_bias[:, heads].astype(F32), precision=lax.Precision.HIGHEST)
    return jnp.where((jnp.abs(rel) <= A_HALF)[None], b, NEG_INF)


def _block_diag_ones(size, block):
    idx = jnp.arange(size) // block
    return (idx[:, None] == idx[None, :]).astype(BF16)


def _rope_tables(seq):
    half = MLA_ROPE // 2
    freqs = ROPE_THETA ** (-jnp.arange(half, dtype=F32) / half)
    ang = jnp.arange(seq).astype(F32)[:, None] * freqs[None, :]
    cos, sin = jnp.cos(ang), jnp.sin(ang)
    ones = jnp.ones((seq, MLA_NOPE), F32)
    z64 = jnp.zeros((seq, MLA_NOPE), F32)
    z16 = jnp.zeros((seq, half), F32)
    z32 = jnp.zeros((seq, LANE - MLA_QK_DIM), F32)
    c = jnp.concatenate([ones, cos, cos, z32], axis=1)
    s1 = jnp.concatenate([z64, -sin, z16, z32], axis=1)
    s2 = jnp.concatenate([z64, z16, sin, z32], axis=1)
    return c, s1, s2


def _pad_heads(w, heads, width, lo, hi):
    w = w.reshape(w.shape[0], heads, width)[:, :, lo:hi]
    w = jnp.pad(w, ((0, 0), (0, 0), (0, MLA_HEAD_PAD - (hi - lo))))
    return w.reshape(w.shape[0], heads * MLA_HEAD_PAD)


def _prep_layer(p, l):
    d = D_MODEL
    row = lambda v: v.reshape(1, -1).astype(F32)
    w_in = p["w_in"][l]
    out = {}
    out["g_mix"] = row(p["norm_mix"][l])
    ng = len(A_GROUPS)
    out["w_a"] = w_in[:, :COL_CQ].reshape(d, 3, ng, A_GROUP_WIDTH).transpose(0, 2, 1, 3).reshape(
        d, COL_CQ).astype(BF16)
    out["aux_a"] = row(jnp.tile(jnp.concatenate([
        jnp.tile(p["a_q_norm"][l], A_HEADS_PER_GROUP) * (A_HEAD_DIM ** -0.5),
        jnp.tile(p["a_k_norm"][l], A_HEADS_PER_GROUP),
        jnp.ones((A_GROUP_WIDTH,), F32)]), ng))
    out["w_c"] = jnp.pad(w_in[:, COL_CQ:COL_GA], ((0, 0), (0, d - (COL_GA - COL_CQ)))).astype(BF16)
    out["w_g"] = w_in[:, COL_GA:].astype(BF16)
    out["g_ql"] = row(p["mla_q_lat_norm"][l])
    out["g_kl"] = row(p["mla_kv_lat_norm"][l])
    out["w_uq"] = _pad_heads(p["w_uq"][l], MLA_HEADS, MLA_QK_DIM, 0, MLA_QK_DIM).astype(BF16)
    w_ukv = p["w_ukv"][l]
    k_nope = _pad_heads(w_ukv, MLA_HEADS, MLA_NOPE + MLA_V, 0, MLA_NOPE)
    eye = jnp.pad(jnp.eye(MLA_ROPE, dtype=F32), ((0, 0), (MLA_NOPE, MLA_HEAD_PAD - MLA_QK_DIM)))
    k_pe = jnp.tile(eye, (1, MLA_HEADS))
    k_rows = d - MLA_Q_RANK - MLA_KV_RANK - MLA_ROPE
    out["w_uk"] = jnp.concatenate(
        [k_nope, k_pe, jnp.zeros((k_rows, MLA_HEADS * MLA_HEAD_PAD), F32)], axis=0).astype(BF16)
    out["w_uv"] = w_ukv.reshape(MLA_KV_RANK, MLA_HEADS, MLA_NOPE + MLA_V)[:, :, MLA_NOPE:].reshape(
        MLA_KV_RANK, MLA_HEADS * MLA_V).astype(BF16)
    pad_gain = lambda g: jnp.tile(jnp.pad(g, (0, MLA_HEAD_PAD - MLA_QK_DIM)), MLA_HEADS)
    out["g_q"] = row(pad_gain(p["mla_q_norm"][l]) * (MLA_QK_DIM ** -0.5 * math.log2(math.e)))
    out["g_k"] = row(pad_gain(p["mla_k_norm"][l]))
    out["w_pa"] = p["w_proj_a"][l].astype(BF16)
    out["w_pb"] = p["w_proj_b"][l].astype(BF16)
    out["w_out"] = p["w_out"][l].astype(BF16)
    out["g_mx"] = row(p["norm_mem_x"][l])
    out["g_mkv"] = row(p["norm_mem_kv"][l])
    out["w_mq"] = p["w_mq"][l].astype(BF16)
    out["w_mkv"] = p["w_mkv"][l].astype(BF16)
    out["g_mq"] = row(p["mem_q_norm"][l] * (MEM_HEAD_DIM ** -0.5))
    out["g_mk"] = row(p["mem_k_norm"][l])
    out["w_mo"] = p["w_mo"][l].astype(BF16)
    out["g_ffn"] = row(p["norm_ffn"][l])
    out["w_router"] = jnp.pad(p["w_router"][l], ((0, 0), (0, LANE - N_EXPERTS))).astype(BF16)
    out["w_gate"] = p["w_gate"][l].astype(BF16)
    out["w_up"] = p["w_up"][l].astype(BF16)
    out["w_down"] = p["w_down"][l].astype(BF16)
    return out


BAND_TQ = 128
BAND_STEP_ROWS = 512
MLA_ROW_BLOCK = 32


def _layer(x, mem, lp, shared, *, batch, seq):
    n = batch * seq
    mem_tokens = mem.shape[0] // batch
    tm = min(1024, n)
    bd64, bd128 = shared["bd64"], shared["bd128"]
    zgs = _norm_proj_heads(x, lp["g_mix"], lp["w_a"], lp["aux_a"], bd64, tm=min(512, n))
    zc = _norm_proj(x, lp["g_mix"], lp["w_c"], mode="plain", tm=tm, tn=D_MODEL, out_dtype=F32)
    gates = _norm_proj(x, lp["g_mix"], lp["w_g"], mode="sigmoid", tm=tm, tn=D_MODEL, out_dtype=BF16)
    oas, lses = [], []
    for g, (_, dil) in enumerate(A_GROUPS):
        o, lse = _band_attention(zgs[g], shared["band_bias"][g], shared["hmask"], batch=batch, seq=seq,
                                 group=g, dilation=dil, tq=BAND_TQ)
        oas.append(o)
        lses.append(lse)
    qm, km, vm = _mla_prep(zc, lp["g_ql"], lp["g_kl"], lp["w_uq"], lp["w_uk"], lp["w_uv"],
                           lp["g_q"], lp["g_k"], shared["rope_c"], shared["rope_s1"],
                           shared["rope_s2"], bd128, seq=seq, tm=min(512, seq))
    ob = _mla_attention(qm, km, vm, batch=batch, seq=seq, tq=min(512, seq), tk=min(512, seq))
    x = _merge(oas, lses, ob, gates, x, lp["w_pa"], lp["w_pb"], lp["w_out"], tm=min(512, n))
    kmem, vmem = _mem_kv(mem, lp["g_mkv"], lp["w_mkv"], lp["g_mk"], tm=min(512, mem.shape[0]))
    x, hf, aff = _mem_attention(x, kmem, vmem, lp["g_mx"], lp["w_mq"], lp["g_mq"], lp["w_mo"],
                                lp["g_ffn"], lp["w_router"], batch=batch, seq=seq,
                                mem_tokens=mem_tokens, tm=min(512, seq))
    return _moe(x, hf, aff, lp)


def _trunk(x, mem, layers, shared):
    batch, seq, d = x.shape
    xf = x.reshape(batch * seq, d)
    memf = mem.reshape(-1, d)
    for lp in layers:
        xf = _layer(xf, memf, lp, shared, batch=batch, seq=seq)
    return xf.reshape(batch, seq, d)


def kernel(x_prompt, x_sample, mem_prompt, mem_sample, norm_mix, w_in, a_q_norm, a_k_norm, rel_bias,
           mla_q_lat_norm, w_uq, mla_kv_lat_norm, w_ukv, mla_q_norm, mla_k_norm, w_proj_a, w_proj_b,
           w_out, norm_mem_x, norm_mem_kv, w_mq, w_mkv, mem_q_norm, mem_k_norm, w_mo, norm_ffn,
           w_router, w_gate, w_up, w_down):
    p = dict(norm_mix=norm_mix, w_in=w_in, a_q_norm=a_q_norm, a_k_norm=a_k_norm,
             mla_q_lat_norm=mla_q_lat_norm, w_uq=w_uq, mla_kv_lat_norm=mla_kv_lat_norm, w_ukv=w_ukv,
             mla_q_norm=mla_q_norm, mla_k_norm=mla_k_norm, w_proj_a=w_proj_a, w_proj_b=w_proj_b,
             w_out=w_out, norm_mem_x=norm_mem_x, norm_mem_kv=norm_mem_kv, w_mq=w_mq, w_mkv=w_mkv,
             mem_q_norm=mem_q_norm, mem_k_norm=mem_k_norm, w_mo=w_mo, norm_ffn=norm_ffn,
             w_router=w_router, w_gate=w_gate, w_up=w_up, w_down=w_down)
    layers = [_prep_layer(p, l) for l in range(DEPTH)]
    outs = []
    for x, mem in ((x_prompt, mem_prompt), (x_sample, mem_sample)):
        seq = x.shape[1]
        c, s1, s2 = _rope_tables(seq)
        hm = (jnp.arange(LANE)[None, :] // A_HEAD_DIM == jnp.arange(16)[:, None]).astype(BF16)
        shared = dict(
            bd64=_block_diag_ones(256, A_HEAD_DIM), bd128=_block_diag_ones(256, MLA_HEAD_PAD),
            band_bias=[_band_bias(rel_bias, g, dil, BAND_TQ) for g, (_, dil) in enumerate(A_GROUPS)],
            hmask=hm, rope_c=c, rope_s1=s1, rope_s2=s2)
        outs.append(_trunk(x, mem, layers, shared))
    return tuple(outs)
```
